```python
import math
import jax, jax.numpy as jnp
from jax import lax
import numpy as np

D_MODEL = 2048
BATCH = 4
SEQ = 2048
DEPTH = 1

SSM_WIDTH = D_MODEL // 2
SSM_GROUP = 16
SSM_GROUPS = SSM_WIDTH // SSM_GROUP
SSM_STATE = 64
ML_HEADS = 8
ML_HEAD_DIM = D_MODEL // 16
ML_WIDTH = ML_HEADS * ML_HEAD_DIM
ML_CHUNK = 64
QK_CONV = 4
PEER_HEADS = 8
PEER_NKEYS = 128
PEER_EXPERTS = PEER_NKEYS * PEER_NKEYS
PEER_QDIM = 256
PEER_HALF = PEER_QDIM // 2
PEER_TOPK = 16
PEER_TOKEN_BLOCK = 128
PLE_DIM = 256
IN_WIDTHS = (SSM_WIDTH, ML_WIDTH, ML_WIDTH, ML_WIDTH, ML_WIDTH, ML_HEADS, ML_HEADS, 2 * D_MODEL)
IN_WIDTH = sum(IN_WIDTHS)
LN_EPS = 1e-5
DN_ALPHA = (2 * DEPTH) ** 0.25
DN_BETA = (8 * DEPTH) ** -0.25

kernel_name = "hybrid_s5_mlstm_peer_deepnorm"


def layer_norm(x, g, b):
    xf = x.astype(jnp.float32)
    mu = jnp.mean(xf, axis=-1, keepdims=True)
    var = jnp.mean(jnp.square(xf - mu), axis=-1, keepdims=True)
    y = (xf - mu) * lax.rsqrt(var + LN_EPS) * g.astype(jnp.float32) + b.astype(jnp.float32)
    return y.astype(x.dtype)


def causal_conv(x, w):
    k_len = w.shape[0]
    s = x.shape[1]
    xp = jnp.pad(x, ((0, 0), (k_len - 1, 0), (0, 0)))
    out = xp[:, 0:s] * w[0]
    for j in range(1, k_len):
        out = out + xp[:, j:j + s] * w[j]
    return out


def _complex_linear_combine(e1, e2):
    a1r, a1i, b1r, b1i = e1
    a2r, a2i, b2r, b2i = e2
    return (a2r * a1r - a2i * a1i,
            a2r * a1i + a2i * a1r,
            a2r * b1r - a2i * b1i + b2r,
            a2r * b1i + a2i * b1r + b2i)


def s5_branch(u, a_re, a_im, log_dt, b_re, b_im, c_re, c_im, d_skip, w_glu, b_glu):
    bsz, s, _ = u.shape
    f32 = jnp.float32
    uf = u.astype(f32)
    ar, ai = a_re.astype(f32), a_im.astype(f32)
    dt = jnp.exp(log_dt.astype(f32))[:, None]
    mag = jnp.exp(dt * ar)
    lam_r = mag * jnp.cos(dt * ai)
    lam_i = mag * jnp.sin(dt * ai)
    den = ar * ar + ai * ai
    zr = ((lam_r - 1.0) * ar + lam_i * ai) / den
    zi = (lam_i * ar - (lam_r - 1.0) * ai) / den
    br, bi = b_re.astype(f32), b_im.astype(f32)
    bbar_r = zr[..., None] * br - zi[..., None] * bi
    bbar_i = zr[..., None] * bi + zi[..., None] * br
    ug = uf.reshape(bsz, s, SSM_GROUPS, SSM_GROUP)
    bu_r = jnp.einsum('bsgc,gpc->bsgp', ug, bbar_r)
    bu_i = jnp.einsum('bsgc,gpc->bsgp', ug, bbar_i)
    lr = jnp.broadcast_to(lam_r, bu_r.shape)
    li = jnp.broadcast_to(lam_i, bu_r.shape)
    _, _, st_r, st_i = lax.associative_scan(_complex_linear_combine, (lr, li, bu_r, bu_i), axis=1)
    y = (jnp.einsum('gcp,bsgp->bsgc', c_re.astype(f32), st_r)
         - jnp.einsum('gcp,bsgp->bsgc', c_im.astype(f32), st_i))
    y = y.reshape(bsz, s, SSM_WIDTH) + d_skip.astype(f32) * uf
    y = jax.nn.gelu(y)
    y = y * jax.nn.sigmoid(y @ w_glu.astype(f32) + b_glu.astype(f32))
    return y.astype(u.dtype)


def mlstm_chunkwise(q, k, v, i_pre, f_pre):
    bsz, s, nh, dh = q.shape
    nc = s // ML_CHUNK
    f32 = jnp.float32

    def to_chunks(t):
        return t.astype(f32).reshape(bsz, nc, ML_CHUNK, nh, -1).transpose(0, 3, 1, 2, 4)

    qc = to_chunks(q)
    kc = to_chunks(k) / math.sqrt(dh)
    vc = to_chunks(v)
    ig = i_pre.astype(f32).reshape(bsz, nc, ML_CHUNK, nh).transpose(0, 3, 1, 2)
    logf = jax.nn.log_sigmoid(f_pre.astype(f32)).reshape(bsz, nc, ML_CHUNK, nh).transpose(0, 3, 1, 2)
    bcum = jnp.cumsum(logf, axis=-1)
    gtot = bcum[..., -1]

    causal = jnp.tril(jnp.ones((ML_CHUNK, ML_CHUNK), dtype=bool))
    dmat = bcum[..., :, None] - bcum[..., None, :] + ig[..., None, :]
    dmat = jnp.where(causal, dmat, -jnp.inf)

    wlog = gtot[..., None] - bcum + ig
    m_loc = jnp.max(wlog, axis=-1)
    wts = jnp.exp(wlog - m_loc[..., None])
    c_chunk = jnp.einsum('bhcl,bhcld,bhcle->bhcde', wts, kc, vc)
    n_chunk = jnp.einsum('bhcl,bhcld->bhcd', wts, kc)

    def step(carry, xs):
        c_st, n_st, m_st = carry
        g_c, ml_c, cc_c, nc_c = xs
        m_new = jnp.maximum(g_c + m_st, ml_c)
        a = jnp.exp(g_c + m_st - m_new)
        bb = jnp.exp(ml_c - m_new)
        c_new = a[..., None, None] * c_st + bb[..., None, None] * cc_c
        n_new = a[..., None] * n_st + bb[..., None] * nc_c
        return (c_new, n_new, m_new), (c_st, n_st, m_st)

    init = (jnp.zeros((bsz, nh, dh, dh), f32), jnp.zeros((bsz, nh, dh), f32), jnp.zeros((bsz, nh), f32))
    xs = (jnp.moveaxis(gtot, 2, 0), jnp.moveaxis(m_loc, 2, 0),
          jnp.moveaxis(c_chunk, 2, 0), jnp.moveaxis(n_chunk, 2, 0))
    _, (c_prev, n_prev, m_prev) = lax.scan(step, init, xs)
    c_prev = jnp.moveaxis(c_prev, 0, 2)
    n_prev = jnp.moveaxis(n_prev, 0, 2)
    m_prev = jnp.moveaxis(m_prev, 0, 2)

    inter_log = bcum + m_prev[..., None]
    m_row = jnp.maximum(inter_log, jnp.max(dmat, axis=-1))
    pw = jnp.exp(dmat - m_row[..., None])
    sc = jnp.einsum('bhcld,bhcsd->bhcls', qc, kc) * pw
    inter_scale = jnp.exp(inter_log - m_row)
    num = (jnp.einsum('bhcls,bhcse->bhcle', sc, vc)
           + inter_scale[..., None] * jnp.einsum('bhcld,bhcde->bhcle', qc, c_prev))
    den = jnp.sum(sc, axis=-1) + inter_scale * jnp.einsum('bhcld,bhcd->bhcl', qc, n_prev)
    h = num / jnp.maximum(jnp.abs(den), jnp.exp(-m_row))[..., None]
    h = h.transpose(0, 2, 3, 1, 4).reshape(bsz, s, nh, dh)
    return h


def head_norm(h, w):
    hf = h.astype(jnp.float32)
    mu = jnp.mean(hf, axis=-1, keepdims=True)
    var = jnp.mean(jnp.square(hf - mu), axis=-1, keepdims=True)
    y = (hf - mu) * lax.rsqrt(var + LN_EPS) * w.astype(jnp.float32).reshape(ML_HEADS, ML_HEAD_DIM)
    return y


def token_mixer(h, w_in, b_igate, b_fgate, conv_qk, mh_norm_w, a_re, a_im, log_dt, b_re, b_im,
                c_re, c_im, d_skip, w_glu, b_glu, w_up_ssm, w_up_ml, w_out):
    bsz, s, _ = h.shape
    z = h @ w_in
    bounds = list(np.cumsum(IN_WIDTHS)[:-1])
    u, q, k, v, o, ig, fg, gates = jnp.split(z, bounds, axis=-1)
    y_a = s5_branch(u, a_re, a_im, log_dt, b_re, b_im, c_re, c_im, d_skip, w_glu, b_glu)
    qk = jax.nn.silu(causal_conv(jnp.concatenate([q, k], axis=-1), conv_qk))
    q, k = jnp.split(qk, 2, axis=-1)
    shp = (bsz, s, ML_HEADS, ML_HEAD_DIM)
    hm = mlstm_chunkwise(q.reshape(shp), k.reshape(shp), v.reshape(shp), ig + b_igate, fg + b_fgate)
    hm = head_norm(hm, mh_norm_w).reshape(bsz, s, ML_WIDTH)
    y_b = (jax.nn.sigmoid(o.astype(jnp.float32)) * hm).astype(h.dtype)
    g_a, g_b = jnp.split(jax.nn.sigmoid(gates), 2, axis=-1)
    merged = g_a * (y_a @ w_up_ssm) + g_b * (y_b @ w_up_ml)
    return merged @ w_out


def peer(h, wq, keys, u_tab, v_tab):
    bsz, s, d = h.shape
    t = bsz * s
    xt = h.reshape(t, d)
    qh = (xt @ wq).reshape(t, PEER_HEADS, 2, PEER_HALF)
    sc = jnp.einsum('thjd,hjnd->thjn', qh, keys).astype(jnp.float32)
    s1, i1 = lax.top_k(sc[:, :, 0], PEER_TOPK)
    s2, i2 = lax.top_k(sc[:, :, 1], PEER_TOPK)
    cand = (s1[..., :, None] + s2[..., None, :]).reshape(t, PEER_HEADS, PEER_TOPK * PEER_TOPK)
    cidx = (i1[..., :, None] * PEER_NKEYS + i2[..., None, :]).reshape(t, PEER_HEADS, PEER_TOPK * PEER_TOPK)
    top_s, pos = lax.top_k(cand, PEER_TOPK)
    eidx = jnp.take_along_axis(cidx, pos, axis=-1)
    gate = jax.nn.softmax(top_s, axis=-1).astype(h.dtype)
    nb = t // PEER_TOKEN_BLOCK

    def block(args):
        xb, eb, gb = args
        ub = jnp.take(u_tab, eb, axis=0)
        vb = jnp.take(v_tab, eb, axis=0)
        act = jax.nn.gelu(jnp.einsum('td,thkd->thk', xb, ub)) * gb
        return jnp.einsum('thk,thkd->td', act, vb)

    out = lax.map(block, (xt.reshape(nb, PEER_TOKEN_BLOCK, d),
                          eidx.reshape(nb, PEER_TOKEN_BLOCK, PEER_HEADS, PEER_TOPK),
                          gate.reshape(nb, PEER_TOKEN_BLOCK, PEER_HEADS, PEER_TOPK)))
    return out.reshape(bsz, s, d)


def setup_inputs(seed: int = 0) -> dict:
    key = jax.random.key(seed)
    ks = jax.random.split(key, 32)
    f32 = jnp.float32

    def nrm(k, shape, scale):
        return jax.random.normal(k, shape, f32) * scale

    L = DEPTH
    a_im_base = jnp.pi * jnp.arange(SSM_STATE, dtype=f32)
    return {
        "x": nrm(ks[0], (BATCH, SEQ, D_MODEL), 1.0),
        "p": nrm(ks[1], (DEPTH, BATCH, SEQ, PLE_DIM), 1.0),
        "w_in": nrm(ks[2], (L, D_MODEL, IN_WIDTH), D_MODEL ** -0.5),
        "b_igate": nrm(ks[3], (L, ML_HEADS), 0.1),
        "b_fgate": jnp.linspace(3.0, 6.0, ML_HEADS, dtype=f32)[None] + nrm(ks[4], (L, ML_HEADS), 0.1),
        "conv_qk": nrm(ks[5], (L, QK_CONV, 2 * ML_WIDTH), QK_CONV ** -0.5),
        "mh_norm_w": 1.0 + nrm(ks[6], (L, ML_WIDTH), 0.02),
        "ssm_a_re": -0.5 + nrm(ks[7], (L, SSM_GROUPS, SSM_STATE), 0.01),
        "ssm_a_im": a_im_base[None, None] + nrm(ks[8], (L, SSM_GROUPS, SSM_STATE), 0.01),
        "ssm_log_dt": jax.random.uniform(ks[9], (L, SSM_GROUPS), f32, math.log(1e-3), math.log(1e-1)),
        "ssm_b_re": nrm(ks[10], (L, SSM_GROUPS, SSM_STATE, SSM_GROUP), (2 * SSM_GROUP) ** -0.5),
        "ssm_b_im": nrm(ks[11], (L, SSM_GROUPS, SSM_STATE, SSM_GROUP), (2 * SSM_GROUP) ** -0.5),
        "ssm_c_re": nrm(ks[12], (L, SSM_GROUPS, SSM_GROUP, SSM_STATE), SSM_STATE ** -0.5),
        "ssm_c_im": nrm(ks[13], (L, SSM_GROUPS, SSM_GROUP, SSM_STATE), SSM_STATE ** -0.5),
        "ssm_d": nrm(ks[14], (L, SSM_WIDTH), 1.0),
        "w_glu": nrm(ks[15], (L, SSM_WIDTH, SSM_WIDTH), SSM_WIDTH ** -0.5),
        "b_glu": nrm(ks[16], (L, SSM_WIDTH), 0.01),
        "w_up_ssm": nrm(ks[17], (L, SSM_WIDTH, D_MODEL), SSM_WIDTH ** -0.5),
        "w_up_ml": nrm(ks[18], (L, ML_WIDTH, D_MODEL), ML_WIDTH ** -0.5),
        "w_out": nrm(ks[19], (L, D_MODEL, D_MODEL), DN_BETA * D_MODEL ** -0.5),
        "ln1_g": 1.0 + nrm(ks[20], (L, D_MODEL), 0.02),
        "ln1_b": nrm(ks[21], (L, D_MODEL), 0.02),
        "peer_wq": nrm(ks[22], (L, D_MODEL, PEER_HEADS * PEER_QDIM), D_MODEL ** -0.5),
        "peer_keys": nrm(ks[23], (L, PEER_HEADS, 2, PEER_NKEYS, PEER_HALF), PEER_HALF ** -0.5),
        "peer_u": nrm(ks[24], (L, PEER_EXPERTS, D_MODEL), D_MODEL ** -0.5),
        "peer_v": nrm(ks[25], (L, PEER_EXPERTS, D_MODEL), DN_BETA * PEER_HEADS ** -0.5),
        "ln2_g": 1.0 + nrm(ks[26], (L, D_MODEL), 0.02),
        "ln2_b": nrm(ks[27], (L, D_MODEL), 0.02),
        "ple_w_gate": nrm(ks[28], (L, D_MODEL, D_MODEL), D_MODEL ** -0.5),
        "ple_w_proj": nrm(ks[29], (L, PLE_DIM, D_MODEL), PLE_DIM ** -0.5),
    }


def reference(x, p, w_in, b_igate, b_fgate, conv_qk, mh_norm_w, ssm_a_re, ssm_a_im, ssm_log_dt,
              ssm_b_re, ssm_b_im, ssm_c_re, ssm_c_im, ssm_d, w_glu, b_glu, w_up_ssm, w_up_ml,
              w_out, ln1_g, ln1_b, peer_wq, peer_keys, peer_u, peer_v, ln2_g, ln2_b,
              ple_w_gate, ple_w_proj):
    h = x
    for i in range(DEPTH):
        y = token_mixer(h, w_in[i], b_igate[i], b_fgate[i], conv_qk[i], mh_norm_w[i],
                        ssm_a_re[i], ssm_a_im[i], ssm_log_dt[i], ssm_b_re[i], ssm_b_im[i],
                        ssm_c_re[i], ssm_c_im[i], ssm_d[i], w_glu[i], b_glu[i],
                        w_up_ssm[i], w_up_ml[i], w_out[i])
        h = layer_norm(DN_ALPHA * h + y, ln1_g[i], ln1_b[i])
        y = peer(h, peer_wq[i], peer_keys[i], peer_u[i], peer_v[i])
        h = layer_norm(DN_ALPHA * h + y, ln2_g[i], ln2_b[i])
        h = h + jax.nn.sigmoid(h @ ple_w_gate[i]) * (p[i] @ ple_w_proj[i])
    return h
```

```python
import functools
import math

import jax
import jax.numpy as jnp
from jax import lax
from jax.experimental import pallas as pl
from jax.experimental.pallas import tpu as pltpu

F32 = jnp.float32
BF16 = jnp.bfloat16

LN_EPS = 1e-5
PEER_TOPK = 16
LANES = 128
SUBLANES = 8
VMEM_LIMIT = 56 * 1024 * 1024

ML_CHUNK = 128
S5_CHUNK = 128
S5_PITCH = S5_CHUNK + SUBLANES
NEG_INF = float("-inf")


def _cparams(*sem):
    return pltpu.CompilerParams(dimension_semantics=sem, vmem_limit_bytes=VMEM_LIMIT)


def _const_spec(shape):
    nd = len(shape)
    return pl.BlockSpec(shape, lambda *_: (0,) * nd, pipeline_mode=pl.Buffered(1))


def _mm_kernel(a_ref, w_ref, o_ref, *, act):
    acc = jnp.dot(a_ref[...], w_ref[...], preferred_element_type=F32)
    if act == "sigmoid":
        acc = jax.nn.sigmoid(acc)
    o_ref[...] = acc.astype(o_ref.dtype)


def _matmul(a, w, out_dtype, act=None, tm=1024, tn=512):
    m, k = a.shape
    n = w.shape[1]
    tm, tn = min(tm, m), min(tn, n)
    assert m % tm == 0 and n % tn == 0
    return pl.pallas_call(
        functools.partial(_mm_kernel, act=act),
        grid=(m // tm, n // tn),
        in_specs=[pl.BlockSpec((tm, k), lambda i, j: (i, 0)),
                  pl.BlockSpec((k, tn), lambda i, j: (0, j))],
        out_specs=pl.BlockSpec((tm, tn), lambda i, j: (i, j)),
        out_shape=jax.ShapeDtypeStruct((m, n), out_dtype),
        compiler_params=_cparams("parallel", "parallel"),
        name="in_proj",
    )(a, w)


def _s5_kernel(u_ref, bb_ref, cb_ref, lam_ref, y_ref, bu_ref, st_ref, *, bsz, chunk, pitch, ntile):
    c = pl.program_id(1)

    @pl.when(c == 0)
    def _():
        st_ref[...] = jnp.zeros_like(st_ref)

    bb = bb_ref[0]
    for b in range(bsz):
        bu = jnp.dot(u_ref[b].astype(BF16), bb, preferred_element_type=F32)
        for k in range(2 * ntile):
            bu_ref[k, b * pitch:b * pitch + chunk, :] = bu[:, k * LANES:(k + 1) * LANES]

    lam = lam_ref[0]
    lr = [jnp.broadcast_to(lam[k:k + 1, :], (bsz, LANES)) for k in range(ntile)]
    li = [jnp.broadcast_to(lam[ntile + k:ntile + k + 1, :], (bsz, LANES)) for k in range(ntile)]

    def step(t, carry):
        new = []
        for k in range(ntile):
            sr, si = carry[2 * k], carry[2 * k + 1]
            rows = pl.ds(t, bsz, stride=pitch)
            nr = lr[k] * sr - li[k] * si + bu_ref[k, rows, :]
            ni = lr[k] * si + li[k] * sr + bu_ref[ntile + k, rows, :]
            bu_ref[k, rows, :] = nr
            bu_ref[ntile + k, rows, :] = ni
            new += [nr, ni]
        return tuple(new)

    init = []
    for k in range(ntile):
        init += [st_ref[k], st_ref[ntile + k]]
    fin = lax.fori_loop(0, chunk, step, tuple(init), unroll=2)
    for k in range(ntile):
        st_ref[k] = fin[2 * k]
        st_ref[ntile + k] = fin[2 * k + 1]

    cb = cb_ref[0]
    for b in range(bsz):
        st = jnp.concatenate([bu_ref[k, b * pitch:b * pitch + chunk, :] for k in range(2 * ntile)], axis=1)
        y_ref[b] = jnp.dot(st.astype(BF16), cb, preferred_element_type=F32)


def _s5_scan(u, bb, cb, lam):
    bsz, s, w = u.shape
    nslab = w // LANES
    ntile = lam.shape[1] // 2
    chunk, pitch = S5_CHUNK, S5_PITCH
    assert s % chunk == 0 and bsz <= SUBLANES
    kern = functools.partial(_s5_kernel, bsz=bsz, chunk=chunk, pitch=pitch, ntile=ntile)
    return pl.pallas_call(
        kern,
        grid=(nslab, s // chunk),
        in_specs=[pl.BlockSpec((bsz, chunk, LANES), lambda j, c: (0, c, j)),
                  pl.BlockSpec((1, LANES, 2 * ntile * LANES), lambda j, c: (j, 0, 0)),
                  pl.BlockSpec((1, 2 * ntile * LANES, LANES), lambda j, c: (j, 0, 0)),
                  pl.BlockSpec((1, 2 * ntile, LANES), lambda j, c: (j, 0, 0))],
        out_specs=pl.BlockSpec((bsz, chunk, LANES), lambda j, c: (0, c, j)),
        out_shape=jax.ShapeDtypeStruct((bsz, s, w), F32),
        scratch_shapes=[pltpu.VMEM((2 * ntile, bsz * pitch, LANES), F32),
                        pltpu.VMEM((2 * ntile, bsz, LANES), F32)],
        compiler_params=_cparams("parallel", "arbitrary"),
        name="s5_scan",
    )(u, bb, cb, lam)


def _s5_params(a_re, a_im, log_dt, b_re, b_im, c_re, c_im):
    g, p = a_re.shape
    gc = b_re.shape[2]
    gps = LANES // gc
    nslab = g // gps
    dt = jnp.exp(log_dt)[:, None]
    mag = jnp.exp(dt * a_re)
    lam_r = mag * jnp.cos(dt * a_im)
    lam_i = mag * jnp.sin(dt * a_im)
    den = a_re * a_re + a_im * a_im
    zr = ((lam_r - 1.0) * a_re + lam_i * a_im) / den
    zi = (lam_i * a_re - (lam_r - 1.0) * a_im) / den
    bbar_r = zr[..., None] * b_re - zi[..., None] * b_im
    bbar_i = zr[..., None] * b_im + zi[..., None] * b_re
    eye = jnp.eye(gps, dtype=F32)

    def in_slab(bm):
        bm = bm.reshape(nslab, gps, p, gc)
        return jnp.einsum("jgpc,gh->jgchp", bm, eye).reshape(nslab, gps * gc, gps * p)

    def out_slab(cm):
        cm = cm.reshape(nslab, gps, gc, p)
        return jnp.einsum("jgcp,gh->jhpgc", cm, eye).reshape(nslab, gps * p, gps * gc)

    bb = jnp.concatenate([in_slab(bbar_r), in_slab(bbar_i)], axis=2).astype(BF16)
    cb = jnp.concatenate([out_slab(c_re), -out_slab(c_im)], axis=1).astype(BF16)
    ntile = gps * p // LANES
    lam = jnp.concatenate([lam_r.reshape(nslab, ntile, LANES), lam_i.reshape(nslab, ntile, LANES)], axis=1)
    return bb, cb, lam


def _log_sigmoid(x):
    return jnp.minimum(x, 0.0) - jnp.log1p(jnp.exp(-jnp.abs(x)))


def _mlstm_kernel(qk_ref, v_ref, o_ref, gc_ref, gr_ref, bc_ref, br_ref, cw_ref, nw_ref, y_ref,
                  buf_ref, c_ref, n_ref, m_ref, *, nh, dh, chunk, kconv):
    ci = pl.program_id(1)
    width = nh * dh
    halo = SUBLANES

    @pl.when(ci == 0)
    def _():
        buf_ref[0:halo, :] = jnp.zeros((halo, 2 * width), F32)
        c_ref[...] = jnp.zeros_like(c_ref)
        n_ref[...] = jnp.zeros_like(n_ref)
        m_ref[...] = jnp.zeros_like(m_ref)

    buf_ref[halo:halo + chunk, :] = qk_ref[0]
    conv = cw_ref[0:1, :] * buf_ref[halo - (kconv - 1):halo - (kconv - 1) + chunk, :]
    for j in range(1, kconv):
        off = halo - (kconv - 1) + j
        conv = conv + cw_ref[j:j + 1, :] * buf_ref[off:off + chunk, :]
    buf_ref[0:halo, :] = buf_ref[chunk:chunk + halo, :]
    qk = conv * jax.nn.sigmoid(conv)

    gcol = gc_ref[0] + bc_ref[...]
    col_id = lax.broadcasted_iota(jnp.int32, gcol.shape, 1)
    lcol = jnp.where(col_id >= nh, _log_sigmoid(gcol), gcol)
    grow = gr_ref[0] + br_ref[...]
    row_id = lax.broadcasted_iota(jnp.int32, grow.shape, 0)
    lrow = jnp.where(row_id >= nh, _log_sigmoid(grow), grow)
    tt = lax.broadcasted_iota(jnp.int32, (chunk, chunk), 0)
    ss = lax.broadcasted_iota(jnp.int32, (chunk, chunk), 1)
    causal = ss <= tt
    tril = causal.astype(F32)
    triu = (tt <= ss).astype(F32)
    bcol = jnp.dot(tril, lcol, preferred_element_type=F32, precision=lax.Precision.HIGHEST)
    brow = jnp.dot(lrow, triu, preferred_element_type=F32, precision=lax.Precision.HIGHEST)

    inv_sqrt = 1.0 / math.sqrt(dh)
    for h in range(nh):
        hs = slice(h * dh, (h + 1) * dh)
        qf = qk[:, hs]
        q = qf.astype(BF16)
        kf = qk[:, width + h * dh:width + (h + 1) * dh] * inv_sqrt
        k = kf.astype(BF16)
        v = v_ref[0, :, hs]
        b_c = bcol[:, nh + h:nh + h + 1]
        i_c = lcol[:, h:h + 1]
        b_r = brow[nh + h:nh + h + 1, :]
        i_r = lrow[h:h + 1, :]
        g_tot = b_c[chunk - 1:chunk, :]
        m_prev = m_ref[h][:, 0:1]
        c_prev = c_ref[h]
        n_prev = n_ref[h]

        dmat = jnp.where(causal, b_c - b_r + i_r, NEG_INF)
        inter_log = b_c + m_prev
        m_row = jnp.maximum(inter_log, jnp.max(dmat, axis=1, keepdims=True))
        pw = jnp.exp(dmat - m_row)
        sc = lax.dot_general(q, k, (((1,), (1,)), ((), ())), preferred_element_type=F32) * pw
        inter_scale = jnp.exp(inter_log - m_row)
        num = (jnp.dot(sc.astype(BF16), v, preferred_element_type=F32)
               + inter_scale * jnp.dot(q, c_prev.astype(BF16), preferred_element_type=F32))
        qn = jnp.sum(qf * n_prev, axis=1, keepdims=True)
        den = jnp.sum(sc, axis=1, keepdims=True) + inter_scale * qn
        hh = num / jnp.maximum(jnp.abs(den), jnp.exp(-m_row))
        mu = jnp.mean(hh, axis=1, keepdims=True)
        var = jnp.mean(jnp.square(hh - mu), axis=1, keepdims=True)
        hn = (hh - mu) * lax.rsqrt(var + LN_EPS) * nw_ref[:, hs]
        y_ref[0, :, hs] = (o_ref[0, :, hs].astype(F32) * hn).astype(y_ref.dtype)

        wlog = g_tot - b_c + i_c
        m_loc = jnp.max(wlog, axis=0, keepdims=True)
        kw = kf * jnp.exp(wlog - m_loc)
        c_chunk = lax.dot_general(kw.astype(BF16), v, (((0,), (0,)), ((), ())), preferred_element_type=F32)
        n_chunk = jnp.sum(kw, axis=0, keepdims=True)
        m_new = jnp.maximum(g_tot + m_prev, m_loc)
        a = jnp.exp(g_tot + m_prev - m_new)
        bb = jnp.exp(m_loc - m_new)
        c_ref[h] = a * c_prev + bb * c_chunk
        n_ref[h] = a * n_prev + bb * n_chunk
        m_ref[h] = jnp.broadcast_to(m_new, (1, LANES))


def _mlstm(qk, v, osig, gcol, grow, bias_c, bias_r, conv_w, norm_w, nh):
    bsz, s, w2 = qk.shape
    width = w2 // 2
    dh = width // nh
    chunk = ML_CHUNK
    kconv = conv_w.shape[0]
    assert s % chunk == 0 and kconv - 1 <= SUBLANES
    kern = functools.partial(_mlstm_kernel, nh=nh, dh=dh, chunk=chunk, kconv=kconv)
    return pl.pallas_call(
        kern,
        grid=(bsz, s // chunk),
        in_specs=[pl.BlockSpec((1, chunk, w2), lambda b, c: (b, c, 0)),
                  pl.BlockSpec((1, chunk, width), lambda b, c: (b, c, 0)),
                  pl.BlockSpec((1, chunk, width), lambda b, c: (b, c, 0)),
                  pl.BlockSpec((1, chunk, LANES), lambda b, c: (b, c, 0)),
                  pl.BlockSpec((1, 2 * nh, chunk), lambda b, c: (b, 0, c)),
                  _const_spec((1, LANES)),
                  _const_spec((2 * nh, 1)),
                  _const_spec((kconv, w2)),
                  _const_spec((1, width))],
        out_specs=pl.BlockSpec((1, chunk, width), lambda b, c: (b, c, 0)),
        out_shape=jax.ShapeDtypeStruct((bsz, s, width), BF16),
        scratch_shapes=[pltpu.VMEM((SUBLANES + chunk, w2), F32),
                        pltpu.VMEM((nh, dh, dh), F32),
                        pltpu.VMEM((nh, 1, dh), F32),
                        pltpu.VMEM((nh, 1, LANES), F32)],
        compiler_params=_cparams("parallel", "arbitrary"),
        name="mlstm",
    )(qk, v, osig, gcol, grow, bias_c, bias_r, conv_w, norm_w)


def _layer_norm(x, g, b):
    mu = jnp.mean(x, axis=-1, keepdims=True)
    var = jnp.mean(jnp.square(x - mu), axis=-1, keepdims=True)
    return (x - mu) * lax.rsqrt(var + LN_EPS) * g + b


def _mix_kernel(yp_ref, u_ref, yb_ref, ga_ref, gb_ref, x_ref, d_ref, wg_ref, bg_ref, wa_ref, wb_ref, wo_ref,
                lg_ref, lb_ref, h_ref, ht_ref, *, alpha):
    y = jax.nn.gelu(yp_ref[...] + d_ref[...] * u_ref[...])
    gate = jax.nn.sigmoid(jnp.dot(y.astype(BF16), wg_ref[...], preferred_element_type=F32) + bg_ref[...])
    ya = (y * gate).astype(BF16)
    merged = (ga_ref[...].astype(F32) * jnp.dot(ya, wa_ref[...], preferred_element_type=F32)
              + gb_ref[...].astype(F32) * jnp.dot(yb_ref[...], wb_ref[...], preferred_element_type=F32))
    mix = jnp.dot(merged.astype(BF16), wo_ref[...], preferred_element_type=F32)
    h = _layer_norm(alpha * x_ref[...] + mix, lg_ref[...], lb_ref[...])
    h_ref[...] = h
    ht_ref[...] = h.T.astype(BF16)


def _mix(y_pre, u, y_b, gates, x, d, w_glu, b_glu, w_a, w_b, w_o, ln_g, ln_b, alpha, tm=256):
    t, dm = x.shape
    w = u.shape[1]
    assert t % tm == 0
    row = lambda width: pl.BlockSpec((tm, width), lambda i: (i, 0))
    return pl.pallas_call(
        functools.partial(_mix_kernel, alpha=alpha),
        grid=(t // tm,),
        in_specs=[row(w), row(w), row(w),
                  pl.BlockSpec((tm, dm), lambda i: (i, 0)),
                  pl.BlockSpec((tm, dm), lambda i: (i, 1)),
                  row(dm),
                  _const_spec((1, w)), _const_spec((w, w)), _const_spec((1, w)),
                  _const_spec((w, dm)), _const_spec((w, dm)), _const_spec((dm, dm)),
                  _const_spec((1, dm)), _const_spec((1, dm))],
        out_specs=[pl.BlockSpec((tm, dm), lambda i: (i, 0)),
                   pl.BlockSpec((dm, tm), lambda i: (0, i))],
        out_shape=[jax.ShapeDtypeStruct((t, dm), F32), jax.ShapeDtypeStruct((dm, t), BF16)],
        compiler_params=_cparams("parallel"),
        name="mix_ln1",
    )(y_pre, u, y_b, gates, gates, x, d, w_glu, b_glu, w_a, w_b, w_o, ln_g, ln_b)


def _sorted_topk_rows(s, dst_ref, k):
    cur = s
    for r in range(k):
        mx = jnp.max(cur, axis=0, keepdims=True)
        dst_ref[r:r + 1, :] = mx
        if r + 1 < k:
            cur = jnp.where(cur == mx, NEG_INF, cur)


def _route_kernel(h_ref, wq_ref, keys_ref, tau_ref, s1_ref, e1_ref, s2_ref, e2_ref, a_ref, b_ref, cand_ref,
                  *, nh, nk, half, topk, ncand_rows):
    q = jnp.dot(h_ref[...].astype(BF16), wq_ref[...], preferred_element_type=F32).astype(BF16)
    tb = q.shape[0]
    for h in range(nh):
        s1 = lax.dot_general(keys_ref[2 * h], q[:, (2 * h) * half:(2 * h + 1) * half],
                             (((1,), (1,)), ((), ())), preferred_element_type=F32)
        s2 = lax.dot_general(keys_ref[2 * h + 1], q[:, (2 * h + 1) * half:(2 * h + 2) * half],
                             (((1,), (1,)), ((), ())), preferred_element_type=F32)
        _sorted_topk_rows(s1, a_ref, topk)
        _sorted_topk_rows(s2, b_ref, topk)
        a = a_ref[...]
        b = b_ref[...]
        cand_ref[...] = jnp.full((ncand_rows, tb), NEG_INF, F32)
        off = 0
        for i in range(topk):
            n_i = topk // (i + 1)
            cand_ref[off:off + n_i, :] = a[i:i + 1, :] + b[0:n_i, :]
            off += n_i
        cur = cand_ref[...]
        top = a[0:1, :] + b[0:1, :]
        z = jnp.zeros((1, tb), F32)
        tau = top
        for r in range(topk):
            tau = jnp.max(cur, axis=0, keepdims=True)
            z = z + jnp.exp(tau - top)
            if r + 1 < topk:
                cur = jnp.where(cur == tau, NEG_INF, cur)
        tau_ref[h] = tau
        s1_ref[h] = s1
        e1_ref[h] = jnp.exp(s1 - a[0:1, :])
        s2_ref[h] = s2
        e2_ref[h] = jnp.exp(s2 - b[0:1, :]) / z


def _route(h, wq, keys2, nh, tb=256):
    t, dm = h.shape
    nk, half = keys2.shape[1], keys2.shape[2]
    topk = PEER_TOPK
    ncand = sum(topk // (i + 1) for i in range(topk))
    ncand_rows = -(-ncand // SUBLANES) * SUBLANES
    assert t % tb == 0
    kern = functools.partial(_route_kernel, nh=nh, nk=nk, half=half, topk=topk, ncand_rows=ncand_rows)
    out = jax.ShapeDtypeStruct((nh, nk, t), F32)
    ospec = pl.BlockSpec((nh, nk, tb), lambda i: (0, 0, i))
    return pl.pallas_call(
        kern,
        grid=(t // tb,),
        in_specs=[pl.BlockSpec((tb, dm), lambda i: (i, 0)),
                  _const_spec(wq.shape), _const_spec(keys2.shape)],
        out_specs=[pl.BlockSpec((nh, 1, tb), lambda i: (0, 0, i))] + [ospec] * 4,
        out_shape=[jax.ShapeDtypeStruct((nh, 1, t), F32)] + [out] * 4,
        scratch_shapes=[pltpu.VMEM((topk, tb), F32), pltpu.VMEM((topk, tb), F32),
                        pltpu.VMEM((ncand_rows, tb), F32)],
        compiler_params=_cparams("parallel"),
        name="peer_route",
    )(h, wq, keys2)


def _expert_kernel(xt_ref, u_ref, vt_ref, tau_ref, s1_ref, e1_ref, s2_ref, e2_ref, o_ref, acc_ref, *, nh, nk, rows):
    e = pl.program_id(1)

    @pl.when(e == 0)
    def _():
        acc_ref[...] = jnp.zeros_like(acc_ref)

    s = jnp.dot(u_ref[...], xt_ref[...], preferred_element_type=F32)
    parts = []
    for r in range(rows):
        g = None
        for h in range(nh):
            sel = jnp.where(s1_ref[h, r:r + 1, :] + s2_ref[h] >= tau_ref[h],
                            e2_ref[h] * e1_ref[h, r:r + 1, :], 0.0)
            g = sel if g is None else g + sel
        parts.append((jax.nn.gelu(s[r * nk:(r + 1) * nk, :]) * g).astype(BF16))
    act = jnp.concatenate(parts, axis=0)
    acc_ref[...] += jnp.dot(vt_ref[...], act, preferred_element_type=F32)

    @pl.when(e == pl.num_programs(1) - 1)
    def _():
        o_ref[...] = acc_ref[...].T


def _experts(xt, u_tab, vt_tab, tau, s1, e1, s2, e2, tb=512, rows=8):
    dm, t = xt.shape
    ne = u_tab.shape[0]
    nh, nk, _ = s1.shape
    neb = rows * nk
    assert t % tb == 0 and ne % neb == 0 and ne == nk * nk
    kern = functools.partial(_expert_kernel, nh=nh, nk=nk, rows=rows)
    return pl.pallas_call(
        kern,
        grid=(t // tb, ne // neb),
        in_specs=[pl.BlockSpec((dm, tb), lambda i, e: (0, i)),
                  pl.BlockSpec((neb, dm), lambda i, e: (e, 0)),
                  pl.BlockSpec((dm, neb), lambda i, e: (0, e)),
                  pl.BlockSpec((nh, 1, tb), lambda i, e: (0, 0, i)),
                  pl.BlockSpec((nh, rows, tb), lambda i, e: (0, e, i)),
                  pl.BlockSpec((nh, rows, tb), lambda i, e: (0, e, i)),
                  pl.BlockSpec((nh, nk, tb), lambda i, e: (0, 0, i)),
                  pl.BlockSpec((nh, nk, tb), lambda i, e: (0, 0, i))],
        out_specs=pl.BlockSpec((tb, dm), lambda i, e: (i, 0)),
        out_shape=jax.ShapeDtypeStruct((t, dm), F32),
        scratch_shapes=[pltpu.VMEM((dm, tb), F32)],
        compiler_params=_cparams("parallel", "arbitrary"),
        name="peer_experts",
    )(xt, u_tab, vt_tab, tau, s1, e1, s2, e2)


def _final_kernel(h_ref, y_ref, p_ref, lg_ref, lb_ref, wg_ref, wp_ref, o_ref, *, alpha):
    h = _layer_norm(alpha * h_ref[...] + y_ref[...], lg_ref[...], lb_ref[...])
    gate = jax.nn.sigmoid(jnp.dot(h.astype(BF16), wg_ref[...], preferred_element_type=F32))
    proj = jnp.dot(p_ref[...].astype(BF16), wp_ref[...], preferred_element_type=F32)
    o_ref[...] = h + gate * proj


def _final(h, y, p, ln_g, ln_b, w_gate, w_proj, alpha, tm=512):
    t, dm = h.shape
    pd = p.shape[1]
    assert t % tm == 0
    return pl.pallas_call(
        functools.partial(_final_kernel, alpha=alpha),
        grid=(t // tm,),
        in_specs=[pl.BlockSpec((tm, dm), lambda i: (i, 0)),
                  pl.BlockSpec((tm, dm), lambda i: (i, 0)),
                  pl.BlockSpec((tm, pd), lambda i: (i, 0)),
                  _const_spec((1, dm)), _const_spec((1, dm)),
                  _const_spec((dm, dm)), _const_spec((pd, dm))],
        out_specs=pl.BlockSpec((tm, dm), lambda i: (i, 0)),
        out_shape=jax.ShapeDtypeStruct((t, dm), F32),
        compiler_params=_cparams("parallel"),
        name="ln2_ple",
    )(h, y, p, ln_g, ln_b, w_gate, w_proj)


def _layer(h, p, w_in, b_igate, b_fgate, conv_qk, mh_norm_w, a_re, a_im, log_dt, b_re, b_im, c_re, c_im, d_skip,
           w_glu, b_glu, w_up_ssm, w_up_ml, w_out, ln1_g, ln1_b, peer_wq, peer_keys, peer_u, peer_v,
           ln2_g, ln2_b, ple_w_gate, ple_w_proj, alpha):
    bsz, s, dm = h.shape
    t = bsz * s
    nh = b_igate.shape[0]
    ssm_w = d_skip.shape[0]
    ml_w = mh_norm_w.shape[0]
    x2 = h.reshape(t, dm)
    xb = x2.astype(BF16)

    o0 = ssm_w
    o1 = o0 + 2 * ml_w
    o2 = o1 + ml_w
    o3 = o2 + ml_w
    o4 = o3 + 2 * nh
    wb = w_in.astype(BF16)
    u = _matmul(xb, wb[:, :o0], F32)
    qk = _matmul(xb, wb[:, o0:o1], F32)
    v = _matmul(xb, wb[:, o1:o2], BF16)
    osig = _matmul(xb, wb[:, o2:o3], BF16, act="sigmoid")
    w_if = jnp.zeros((dm, LANES), BF16).at[:, :2 * nh].set(wb[:, o3:o4])
    gif = _matmul(xb, w_if, F32)
    gates = _matmul(xb, wb[:, o4:], BF16, act="sigmoid")

    bb, cb, lam = _s5_params(a_re, a_im, log_dt, b_re, b_im, c_re, c_im)
    y_pre = _s5_scan(u.reshape(bsz, s, ssm_w), bb, cb, lam)

    gcol = gif.reshape(bsz, s, LANES)
    grow = jnp.swapaxes(gcol[:, :, :2 * nh], 1, 2)
    bias = jnp.concatenate([b_igate, b_fgate])
    bias_c = jnp.zeros((1, LANES), F32).at[0, :2 * nh].set(bias)
    y_b = _mlstm(qk.reshape(bsz, s, 2 * ml_w), v.reshape(bsz, s, ml_w), osig.reshape(bsz, s, ml_w),
                 gcol, grow, bias_c, bias[:, None], conv_qk, mh_norm_w[None, :], nh)

    h1, h1t = _mix(y_pre.reshape(t, ssm_w), u, y_b.reshape(t, ml_w), gates, x2, d_skip[None, :],
                   w_glu.astype(BF16), b_glu[None, :], w_up_ssm.astype(BF16), w_up_ml.astype(BF16),
                   w_out.astype(BF16), ln1_g[None, :], ln1_b[None, :], alpha)

    ph, _, nk, half = peer_keys.shape
    keys2 = peer_keys.reshape(ph * 2, nk, half).astype(BF16)
    tau, s1, e1, s2, e2 = _route(h1, peer_wq.astype(BF16), keys2, ph)
    y2 = _experts(h1t, peer_u.astype(BF16), peer_v.T.astype(BF16), tau, s1, e1, s2, e2)

    out = _final(h1, y2, p.reshape(t, -1), ln2_g[None, :], ln2_b[None, :],
                 ple_w_gate.astype(BF16), ple_w_proj.astype(BF16), alpha)
    return out.reshape(bsz, s, dm)


def kernel(x, p, w_in, b_igate, b_fgate, conv_qk, mh_norm_w, ssm_a_re, ssm_a_im, ssm_log_dt, ssm_b_re, ssm_b_im,
           ssm_c_re, ssm_c_im, ssm_d, w_glu, b_glu, w_up_ssm, w_up_ml, w_out, ln1_g, ln1_b, peer_wq, peer_keys,
           peer_u, peer_v, ln2_g, ln2_b, ple_w_gate, ple_w_proj):
    depth = w_in.shape[0]
    alpha = (2 * depth) ** 0.25
    h = x
    for i in range(depth):
        h = _layer(h, p[i], w_in[i], b_igate[i], b_fgate[i], conv_qk[i], mh_norm_w[i], ssm_a_re[i], ssm_a_im[i],
                   ssm_log_dt[i], ssm_b_re[i], ssm_b_im[i], ssm_c_re[i], ssm_c_im[i], ssm_d[i], w_glu[i], b_glu[i],
                   w_up_ssm[i], w_up_ml[i], w_out[i], ln1_g[i], ln1_b[i], peer_wq[i], peer_keys[i], peer_u[i],
                   peer_v[i], ln2_g[i], ln2_b[i], ple_w_gate[i], ple_w_proj[i], alpha)
    return h
```

```python
import functools
import math

import jax
import jax.numpy as jnp
from jax import lax
from jax.experimental import pallas as pl
from jax.experimental.pallas import tpu as pltpu

F32 = jnp.float32
BF16 = jnp.bfloat16

LN_EPS = 1e-5
PEER_TOPK = 16
LANES = 128
SUBLANES = 8
VMEM_LIMIT = 56 * 1024 * 1024

ML_CHUNK = 128
S5_CHUNK = 128
S5_PITCH = S5_CHUNK + SUBLANES
NEG_INF = float("-inf")


def _cparams(*sem):
    return pltpu.CompilerParams(dimension_semantics=sem, vmem_limit_bytes=VMEM_LIMIT)


def _const_spec(shape):
    nd = len(shape)
    return pl.BlockSpec(shape, lambda *_: (0,) * nd, pipeline_mode=pl.Buffered(1))


def _mm_kernel(a_ref, w_ref, o_ref, *, act):
    acc = jnp.dot(a_ref[...], w_ref[...], preferred_element_type=F32)
    if act == "sigmoid":
        acc = jax.nn.sigmoid(acc)
    o_ref[...] = acc.astype(o_ref.dtype)


def _matmul(a, w, out_dtype, act=None, tm=1024, tn=512):
    m, k = a.shape
    n = w.shape[1]
    tm, tn = min(tm, m), min(tn, n)
    assert m % tm == 0 and n % tn == 0
    return pl.pallas_call(
        functools.partial(_mm_kernel, act=act),
        grid=(m // tm, n // tn),
        in_specs=[pl.BlockSpec((tm, k), lambda i, j: (i, 0)),
                  pl.BlockSpec((k, tn), lambda i, j: (0, j))],
        out_specs=pl.BlockSpec((tm, tn), lambda i, j: (i, j)),
        out_shape=jax.ShapeDtypeStruct((m, n), out_dtype),
        compiler_params=_cparams("parallel", "parallel"),
        name="in_proj",
    )(a, w)


def _s5_kernel(u_ref, bb_ref, cb_ref, lam_ref, y_ref, bu_ref, st_ref, *, bsz, chunk, pitch, ntile):
    c = pl.program_id(1)

    @pl.when(c == 0)
    def _():
        st_ref[...] = jnp.zeros_like(st_ref)

    bb = bb_ref[0]
    for b in range(bsz):
        bu = jnp.dot(u_ref[b].astype(BF16), bb, preferred_element_type=F32)
        for k in range(2 * ntile):
            bu_ref[k, b * pitch:b * pitch + chunk, :] = bu[:, k * LANES:(k + 1) * LANES]

    lam = lam_ref[0]
    lr = [jnp.broadcast_to(lam[k:k + 1, :], (bsz, LANES)) for k in range(ntile)]
    li = [jnp.broadcast_to(lam[ntile + k:ntile + k + 1, :], (bsz, LANES)) for k in range(ntile)]

    def step(t, carry):
        new = []
        for k in range(ntile):
            sr, si = carry[2 * k], carry[2 * k + 1]
            rows = pl.ds(t, bsz, stride=pitch)
            nr = lr[k] * sr - li[k] * si + bu_ref[k, rows, :]
            ni = lr[k] * si + li[k] * sr + bu_ref[ntile + k, rows, :]
            bu_ref[k, rows, :] = nr
            bu_ref[ntile + k, rows, :] = ni
            new += [nr, ni]
        return tuple(new)

    init = []
    for k in range(ntile):
        init += [st_ref[k], st_ref[ntile + k]]
    fin = lax.fori_loop(0, chunk, step, tuple(init), unroll=2)
    for k in range(ntile):
        st_ref[k] = fin[2 * k]
        st_ref[ntile + k] = fin[2 * k + 1]

    cb = cb_ref[0]
    for b in range(bsz):
        st = jnp.concatenate([bu_ref[k, b * pitch:b * pitch + chunk, :] for k in range(2 * ntile)], axis=1)
        y_ref[b] = jnp.dot(st.astype(BF16), cb, preferred_element_type=F32)


def _s5_scan(u, bb, cb, lam):
    bsz, s, w = u.shape
    nslab = w // LANES
    ntile = lam.shape[1] // 2
    chunk, pitch = S5_CHUNK, S5_PITCH
    assert s % chunk == 0 and bsz <= SUBLANES
    kern = functools.partial(_s5_kernel, bsz=bsz, chunk=chunk, pitch=pitch, ntile=ntile)
    return pl.pallas_call(
        kern,
        grid=(nslab, s // chunk),
        in_specs=[pl.BlockSpec((bsz, chunk, LANES), lambda j, c: (0, c, j)),
                  pl.BlockSpec((1, LANES, 2 * ntile * LANES), lambda j, c: (j, 0, 0)),
                  pl.BlockSpec((1, 2 * ntile * LANES, LANES), lambda j, c: (j, 0, 0)),
                  pl.BlockSpec((1, 2 * ntile, LANES), lambda j, c: (j, 0, 0))],
        out_specs=pl.BlockSpec((bsz, chunk, LANES), lambda j, c: (0, c, j)),
        out_shape=jax.ShapeDtypeStruct((bsz, s, w), F32),
        scratch_shapes=[pltpu.VMEM((2 * ntile, bsz * pitch, LANES), F32),
                        pltpu.VMEM((2 * ntile, bsz, LANES), F32)],
        compiler_params=_cparams("parallel", "arbitrary"),
        name="s5_scan",
    )(u, bb, cb, lam)


def _s5_params(a_re, a_im, log_dt, b_re, b_im, c_re, c_im):
    g, p = a_re.shape
    gc = b_re.shape[2]
    gps = LANES // gc
    nslab = g // gps
    dt = jnp.exp(log_dt)[:, None]
    mag = jnp.exp(dt * a_re)
    lam_r = mag * jnp.cos(dt * a_im)
    lam_i = mag * jnp.sin(dt * a_im)
    den = a_re * a_re + a_im * a_im
    zr = ((lam_r - 1.0) * a_re + lam_i * a_im) / den
    zi = (lam_i * a_re - (lam_r - 1.0) * a_im) / den
    bbar_r = zr[..., None] * b_re - zi[..., None] * b_im
    bbar_i = zr[..., None] * b_im + zi[..., None] * b_re
    eye = jnp.eye(gps, dtype=F32)

    def in_slab(bm):
        bm = bm.reshape(nslab, gps, p, gc)
        return jnp.einsum("jgpc,gh->jgchp", bm, eye).reshape(nslab, gps * gc, gps * p)

    def out_slab(cm):
        cm = cm.reshape(nslab, gps, gc, p)
        return jnp.einsum("jgcp,gh->jhpgc", cm, eye).reshape(nslab, gps * p, gps * gc)

    bb = jnp.concatenate([in_slab(bbar_r), in_slab(bbar_i)], axis=2).astype(BF16)
    cb = jnp.concatenate([out_slab(c_re), -out_slab(c_im)], axis=1).astype(BF16)
    ntile = gps * p // LANES
    lam = jnp.concatenate([lam_r.reshape(nslab, ntile, LANES), lam_i.reshape(nslab, ntile, LANES)], axis=1)
    return bb, cb, lam


def _log_sigmoid(x):
    return jnp.minimum(x, 0.0) - jnp.log1p(jnp.exp(-jnp.abs(x)))


def _mlstm_kernel(qk_ref, v_ref, o_ref, gc_ref, gr_ref, bc_ref, br_ref, cw_ref, nw_ref, y_ref,
                  buf_ref, c_ref, n_ref, m_ref, *, nh, dh, chunk, kconv):
    ci = pl.program_id(1)
    width = nh * dh
    halo = SUBLANES

    @pl.when(ci == 0)
    def _():
        buf_ref[0:halo, :] = jnp.zeros((halo, 2 * width), F32)
        c_ref[...] = jnp.zeros_like(c_ref)
        n_ref[...] = jnp.zeros_like(n_ref)
        m_ref[...] = jnp.zeros_like(m_ref)

    buf_ref[halo:halo + chunk, :] = qk_ref[0]
    conv = cw_ref[0:1, :] * buf_ref[halo - (kconv - 1):halo - (kconv - 1) + chunk, :]
    for j in range(1, kconv):
        off = halo - (kconv - 1) + j
        conv = conv + cw_ref[j:j + 1, :] * buf_ref[off:off + chunk, :]
    buf_ref[0:halo, :] = buf_ref[chunk:chunk + halo, :]
    qk = conv * jax.nn.sigmoid(conv)

    gcol = gc_ref[0] + bc_ref[...]
    col_id = lax.broadcasted_iota(jnp.int32, gcol.shape, 1)
    lcol = jnp.where(col_id >= nh, _log_sigmoid(gcol), gcol)
    grow = gr_ref[0] + br_ref[...]
    row_id = lax.broadcasted_iota(jnp.int32, grow.shape, 0)
    lrow = jnp.where(row_id >= nh, _log_sigmoid(grow), grow)
    tt = lax.broadcasted_iota(jnp.int32, (chunk, chunk), 0)
    ss = lax.broadcasted_iota(jnp.int32, (chunk, chunk), 1)
    causal = ss <= tt
    tril = causal.astype(F32)
    triu = (tt <= ss).astype(F32)
    bcol = jnp.dot(tril, lcol, preferred_element_type=F32, precision=lax.Precision.HIGHEST)
    brow = jnp.dot(lrow, triu, preferred_element_type=F32, precision=lax.Precision.HIGHEST)

    inv_sqrt = 1.0 / math.sqrt(dh)
    for h in range(nh):
        hs = slice(h * dh, (h + 1) * dh)
        qf = qk[:, hs]
        q = qf.astype(BF16)
        kf = qk[:, width + h * dh:width + (h + 1) * dh] * inv_sqrt
        k = kf.astype(BF16)
        v = v_ref[0, :, hs]
        b_c = bcol[:, nh + h:nh + h + 1]
        i_c = lcol[:, h:h + 1]
        b_r = brow[nh + h:nh + h + 1, :]
        i_r = lrow[h:h + 1, :]
        g_tot = b_c[chunk - 1:chunk, :]
        m_prev = m_ref[h][:, 0:1]
        c_prev = c_ref[h]
        n_prev = n_ref[h]

        dmat = jnp.where(causal, b_c - b_r + i_r, NEG_INF)
        inter_log = b_c + m_prev
        m_row = jnp.maximum(inter_log, jnp.max(dmat, axis=1, keepdims=True))
        pw = jnp.exp(dmat - m_row)
        sc = lax.dot_general(q, k, (((1,), (1,)), ((), ())), preferred_element_type=F32) * pw
        inter_scale = jnp.exp(inter_log - m_row)
        num = (jnp.dot(sc.astype(BF16), v, preferred_element_type=F32)
               + inter_scale * jnp.dot(q, c_prev.astype(BF16), preferred_element_type=F32))
        qn = jnp.sum(qf * n_prev, axis=1, keepdims=True)
        den = jnp.sum(sc, axis=1, keepdims=True) + inter_scale * qn
        hh = num / jnp.maximum(jnp.abs(den), jnp.exp(-m_row))
        mu = jnp.mean(hh, axis=1, keepdims=True)
        var = jnp.mean(jnp.square(hh - mu), axis=1, keepdims=True)
        hn = (hh - mu) * lax.rsqrt(var + LN_EPS) * nw_ref[:, hs]
        y_ref[0, :, hs] = (o_ref[0, :, hs].astype(F32) * hn).astype(y_ref.dtype)

        wlog = g_tot - b_c + i_c
        m_loc = jnp.max(wlog, axis=0, keepdims=True)
        kw = kf * jnp.exp(wlog - m_loc)
        c_chunk = lax.dot_general(kw.astype(BF16), v, (((0,), (0,)), ((), ())), preferred_element_type=F32)
        n_chunk = jnp.sum(kw, axis=0, keepdims=True)
        m_new = jnp.maximum(g_tot + m_prev, m_loc)
        a = jnp.exp(g_tot + m_prev - m_new)
        bb = jnp.exp(m_loc - m_new)
        c_ref[h] = a * c_prev + bb * c_chunk
        n_ref[h] = a * n_prev + bb * n_chunk
        m_ref[h] = jnp.broadcast_to(m_new, (1, LANES))


def _mlstm(qk, v, osig, gcol, grow, bias_c, bias_r, conv_w, norm_w, nh):
    bsz, s, w2 = qk.shape
    width = w2 // 2
    dh = width // nh
    chunk = ML_CHUNK
    kconv = conv_w.shape[0]
    assert s % chunk == 0 and kconv - 1 <= SUBLANES
    kern = functools.partial(_mlstm_kernel, nh=nh, dh=dh, chunk=chunk, kconv=kconv)
    return pl.pallas_call(
        kern,
        grid=(bsz, s // chunk),
        in_specs=[pl.BlockSpec((1, chunk, w2), lambda b, c: (b, c, 0)),
                  pl.BlockSpec((1, chunk, width), lambda b, c: (b, c, 0)),
                  pl.BlockSpec((1, chunk, width), lambda b, c: (b, c, 0)),
                  pl.BlockSpec((1, chunk, LANES), lambda b, c: (b, c, 0)),
                  pl.BlockSpec((1, 2 * nh, chunk), lambda b, c: (b, 0, c)),
                  _const_spec((1, LANES)),
                  _const_spec((2 * nh, 1)),
                  _const_spec((kconv, w2)),
                  _const_spec((1, width))],
        out_specs=pl.BlockSpec((1, chunk, width), lambda b, c: (b, c, 0)),
        out_shape=jax.ShapeDtypeStruct((bsz, s, width), BF16),
        scratch_shapes=[pltpu.VMEM((SUBLANES + chunk, w2), F32),
                        pltpu.VMEM((nh, dh, dh), F32),
                        pltpu.VMEM((nh, 1, dh), F32),
                        pltpu.VMEM((nh, 1, LANES), F32)],
        compiler_params=_cparams("parallel", "arbitrary"),
        name="mlstm",
    )(qk, v, osig, gcol, grow, bias_c, bias_r, conv_w, norm_w)


def _layer_norm(x, g, b):
    mu = jnp.mean(x, axis=-1, keepdims=True)
    var = jnp.mean(jnp.square(x - mu), axis=-1, keepdims=True)
    return (x - mu) * lax.rsqrt(var + LN_EPS) * g + b


def _mix_kernel(yp_ref, u_ref, yb_ref, ga_ref, gb_ref, x_ref, d_ref, wg_ref, bg_ref, wa_ref, wb_ref, wo_ref,
                lg_ref, lb_ref, h_ref, ht_ref, *, alpha):
    y = jax.nn.gelu(yp_ref[...] + d_ref[...] * u_ref[...])
    gate = jax.nn.sigmoid(jnp.dot(y.astype(BF16), wg_ref[...], preferred_element_type=F32) + bg_ref[...])
    ya = (y * gate).astype(BF16)
    merged = (ga_ref[...].astype(F32) * jnp.dot(ya, wa_ref[...], preferred_element_type=F32)
              + gb_ref[...].astype(F32) * jnp.dot(yb_ref[...], wb_ref[...], preferred_element_type=F32))
    mix = jnp.dot(merged.astype(BF16), wo_ref[...], preferred_element_type=F32)
    h = _layer_norm(alpha * x_ref[...] + mix, lg_ref[...], lb_ref[...])
    h_ref[...] = h
    ht_ref[...] = h.T.astype(BF16)


def _mix(y_pre, u, y_b, gates, x, d, w_glu, b_glu, w_a, w_b, w_o, ln_g, ln_b, alpha, tm=256):
    t, dm = x.shape
    w = u.shape[1]
    assert t % tm == 0
    row = lambda width: pl.BlockSpec((tm, width), lambda i: (i, 0))
    return pl.pallas_call(
        functools.partial(_mix_kernel, alpha=alpha),
        grid=(t // tm,),
        in_specs=[row(w), row(w), row(w),
                  pl.BlockSpec((tm, dm), lambda i: (i, 0)),
                  pl.BlockSpec((tm, dm), lambda i: (i, 1)),
                  row(dm),
                  _const_spec((1, w)), _const_spec((w, w)), _const_spec((1, w)),
                  _const_spec((w, dm)), _const_spec((w, dm)), _const_spec((dm, dm)),
                  _const_spec((1, dm)), _const_spec((1, dm))],
        out_specs=[pl.BlockSpec((tm, dm), lambda i: (i, 0)),
                   pl.BlockSpec((dm, tm), lambda i: (0, i))],
        out_shape=[jax.ShapeDtypeStruct((t, dm), F32), jax.ShapeDtypeStruct((dm, t), BF16)],
        compiler_params=_cparams("parallel"),
        name="mix_ln1",
    )(y_pre, u, y_b, gates, gates, x, d, w_glu, b_glu, w_a, w_b, w_o, ln_g, ln_b)


def _sorted_topk_rows(s, dst_ref, k):
    cur = s
    for r in range(k):
        mx = jnp.max(cur, axis=0, keepdims=True)
        dst_ref[r:r + 1, :] = mx
        if r + 1 < k:
            cur = jnp.where(cur == mx, NEG_INF, cur)


def _route_kernel(h_ref, wq_ref, keys_ref, thr_ref, e1_ref, s2_ref, e2_ref, a_ref, b_ref, cand_ref,
                  *, nh, nk, half, topk, ncand_rows):
    q = jnp.dot(h_ref[...].astype(BF16), wq_ref[...], preferred_element_type=F32).astype(BF16)
    tb = q.shape[0]
    nrank = topk + 1
    for h in range(nh):
        s1 = lax.dot_general(keys_ref[2 * h], q[:, (2 * h) * half:(2 * h + 1) * half],
                             (((1,), (1,)), ((), ())), preferred_element_type=F32)
        s2 = lax.dot_general(keys_ref[2 * h + 1], q[:, (2 * h + 1) * half:(2 * h + 2) * half],
                             (((1,), (1,)), ((), ())), preferred_element_type=F32)
        _sorted_topk_rows(s1, a_ref, nrank)
        _sorted_topk_rows(s2, b_ref, nrank)
        a = a_ref[0:nrank, :]
        b = b_ref[0:nrank, :]
        cand_ref[...] = jnp.full((ncand_rows, tb), NEG_INF, F32)
        off = 0
        for i in range(nrank):
            n_i = nrank // (i + 1)
            cand_ref[off:off + n_i, :] = a[i:i + 1, :] + b[0:n_i, :]
            off += n_i
        cur = cand_ref[...]
        top = a[0:1, :] + b[0:1, :]
        z = jnp.zeros((1, tb), F32)
        kth = top
        for r in range(topk):
            kth = jnp.max(cur, axis=0, keepdims=True)
            z = z + jnp.exp(kth - top)
            cur = jnp.where(cur == kth, NEG_INF, cur)
        nxt = jnp.max(cur, axis=0, keepdims=True)
        tau = 0.5 * (kth + nxt)
        thr_ref[h] = tau - s1
        e1_ref[h] = jnp.exp(s1 - a[0:1, :])
        s2_ref[h] = s2
        e2_ref[h] = jnp.exp(s2 - b[0:1, :]) / z


def _route(h, wq, keys2, nh, tb=256):
    t, dm = h.shape
    nk, half = keys2.shape[1], keys2.shape[2]
    topk = PEER_TOPK
    nrank = topk + 1
    ncand = sum(nrank // (i + 1) for i in range(nrank))
    ncand_rows = -(-ncand // SUBLANES) * SUBLANES
    rank_rows = -(-nrank // SUBLANES) * SUBLANES
    assert t % tb == 0 and nk > nrank
    kern = functools.partial(_route_kernel, nh=nh, nk=nk, half=half, topk=topk, ncand_rows=ncand_rows)
    out = jax.ShapeDtypeStruct((nh, nk, t), F32)
    ospec = pl.BlockSpec((nh, nk, tb), lambda i: (0, 0, i))
    return pl.pallas_call(
        kern,
        grid=(t // tb,),
        in_specs=[pl.BlockSpec((tb, dm), lambda i: (i, 0)),
                  _const_spec(wq.shape), _const_spec(keys2.shape)],
        out_specs=[ospec] * 4,
        out_shape=[out] * 4,
        scratch_shapes=[pltpu.VMEM((rank_rows, tb), F32), pltpu.VMEM((rank_rows, tb), F32),
                        pltpu.VMEM((ncand_rows, tb), F32)],
        compiler_params=_cparams("parallel"),
        name="peer_route",
    )(h, wq, keys2)


_GELU_K1 = -2.0 * math.sqrt(2.0 / math.pi) * math.log2(math.e)
_GELU_K2 = _GELU_K1 * 0.044715


def _gelu_tanh(x):
    return x / (1.0 + jnp.exp2(x * (_GELU_K1 + _GELU_K2 * (x * x))))


def _expert_kernel(xt_ref, u_ref, vt_ref, thr_ref, e1_ref, s2_ref, e2_ref, o_ref, sa_ref, sb_ref,
                   *, nh, nk, rows, sub):
    e = pl.program_id(1)

    @pl.when(e == 0)
    def _():
        o_ref[...] = jnp.zeros_like(o_ref)
        sa_ref[...] = jnp.zeros_like(sa_ref)

    def step(rd_ref, wr_ref):
        for q in range(xt_ref.shape[1] // sub):
            cs = slice(q * sub, (q + 1) * sub)
            wr_ref[:, cs] = jnp.dot(u_ref[...], xt_ref[:, cs], preferred_element_type=F32)
            parts = []
            for r in range(rows):
                g = None
                for h in range(nh):
                    sel = jnp.where(s2_ref[h, :, cs] >= thr_ref[h, r:r + 1, cs],
                                    e2_ref[h, :, cs] * e1_ref[h, r:r + 1, cs], 0.0)
                    g = sel if g is None else g + sel
                parts.append((_gelu_tanh(rd_ref[r * nk:(r + 1) * nk, cs]) * g).astype(BF16))
            act = jnp.concatenate(parts, axis=0)
            o_ref[:, cs] += jnp.dot(vt_ref[...], act, preferred_element_type=F32)

    @pl.when(lax.rem(e, 2) == 0)
    def _():
        step(sa_ref, sb_ref)

    @pl.when(lax.rem(e, 2) == 1)
    def _():
        step(sb_ref, sa_ref)


def _experts(xt, u_tab, vt_tab, thr, e1, s2, e2, tb=1024, rows=8, sub=256):
    dm, t = xt.shape
    ne = u_tab.shape[0]
    nh, nk, _ = thr.shape
    neb = rows * nk
    tb = min(tb, t)
    assert t % tb == 0 and tb % sub == 0 and ne % neb == 0 and ne == nk * nk
    nblk = ne // neb
    kern = functools.partial(_expert_kernel, nh=nh, nk=nk, rows=rows, sub=sub)
    once = pl.Buffered(1)
    score_blk = lambda e: jnp.minimum(e, nblk - 1)
    gate_blk = lambda e: jnp.maximum(e - 1, 0)
    return pl.pallas_call(
        kern,
        grid=(t // tb, nblk + 1),
        in_specs=[pl.BlockSpec((dm, tb), lambda i, e: (0, i), pipeline_mode=once),
                  pl.BlockSpec((neb, dm), lambda i, e: (score_blk(e), 0)),
                  pl.BlockSpec((dm, neb), lambda i, e: (0, gate_blk(e))),
                  pl.BlockSpec((nh, rows, tb), lambda i, e: (0, gate_blk(e), i)),
                  pl.BlockSpec((nh, rows, tb), lambda i, e: (0, gate_blk(e), i)),
                  pl.BlockSpec((nh, nk, tb), lambda i, e: (0, 0, i), pipeline_mode=once),
                  pl.BlockSpec((nh, nk, tb), lambda i, e: (0, 0, i), pipeline_mode=once)],
        out_specs=pl.BlockSpec((dm, tb), lambda i, e: (0, i), pipeline_mode=once),
        out_shape=jax.ShapeDtypeStruct((dm, t), F32),
        scratch_shapes=[pltpu.VMEM((neb, tb), F32), pltpu.VMEM((neb, tb), F32)],
        compiler_params=_cparams("parallel", "arbitrary"),
        name="peer_experts",
    )(xt, u_tab, vt_tab, thr, e1, s2, e2)


def _final_kernel(h_ref, yt_ref, p_ref, lg_ref, lb_ref, wg_ref, wp_ref, o_ref, *, alpha):
    h = _layer_norm(alpha * h_ref[...] + yt_ref[...].T, lg_ref[...], lb_ref[...])
    gate = jax.nn.sigmoid(jnp.dot(h.astype(BF16), wg_ref[...], preferred_element_type=F32))
    proj = jnp.dot(p_ref[...].astype(BF16), wp_ref[...], preferred_element_type=F32)
    o_ref[...] = h + gate * proj


def _final(h, yt, p, ln_g, ln_b, w_gate, w_proj, alpha, tm=512):
    t, dm = h.shape
    pd = p.shape[1]
    assert t % tm == 0
    return pl.pallas_call(
        functools.partial(_final_kernel, alpha=alpha),
        grid=(t // tm,),
        in_specs=[pl.BlockSpec((tm, dm), lambda i: (i, 0)),
                  pl.BlockSpec((dm, tm), lambda i: (0, i)),
                  pl.BlockSpec((tm, pd), lambda i: (i, 0)),
                  _const_spec((1, dm)), _const_spec((1, dm)),
                  _const_spec((dm, dm)), _const_spec((pd, dm))],
        out_specs=pl.BlockSpec((tm, dm), lambda i: (i, 0)),
        out_shape=jax.ShapeDtypeStruct((t, dm), F32),
        compiler_params=_cparams("parallel"),
        name="ln2_ple",
    )(h, yt, p, ln_g, ln_b, w_gate, w_proj)


def _layer(h, p, w_in, b_igate, b_fgate, conv_qk, mh_norm_w, a_re, a_im, log_dt, b_re, b_im, c_re, c_im, d_skip,
           w_glu, b_glu, w_up_ssm, w_up_ml, w_out, ln1_g, ln1_b, peer_wq, peer_keys, peer_u, peer_v,
           ln2_g, ln2_b, ple_w_gate, ple_w_proj, alpha):
    bsz, s, dm = h.shape
    t = bsz * s
    nh = b_igate.shape[0]
    ssm_w = d_skip.shape[0]
    ml_w = mh_norm_w.shape[0]
    x2 = h.reshape(t, dm)
    xb = x2.astype(BF16)

    o0 = ssm_w
    o1 = o0 + 2 * ml_w
    o2 = o1 + ml_w
    o3 = o2 + ml_w
    o4 = o3 + 2 * nh
    wb = w_in.astype(BF16)
    u = _matmul(xb, wb[:, :o0], F32)
    qk = _matmul(xb, wb[:, o0:o1], F32)
    v = _matmul(xb, wb[:, o1:o2], BF16)
    osig = _matmul(xb, wb[:, o2:o3], BF16, act="sigmoid")
    w_if = jnp.zeros((dm, LANES), BF16).at[:, :2 * nh].set(wb[:, o3:o4])
    gif = _matmul(xb, w_if, F32)
    gates = _matmul(xb, wb[:, o4:], BF16, act="sigmoid")

    bb, cb, lam = _s5_params(a_re, a_im, log_dt, b_re, b_im, c_re, c_im)
    y_pre = _s5_scan(u.reshape(bsz, s, ssm_w), bb, cb, lam)

    gcol = gif.reshape(bsz, s, LANES)
    grow = jnp.swapaxes(gcol[:, :, :2 * nh], 1, 2)
    bias = jnp.concatenate([b_igate, b_fgate])
    bias_c = jnp.zeros((1, LANES), F32).at[0, :2 * nh].set(bias)
    y_b = _mlstm(qk.reshape(bsz, s, 2 * ml_w), v.reshape(bsz, s, ml_w), osig.reshape(bsz, s, ml_w),
                 gcol, grow, bias_c, bias[:, None], conv_qk, mh_norm_w[None, :], nh)

    h1, h1t = _mix(y_pre.reshape(t, ssm_w), u, y_b.reshape(t, ml_w), gates, x2, d_skip[None, :],
                   w_glu.astype(BF16), b_glu[None, :], w_up_ssm.astype(BF16), w_up_ml.astype(BF16),
                   w_out.astype(BF16), ln1_g[None, :], ln1_b[None, :], alpha)

    ph, _, nk, half = peer_keys.shape
    keys2 = peer_keys.reshape(ph * 2, nk, half).astype(BF16)
    thr, e1, s2, e2 = _route(h1, peer_wq.astype(BF16), keys2, ph)
    y2t = _experts(h1t, peer_u.astype(BF16), peer_v.T.astype(BF16), thr, e1, s2, e2)

    out = _final(h1, y2t, p.reshape(t, -1), ln2_g[None, :], ln2_b[None, :],
                 ple_w_gate.astype(BF16), ple_w_proj.astype(BF16), alpha)
    return out.reshape(bsz, s, dm)


def kernel(x, p, w_in, b_igate, b_fgate, conv_qk, mh_norm_w, ssm_a_re, ssm_a_im, ssm_log_dt, ssm_b_re, ssm_b_im,
           ssm_c_re, ssm_c_im, ssm_d, w_glu, b_glu, w_up_ssm, w_up_ml, w_out, ln1_g, ln1_b, peer_wq, peer_keys,
           peer_u, peer_v, ln2_g, ln2_b, ple_w_gate, ple_w_proj):
    depth = w_in.shape[0]
    alpha = (2 * depth) ** 0.25
    h = x
    for i in range(depth):
        h = _layer(h, p[i], w_in[i], b_igate[i], b_fgate[i], conv_qk[i], mh_norm_w[i], ssm_a_re[i], ssm_a_im[i],
                   ssm_log_dt[i], ssm_b_re[i], ssm_b_im[i], ssm_c_re[i], ssm_c_im[i], ssm_d[i], w_glu[i], b_glu[i],
                   w_up_ssm[i], w_up_ml[i], w_out[i], ln1_g[i], ln1_b[i], peer_wq[i], peer_keys[i], peer_u[i],
                   peer_v[i], ln2_g[i], ln2_b[i], ple_w_gate[i], ple_w_proj[i], alpha)
    return h
```

```python
import functools
import math

import jax
import jax.numpy as jnp
from jax import lax
from jax.experimental import pallas as pl
from jax.experimental.pallas import tpu as pltpu

F32 = jnp.float32
BF16 = jnp.bfloat16

LN_EPS = 1e-5
PEER_TOPK = 16
LANES = 128
SUBLANES = 8
VMEM_LIMIT = 56 * 1024 * 1024

ML_CHUNK = 128
ML_BATCH = 4
S5_CHUNK = 128
S5_PITCH = S5_CHUNK + SUBLANES
NEG_INF = float("-inf")


def _cparams(*sem):
    return pltpu.CompilerParams(dimension_semantics=sem, vmem_limit_bytes=VMEM_LIMIT)


def _const_spec(shape):
    nd = len(shape)
    return pl.BlockSpec(shape, lambda *_: (0,) * nd, pipeline_mode=pl.Buffered(1))


def _mm_kernel(a_ref, w_ref, o_ref):
    o_ref[...] = jnp.dot(a_ref[...], w_ref[...], preferred_element_type=F32).astype(o_ref.dtype)


def _matmul(a, w, out_dtype, tm=1024, tn=512):
    m, k = a.shape
    n = w.shape[1]
    tm, tn = min(tm, m), min(tn, n)
    assert m % tm == 0 and n % tn == 0
    return pl.pallas_call(
        _mm_kernel,
        grid=(m // tm, n // tn),
        in_specs=[pl.BlockSpec((tm, k), lambda i, j: (i, 0)),
                  pl.BlockSpec((k, tn), lambda i, j: (0, j))],
        out_specs=pl.BlockSpec((tm, tn), lambda i, j: (i, j)),
        out_shape=jax.ShapeDtypeStruct((m, n), out_dtype),
        compiler_params=_cparams("parallel", "parallel"),
        name="gate_proj",
    )(a, w)


def _in_proj_kernel(a_ref, w_ref, *o_refs, segs):
    j = pl.program_id(1)
    acc = jnp.dot(a_ref[...], w_ref[...], preferred_element_type=F32)
    for o_ref, (lo, hi, sig) in zip(o_refs, segs):
        @pl.when((j >= lo) & (j < hi))
        def _(o_ref=o_ref, sig=sig):
            o_ref[...] = (jax.nn.sigmoid(acc) if sig else acc).astype(o_ref.dtype)


def _in_proj(a, w, outs, tm=1024, tn=1024):
    m, k = a.shape
    n = w.shape[1]
    tm = min(tm, m)
    assert m % tm == 0 and n % tn == 0 and sum(o[0] for o in outs) == n and all(o[0] % tn == 0 for o in outs)
    segs, specs, shapes, lo = [], [], [], 0
    for width, dtype, sig in outs:
        nblk = width // tn
        segs.append((lo, lo + nblk, sig))
        specs.append(pl.BlockSpec((tm, tn), lambda i, j, lo=lo, nblk=nblk: (i, jnp.clip(j - lo, 0, nblk - 1))))
        shapes.append(jax.ShapeDtypeStruct((m, width), dtype))
        lo += nblk
    return pl.pallas_call(
        functools.partial(_in_proj_kernel, segs=tuple(segs)),
        grid=(m // tm, n // tn),
        in_specs=[pl.BlockSpec((tm, k), lambda i, j: (i, 0)),
                  pl.BlockSpec((k, tn), lambda i, j: (0, j))],
        out_specs=specs,
        out_shape=shapes,
        compiler_params=_cparams("parallel", "arbitrary"),
        name="in_proj",
    )(a, w)


def _s5_kernel(u_ref, bb_ref, cb_ref, lam_ref, y_ref, bu_ref, st_ref, *, bsz, chunk, pitch, ntile):
    c = pl.program_id(1)

    @pl.when(c == 0)
    def _():
        st_ref[...] = jnp.zeros_like(st_ref)

    bb = bb_ref[0]
    for b in range(bsz):
        bu = jnp.dot(u_ref[b].astype(BF16), bb, preferred_element_type=F32)
        for k in range(2 * ntile):
            bu_ref[k, b * pitch:b * pitch + chunk, :] = bu[:, k * LANES:(k + 1) * LANES]

    lam = lam_ref[0]
    lr = [jnp.broadcast_to(lam[k:k + 1, :], (bsz, LANES)) for k in range(ntile)]
    li = [jnp.broadcast_to(lam[ntile + k:ntile + k + 1, :], (bsz, LANES)) for k in range(ntile)]

    def step(t, carry):
        new = []
        for k in range(ntile):
            sr, si = carry[2 * k], carry[2 * k + 1]
            rows = pl.ds(t, bsz, stride=pitch)
            nr = lr[k] * sr - li[k] * si + bu_ref[k, rows, :]
            ni = lr[k] * si + li[k] * sr + bu_ref[ntile + k, rows, :]
            bu_ref[k, rows, :] = nr
            bu_ref[ntile + k, rows, :] = ni
            new += [nr, ni]
        return tuple(new)

    init = []
    for k in range(ntile):
        init += [st_ref[k], st_ref[ntile + k]]
    fin = lax.fori_loop(0, chunk, step, tuple(init), unroll=2)
    for k in range(ntile):
        st_ref[k] = fin[2 * k]
        st_ref[ntile + k] = fin[2 * k + 1]

    cb = cb_ref[0]
    for b in range(bsz):
        st = jnp.concatenate([bu_ref[k, b * pitch:b * pitch + chunk, :] for k in range(2 * ntile)], axis=1)
        y_ref[b] = jnp.dot(st.astype(BF16), cb, preferred_element_type=F32)


def _s5_scan(u, bb, cb, lam):
    bsz, s, w = u.shape
    nslab = w // LANES
    ntile = lam.shape[1] // 2
    chunk, pitch = S5_CHUNK, S5_PITCH
    assert s % chunk == 0 and bsz <= SUBLANES
    kern = functools.partial(_s5_kernel, bsz=bsz, chunk=chunk, pitch=pitch, ntile=ntile)
    return pl.pallas_call(
        kern,
        grid=(nslab, s // chunk),
        in_specs=[pl.BlockSpec((bsz, chunk, LANES), lambda j, c: (0, c, j)),
                  pl.BlockSpec((1, LANES, 2 * ntile * LANES), lambda j, c: (j, 0, 0)),
                  pl.BlockSpec((1, 2 * ntile * LANES, LANES), lambda j, c: (j, 0, 0)),
                  pl.BlockSpec((1, 2 * ntile, LANES), lambda j, c: (j, 0, 0))],
        out_specs=pl.BlockSpec((bsz, chunk, LANES), lambda j, c: (0, c, j)),
        out_shape=jax.ShapeDtypeStruct((bsz, s, w), F32),
        scratch_shapes=[pltpu.VMEM((2 * ntile, bsz * pitch, LANES), F32),
                        pltpu.VMEM((2 * ntile, bsz, LANES), F32)],
        compiler_params=_cparams("parallel", "arbitrary"),
        name="s5_scan",
    )(u, bb, cb, lam)


def _s5_params(a_re, a_im, log_dt, b_re, b_im, c_re, c_im):
    g, p = a_re.shape
    gc = b_re.shape[2]
    gps = LANES // gc
    nslab = g // gps
    dt = jnp.exp(log_dt)[:, None]
    mag = jnp.exp(dt * a_re)
    lam_r = mag * jnp.cos(dt * a_im)
    lam_i = mag * jnp.sin(dt * a_im)
    den = a_re * a_re + a_im * a_im
    zr = ((lam_r - 1.0) * a_re + lam_i * a_im) / den
    zi = (lam_i * a_re - (lam_r - 1.0) * a_im) / den
    bbar_r = zr[..., None] * b_re - zi[..., None] * b_im
    bbar_i = zr[..., None] * b_im + zi[..., None] * b_re
    eye = jnp.eye(gps, dtype=F32)

    def in_slab(bm):
        bm = bm.reshape(nslab, gps, p, gc)
        return jnp.einsum("jgpc,gh->jgchp", bm, eye).reshape(nslab, gps * gc, gps * p)

    def out_slab(cm):
        cm = cm.reshape(nslab, gps, gc, p)
        return jnp.einsum("jgcp,gh->jhpgc", cm, eye).reshape(nslab, gps * p, gps * gc)

    bb = jnp.concatenate([in_slab(bbar_r), in_slab(bbar_i)], axis=2).astype(BF16)
    cb = jnp.concatenate([out_slab(c_re), -out_slab(c_im)], axis=1).astype(BF16)
    ntile = gps * p // LANES
    lam = jnp.concatenate([lam_r.reshape(nslab, ntile, LANES), lam_i.reshape(nslab, ntile, LANES)], axis=1)
    return bb, cb, lam


def _log_sigmoid(x):
    return jnp.minimum(x, 0.0) - jnp.log1p(jnp.exp(-jnp.abs(x)))


def _mlstm_kernel(qk_ref, v_ref, o_ref, gc_ref, gr_ref, bc_ref, br_ref, cw_ref, nw_ref, y_ref,
                  buf_ref, qs_ref, c_ref, n_ref, m_ref, *, nb, nh, dh, chunk, kconv):
    ci = pl.program_id(1)
    width = nh * dh
    halo = SUBLANES

    @pl.when(ci == 0)
    def _():
        buf_ref[:, 0:halo, :] = jnp.zeros((nb, halo, 2 * width), F32)
        c_ref[...] = jnp.zeros_like(c_ref)
        n_ref[...] = jnp.zeros_like(n_ref)
        m_ref[...] = jnp.zeros_like(m_ref)

    tt = lax.broadcasted_iota(jnp.int32, (chunk, chunk), 0)
    ss = lax.broadcasted_iota(jnp.int32, (chunk, chunk), 1)
    causal = ss <= tt
    tril = causal.astype(F32)
    triu = (tt <= ss).astype(F32)

    gates = []
    for bi in range(nb):
        buf_ref[bi, halo:halo + chunk, :] = qk_ref[bi]
        base = halo - (kconv - 1)
        conv = cw_ref[0:1, :] * buf_ref[bi, base:base + chunk, :]
        for j in range(1, kconv):
            conv = conv + cw_ref[j:j + 1, :] * buf_ref[bi, base + j:base + j + chunk, :]
        buf_ref[bi, 0:halo, :] = buf_ref[bi, chunk:chunk + halo, :]
        qs_ref[bi] = conv * jax.nn.sigmoid(conv)

        gcol = gc_ref[bi] + bc_ref[...]
        col_id = lax.broadcasted_iota(jnp.int32, gcol.shape, 1)
        lcol = jnp.where(col_id >= nh, _log_sigmoid(gcol), gcol)
        grow = gr_ref[bi] + br_ref[...]
        row_id = lax.broadcasted_iota(jnp.int32, grow.shape, 0)
        lrow = jnp.where(row_id >= nh, _log_sigmoid(grow), grow)
        bcol = jnp.dot(tril, lcol, preferred_element_type=F32, precision=lax.Precision.HIGHEST)
        brow = jnp.dot(lrow, triu, preferred_element_type=F32, precision=lax.Precision.HIGHEST)
        gates.append((lcol, lrow, bcol, brow))

    inv_sqrt = 1.0 / math.sqrt(dh)
    for h in range(nh):
        hs = slice(h * dh, (h + 1) * dh)
        ks = slice(width + h * dh, width + (h + 1) * dh)
        bs = range(nb)
        st = [bi * nh + h for bi in bs]
        qf = [qs_ref[bi, :, hs] for bi in bs]
        q = [x.astype(BF16) for x in qf]
        kf = [qs_ref[bi, :, ks] * inv_sqrt for bi in bs]
        k = [x.astype(BF16) for x in kf]
        v = [v_ref[bi, :, hs] for bi in bs]
        b_c = [gates[bi][2][:, nh + h:nh + h + 1] for bi in bs]
        i_c = [gates[bi][0][:, h:h + 1] for bi in bs]
        b_r = [gates[bi][3][nh + h:nh + h + 1, :] for bi in bs]
        i_r = [gates[bi][1][h:h + 1, :] for bi in bs]
        g_tot = [x[chunk - 1:chunk, :] for x in b_c]
        m_prev = [m_ref[s_][:, 0:1] for s_ in st]
        c_prev = [c_ref[s_] for s_ in st]
        n_prev = [n_ref[s_] for s_ in st]

        qk_t = [lax.dot_general(q[bi], k[bi], (((1,), (1,)), ((), ())), preferred_element_type=F32) for bi in bs]
        q_c = [jnp.dot(q[bi], c_prev[bi].astype(BF16), preferred_element_type=F32) for bi in bs]
        dmat = [jnp.where(causal, b_c[bi] - b_r[bi] + i_r[bi], NEG_INF) for bi in bs]
        inter_log = [b_c[bi] + m_prev[bi] for bi in bs]
        m_row = [jnp.maximum(inter_log[bi], jnp.max(dmat[bi], axis=1, keepdims=True)) for bi in bs]
        sc = [qk_t[bi] * jnp.exp(dmat[bi] - m_row[bi]) for bi in bs]
        inter_scale = [jnp.exp(inter_log[bi] - m_row[bi]) for bi in bs]
        sc_v = [jnp.dot(sc[bi].astype(BF16), v[bi], preferred_element_type=F32) for bi in bs]

        wlog = [g_tot[bi] - b_c[bi] + i_c[bi] for bi in bs]
        m_loc = [jnp.max(x, axis=0, keepdims=True) for x in wlog]
        kw = [kf[bi] * jnp.exp(wlog[bi] - m_loc[bi]) for bi in bs]
        c_chunk = [lax.dot_general(kw[bi].astype(BF16), v[bi], (((0,), (0,)), ((), ())),
                                   preferred_element_type=F32) for bi in bs]

        num = [sc_v[bi] + inter_scale[bi] * q_c[bi] for bi in bs]
        qn = [jnp.sum(qf[bi] * n_prev[bi], axis=1, keepdims=True) for bi in bs]
        den = [jnp.sum(sc[bi], axis=1, keepdims=True) + inter_scale[bi] * qn[bi] for bi in bs]
        hh = [num[bi] / jnp.maximum(jnp.abs(den[bi]), jnp.exp(-m_row[bi])) for bi in bs]
        mu = [jnp.mean(x, axis=1, keepdims=True) for x in hh]
        var = [jnp.mean(jnp.square(hh[bi] - mu[bi]), axis=1, keepdims=True) for bi in bs]
        for bi in bs:
            hn = (hh[bi] - mu[bi]) * lax.rsqrt(var[bi] + LN_EPS) * nw_ref[:, hs]
            y_ref[bi, :, hs] = (o_ref[bi, :, hs].astype(F32) * hn).astype(y_ref.dtype)

        for bi in bs:
            n_chunk = jnp.sum(kw[bi], axis=0, keepdims=True)
            m_new = jnp.maximum(g_tot[bi] + m_prev[bi], m_loc[bi])
            a = jnp.exp(g_tot[bi] + m_prev[bi] - m_new)
            bb = jnp.exp(m_loc[bi] - m_new)
            c_ref[st[bi]] = a * c_prev[bi] + bb * c_chunk[bi]
            n_ref[st[bi]] = a * n_prev[bi] + bb * n_chunk
            m_ref[st[bi]] = jnp.broadcast_to(m_new, (1, LANES))


def _mlstm(qk, v, osig, gcol, grow, bias_c, bias_r, conv_w, norm_w, nh, nb=ML_BATCH):
    bsz, s, w2 = qk.shape
    width = w2 // 2
    dh = width // nh
    chunk = ML_CHUNK
    kconv = conv_w.shape[0]
    nb = math.gcd(nb, bsz)
    assert s % chunk == 0 and kconv - 1 <= SUBLANES
    kern = functools.partial(_mlstm_kernel, nb=nb, nh=nh, dh=dh, chunk=chunk, kconv=kconv)
    return pl.pallas_call(
        kern,
        grid=(bsz // nb, s // chunk),
        in_specs=[pl.BlockSpec((nb, chunk, w2), lambda b, c: (b, c, 0)),
                  pl.BlockSpec((nb, chunk, width), lambda b, c: (b, c, 0)),
                  pl.BlockSpec((nb, chunk, width), lambda b, c: (b, c, 0)),
                  pl.BlockSpec((nb, chunk, LANES), lambda b, c: (b, c, 0)),
                  pl.BlockSpec((nb, 2 * nh, chunk), lambda b, c: (b, 0, c)),
                  _const_spec((1, LANES)),
                  _const_spec((2 * nh, 1)),
                  _const_spec((kconv, w2)),
                  _const_spec((1, width))],
        out_specs=pl.BlockSpec((nb, chunk, width), lambda b, c: (b, c, 0)),
        out_shape=jax.ShapeDtypeStruct((bsz, s, width), BF16),
        scratch_shapes=[pltpu.VMEM((nb, SUBLANES + chunk, w2), F32),
                        pltpu.VMEM((nb, chunk, w2), F32),
                        pltpu.VMEM((nb * nh, dh, dh), F32),
                        pltpu.VMEM((nb * nh, 1, dh), F32),
                        pltpu.VMEM((nb * nh, 1, LANES), F32)],
        compiler_params=_cparams("parallel", "arbitrary"),
        name="mlstm",
    )(qk, v, osig, gcol, grow, bias_c, bias_r, conv_w, norm_w)


def _layer_norm(x, g, b):
    mu = jnp.mean(x, axis=-1, keepdims=True)
    var = jnp.mean(jnp.square(x - mu), axis=-1, keepdims=True)
    return (x - mu) * lax.rsqrt(var + LN_EPS) * g + b


def _mix_kernel(yp_ref, u_ref, yb_ref, ga_ref, gb_ref, x_ref, d_ref, wg_ref, bg_ref, wa_ref, wb_ref, wo_ref,
                lg_ref, lb_ref, h_ref, ht_ref, *, alpha):
    y = jax.nn.gelu(yp_ref[...] + d_ref[...] * u_ref[...])
    gate = jax.nn.sigmoid(jnp.dot(y.astype(BF16), wg_ref[...], preferred_element_type=F32) + bg_ref[...])
    ya = (y * gate).astype(BF16)
    merged = (ga_ref[...].astype(F32) * jnp.dot(ya, wa_ref[...], preferred_element_type=F32)
              + gb_ref[...].astype(F32) * jnp.dot(yb_ref[...], wb_ref[...], preferred_element_type=F32))
    mix = jnp.dot(merged.astype(BF16), wo_ref[...], preferred_element_type=F32)
    h = _layer_norm(alpha * x_ref[...] + mix, lg_ref[...], lb_ref[...])
    h_ref[...] = h
    ht_ref[...] = h.T.astype(BF16)


def _mix(y_pre, u, y_b, gates, x, d, w_glu, b_glu, w_a, w_b, w_o, ln_g, ln_b, alpha, tm=256):
    t, dm = x.shape
    w = u.shape[1]
    assert t % tm == 0
    row = lambda width: pl.BlockSpec((tm, width), lambda i: (i, 0))
    return pl.pallas_call(
        functools.partial(_mix_kernel, alpha=alpha),
        grid=(t // tm,),
        in_specs=[row(w), row(w), row(w),
                  pl.BlockSpec((tm, dm), lambda i: (i, 0)),
                  pl.BlockSpec((tm, dm), lambda i: (i, 1)),
                  row(dm),
                  _const_spec((1, w)), _const_spec((w, w)), _const_spec((1, w)),
                  _const_spec((w, dm)), _const_spec((w, dm)), _const_spec((dm, dm)),
                  _const_spec((1, dm)), _const_spec((1, dm))],
        out_specs=[pl.BlockSpec((tm, dm), lambda i: (i, 0)),
                   pl.BlockSpec((dm, tm), lambda i: (0, i))],
        out_shape=[jax.ShapeDtypeStruct((t, dm), F32), jax.ShapeDtypeStruct((dm, t), BF16)],
        compiler_params=_cparams("parallel"),
        name="mix_ln1",
    )(y_pre, u, y_b, gates, gates, x, d, w_glu, b_glu, w_a, w_b, w_o, ln_g, ln_b)


def _oddeven_merge(lo, hi, r):
    step = r * 2
    if step < hi - lo:
        yield from _oddeven_merge(lo, hi, step)
        yield from _oddeven_merge(lo + r, hi, step)
        yield from [(i, i + r) for i in range(lo + r, hi - r, step)]
    else:
        yield (lo, lo + r)


def _oddeven_merge_sort(lo, hi):
    if hi - lo >= 1:
        mid = lo + (hi - lo) // 2
        yield from _oddeven_merge_sort(lo, mid)
        yield from _oddeven_merge_sort(mid + 1, hi)
        yield from _oddeven_merge(lo, hi, 1)


def _sorted_topk_rows(s, dst_ref, k):
    g = s.shape[0] // SUBLANES
    w = [s[j * SUBLANES:(j + 1) * SUBLANES, :] for j in range(g)]
    for i, j in _oddeven_merge_sort(0, g - 1):
        w[i], w[j] = jnp.maximum(w[i], w[j]), jnp.minimum(w[i], w[j])
    for r in range(k):
        mx = jnp.max(w[0], axis=0, keepdims=True)
        dst_ref[r:r + 1, :] = mx
        left = k - 1 - r
        if left > 0:
            hit = w[0] == mx
            for lvl in range(min(g, left)):
                below = w[lvl + 1] if lvl + 1 < g else NEG_INF
                w[lvl] = jnp.where(hit, below, w[lvl])


def _route_kernel(h_ref, wq_ref, keys_ref, thr_ref, e1_ref, s2_ref, e2_ref, a_ref, b_ref, cand_ref,
                  *, nh, nk, half, topk, ncand_rows):
    q = jnp.dot(h_ref[...].astype(BF16), wq_ref[...], preferred_element_type=F32).astype(BF16)
    tb = q.shape[0]
    nrank = topk + 1
    for h in range(nh):
        s1 = lax.dot_general(keys_ref[2 * h], q[:, (2 * h) * half:(2 * h + 1) * half],
                             (((1,), (1,)), ((), ())), preferred_element_type=F32)
        s2 = lax.dot_general(keys_ref[2 * h + 1], q[:, (2 * h + 1) * half:(2 * h + 2) * half],
                             (((1,), (1,)), ((), ())), preferred_element_type=F32)
        _sorted_topk_rows(s1, a_ref, nrank)
        _sorted_topk_rows(s2, b_ref, nrank)
        a = a_ref[0:nrank, :]
        b = b_ref[0:nrank, :]
        cand_ref[...] = jnp.full((ncand_rows, tb), NEG_INF, F32)
        off = 0
        for i in range(nrank):
            n_i = nrank // (i + 1)
            cand_ref[off:off + n_i, :] = a[i:i + 1, :] + b[0:n_i, :]
            off += n_i
        cur = cand_ref[...]
        top = a[0:1, :] + b[0:1, :]
        z = jnp.zeros((1, tb), F32)
        kth = top
        for r in range(topk):
            kth = jnp.max(cur, axis=0, keepdims=True)
            z = z + jnp.exp(kth - top)
            cur = jnp.where(cur == kth, NEG_INF, cur)
        nxt = jnp.max(cur, axis=0, keepdims=True)
        tau = 0.5 * (kth + nxt)
        thr_ref[h] = tau - s1
        e1_ref[h] = jnp.exp(s1 - a[0:1, :])
        s2_ref[h] = s2
        e2_ref[h] = jnp.exp(s2 - b[0:1, :]) / z


def _route(h, wq, keys2, nh, tb=256):
    t, dm = h.shape
    nk, half = keys2.shape[1], keys2.shape[2]
    topk = PEER_TOPK
    nrank = topk + 1
    ncand = sum(nrank // (i + 1) for i in range(nrank))
    ncand_rows = -(-ncand // SUBLANES) * SUBLANES
    rank_rows = -(-nrank // SUBLANES) * SUBLANES
    assert t % tb == 0 and nk > nrank
    kern = functools.partial(_route_kernel, nh=nh, nk=nk, half=half, topk=topk, ncand_rows=ncand_rows)
    out = jax.ShapeDtypeStruct((nh, nk, t), F32)
    ospec = pl.BlockSpec((nh, nk, tb), lambda i: (0, 0, i))
    return pl.pallas_call(
        kern,
        grid=(t // tb,),
        in_specs=[pl.BlockSpec((tb, dm), lambda i: (i, 0)),
                  _const_spec(wq.shape), _const_spec(keys2.shape)],
        out_specs=[ospec] * 4,
        out_shape=[out] * 4,
        scratch_shapes=[pltpu.VMEM((rank_rows, tb), F32), pltpu.VMEM((rank_rows, tb), F32),
                        pltpu.VMEM((ncand_rows, tb), F32)],
        compiler_params=_cparams("parallel"),
        name="peer_route",
    )(h, wq, keys2)


_GELU_K1 = -2.0 * math.sqrt(2.0 / math.pi) * math.log2(math.e)
_GELU_K2 = _GELU_K1 * 0.044715


def _gelu_tanh(x):
    return x / (1.0 + jnp.exp2(x * (_GELU_K1 + _GELU_K2 * (x * x))))


def _expert_kernel(xt_ref, u_ref, vt_ref, thr_ref, e1_ref, s2_ref, e2_ref, o_ref, sa_ref, sb_ref,
                   *, nh, nk, rows, sub):
    e = pl.program_id(1)

    @pl.when(e == 0)
    def _():
        o_ref[...] = jnp.zeros_like(o_ref)
        sa_ref[...] = jnp.zeros_like(sa_ref)

    def step(rd_ref, wr_ref):
        for q in range(xt_ref.shape[1] // sub):
            cs = slice(q * sub, (q + 1) * sub)
            wr_ref[:, cs] = jnp.dot(u_ref[...], xt_ref[:, cs], preferred_element_type=F32)
            parts = []
            for r in range(rows):
                g = None
                for h in range(nh):
                    sel = jnp.where(s2_ref[h, :, cs] >= thr_ref[h, r:r + 1, cs],
                                    e2_ref[h, :, cs] * e1_ref[h, r:r + 1, cs], 0.0)
                    g = sel if g is None else g + sel
                parts.append((_gelu_tanh(rd_ref[r * nk:(r + 1) * nk, cs]) * g).astype(BF16))
            act = jnp.concatenate(parts, axis=0)
            o_ref[:, cs] += jnp.dot(vt_ref[...], act, preferred_element_type=F32)

    @pl.when(lax.rem(e, 2) == 0)
    def _():
        step(sa_ref, sb_ref)

    @pl.when(lax.rem(e, 2) == 1)
    def _():
        step(sb_ref, sa_ref)


def _experts(xt, u_tab, vt_tab, thr, e1, s2, e2, tb=1024, rows=8, sub=256):
    dm, t = xt.shape
    ne = u_tab.shape[0]
    nh, nk, _ = thr.shape
    neb = rows * nk
    tb = min(tb, t)
    assert t % tb == 0 and tb % sub == 0 and ne % neb == 0 and ne == nk * nk
    nblk = ne // neb
    kern = functools.partial(_expert_kernel, nh=nh, nk=nk, rows=rows, sub=sub)
    once = pl.Buffered(1)
    score_blk = lambda e: jnp.minimum(e, nblk - 1)
    gate_blk = lambda e: jnp.maximum(e - 1, 0)
    return pl.pallas_call(
        kern,
        grid=(t // tb, nblk + 1),
        in_specs=[pl.BlockSpec((dm, tb), lambda i, e: (0, i), pipeline_mode=once),
                  pl.BlockSpec((neb, dm), lambda i, e: (score_blk(e), 0)),
                  pl.BlockSpec((dm, neb), lambda i, e: (0, gate_blk(e))),
                  pl.BlockSpec((nh, rows, tb), lambda i, e: (0, gate_blk(e), i)),
                  pl.BlockSpec((nh, rows, tb), lambda i, e: (0, gate_blk(e), i)),
                  pl.BlockSpec((nh, nk, tb), lambda i, e: (0, 0, i), pipeline_mode=once),
                  pl.BlockSpec((nh, nk, tb), lambda i, e: (0, 0, i), pipeline_mode=once)],
        out_specs=pl.BlockSpec((dm, tb), lambda i, e: (0, i), pipeline_mode=once),
        out_shape=jax.ShapeDtypeStruct((dm, t), F32),
        scratch_shapes=[pltpu.VMEM((neb, tb), F32), pltpu.VMEM((neb, tb), F32)],
        compiler_params=_cparams("parallel", "arbitrary"),
        name="peer_experts",
    )(xt, u_tab, vt_tab, thr, e1, s2, e2)


def _final_kernel(h_ref, yt_ref, p_ref, lg_ref, lb_ref, wg_ref, wp_ref, o_ref, *, alpha):
    h = _layer_norm(alpha * h_ref[...] + yt_ref[...].T, lg_ref[...], lb_ref[...])
    gate = jax.nn.sigmoid(jnp.dot(h.astype(BF16), wg_ref[...], preferred_element_type=F32))
    proj = jnp.dot(p_ref[...].astype(BF16), wp_ref[...], preferred_element_type=F32)
    o_ref[...] = h + gate * proj


def _final(h, yt, p, ln_g, ln_b, w_gate, w_proj, alpha, tm=512):
    t, dm = h.shape
    pd = p.shape[1]
    assert t % tm == 0
    return pl.pallas_call(
        functools.partial(_final_kernel, alpha=alpha),
        grid=(t // tm,),
        in_specs=[pl.BlockSpec((tm, dm), lambda i: (i, 0)),
                  pl.BlockSpec((dm, tm), lambda i: (0, i)),
                  pl.BlockSpec((tm, pd), lambda i: (i, 0)),
                  _const_spec((1, dm)), _const_spec((1, dm)),
                  _const_spec((dm, dm)), _const_spec((pd, dm))],
        out_specs=pl.BlockSpec((tm, dm), lambda i: (i, 0)),
        out_shape=jax.ShapeDtypeStruct((t, dm), F32),
        compiler_params=_cparams("parallel"),
        name="ln2_ple",
    )(h, yt, p, ln_g, ln_b, w_gate, w_proj)


def _layer(h, p, w_in, b_igate, b_fgate, conv_qk, mh_norm_w, a_re, a_im, log_dt, b_re, b_im, c_re, c_im, d_skip,
           w_glu, b_glu, w_up_ssm, w_up_ml, w_out, ln1_g, ln1_b, peer_wq, peer_keys, peer_u, peer_v,
           ln2_g, ln2_b, ple_w_gate, ple_w_proj, alpha):
    bsz, s, dm = h.shape
    t = bsz * s
    nh = b_igate.shape[0]
    ssm_w = d_skip.shape[0]
    ml_w = mh_norm_w.shape[0]
    x2 = h.reshape(t, dm)
    xb = x2.astype(BF16)

    o0 = ssm_w
    o1 = o0 + 2 * ml_w
    o2 = o1 + ml_w
    o3 = o2 + ml_w
    o4 = o3 + 2 * nh
    w_main = jnp.concatenate([w_in[:, :o3], w_in[:, o4:]], axis=1).astype(BF16)
    u, qk, v, osig, gates = _in_proj(xb, w_main, [(ssm_w, F32, False), (2 * ml_w, F32, False), (ml_w, BF16, False),
                                                   (ml_w, BF16, True), (w_in.shape[1] - o4, BF16, True)])
    w_if = jnp.zeros((dm, LANES), F32).at[:, :2 * nh].set(w_in[:, o3:o4]).astype(BF16)
    gif = _matmul(xb, w_if, F32)

    bb, cb, lam = _s5_params(a_re, a_im, log_dt, b_re, b_im, c_re, c_im)
    y_pre = _s5_scan(u.reshape(bsz, s, ssm_w), bb, cb, lam)

    gcol = gif.reshape(bsz, s, LANES)
    grow = jnp.swapaxes(gcol[:, :, :2 * nh], 1, 2)
    bias = jnp.concatenate([b_igate, b_fgate])
    bias_c = jnp.zeros((1, LANES), F32).at[0, :2 * nh].set(bias)
    y_b = _mlstm(qk.reshape(bsz, s, 2 * ml_w), v.reshape(bsz, s, ml_w), osig.reshape(bsz, s, ml_w),
                 gcol, grow, bias_c, bias[:, None], conv_qk, mh_norm_w[None, :], nh)

    h1, h1t = _mix(y_pre.reshape(t, ssm_w), u, y_b.reshape(t, ml_w), gates, x2, d_skip[None, :],
                   w_glu.astype(BF16), b_glu[None, :], w_up_ssm.astype(BF16), w_up_ml.astype(BF16),
                   w_out.astype(BF16), ln1_g[None, :], ln1_b[None, :], alpha)

    ph, _, nk, half = peer_keys.shape
    keys2 = peer_keys.reshape(ph * 2, nk, half).astype(BF16)
    thr, e1, s2, e2 = _route(h1, peer_wq.astype(BF16), keys2, ph)
    y2t = _experts(h1t, peer_u.astype(BF16), peer_v.T.astype(BF16), thr, e1, s2, e2)

    out = _final(h1, y2t, p.reshape(t, -1), ln2_g[None, :], ln2_b[None, :],
                 ple_w_gate.astype(BF16), ple_w_proj.astype(BF16), alpha)
    return out.reshape(bsz, s, dm)


def kernel(x, p, w_in, b_igate, b_fgate, conv_qk, mh_norm_w, ssm_a_re, ssm_a_im, ssm_log_dt, ssm_b_re, ssm_b_im,
           ssm_c_re, ssm_c_im, ssm_d, w_glu, b_glu, w_up_ssm, w_up_ml, w_out, ln1_g, ln1_b, peer_wq, peer_keys,
           peer_u, peer_v, ln2_g, ln2_b, ple_w_gate, ple_w_proj):
    depth = w_in.shape[0]
    alpha = (2 * depth) ** 0.25
    h = x
    for i in range(depth):
        h = _layer(h, p[i], w_in[i], b_igate[i], b_fgate[i], conv_qk[i], mh_norm_w[i], ssm_a_re[i], ssm_a_im[i],
                   ssm_log_dt[i], ssm_b_re[i], ssm_b_im[i], ssm_c_re[i], ssm_c_im[i], ssm_d[i], w_glu[i], b_glu[i],
                   w_up_ssm[i], w_up_ml[i], w_out[i], ln1_g[i], ln1_b[i], peer_wq[i], peer_keys[i], peer_u[i],
                   peer_v[i], ln2_g[i], ln2_b[i], ple_w_gate[i], ple_w_proj[i], alpha)
    return h
```

```python
import functools
import math

import jax
import jax.numpy as jnp
from jax import lax
from jax.experimental import pallas as pl
from jax.experimental.pallas import tpu as pltpu

F32 = jnp.float32
BF16 = jnp.bfloat16

LN_EPS = 1e-5
PEER_TOPK = 16
LANES = 128
SUBLANES = 8
VMEM_LIMIT = 56 * 1024 * 1024

ML_CHUNK = 128
ML_BATCH = 4
S5_CHUNK = 128
S5_PITCH = S5_CHUNK + SUBLANES
NEG_INF = float("-inf")


def _cparams(*sem):
    return pltpu.CompilerParams(dimension_semantics=sem, vmem_limit_bytes=VMEM_LIMIT)


def _const_spec(shape):
    nd = len(shape)
    return pl.BlockSpec(shape, lambda *_: (0,) * nd, pipeline_mode=pl.Buffered(1))


def _mm_kernel(a_ref, w_ref, o_ref):
    o_ref[...] = jnp.dot(a_ref[...], w_ref[...], preferred_element_type=F32).astype(o_ref.dtype)


def _matmul(a, w, out_dtype, tm=1024, tn=512):
    m, k = a.shape
    n = w.shape[1]
    tm, tn = min(tm, m), min(tn, n)
    assert m % tm == 0 and n % tn == 0
    return pl.pallas_call(
        _mm_kernel,
        grid=(m // tm, n // tn),
        in_specs=[pl.BlockSpec((tm, k), lambda i, j: (i, 0)),
                  pl.BlockSpec((k, tn), lambda i, j: (0, j))],
        out_specs=pl.BlockSpec((tm, tn), lambda i, j: (i, j)),
        out_shape=jax.ShapeDtypeStruct((m, n), out_dtype),
        compiler_params=_cparams("parallel", "parallel"),
        name="gate_proj",
    )(a, w)


def _in_proj_kernel(a_ref, *refs, segs, nw, sub):
    j = pl.program_id(1)
    w_refs, o_refs = refs[:nw], refs[nw:]
    for o_ref, (lo, hi, wi, sig) in zip(o_refs, segs):
        @pl.when((j >= lo) & (j < hi))
        def _(o_ref=o_ref, w_ref=w_refs[wi], sig=sig):
            for q in range(o_ref.shape[1] // sub):
                cs = slice(q * sub, (q + 1) * sub)
                acc = jnp.dot(a_ref[...], w_ref[:, cs], preferred_element_type=F32)
                o_ref[:, cs] = (jax.nn.sigmoid(acc) if sig else acc).astype(o_ref.dtype)


def _in_proj(a, weights, outs, tm=1024, tn=1024, sub=256):
    m, k = a.shape
    tm = min(tm, m)
    assert m % tm == 0 and all(o[1] % tn == 0 for o in outs)
    segs, ospecs, shapes, step = [], [], [], 0
    wlo = [None] * len(weights)
    wnb = [0] * len(weights)
    for wi, width, dtype, sig in outs:
        nblk = width // tn
        if wlo[wi] is None:
            wlo[wi] = step
        assert wlo[wi] + wnb[wi] == step, "outputs sharing a weight must be adjacent"
        wnb[wi] += nblk
        segs.append((step, step + nblk, wi, sig))
        ospecs.append(pl.BlockSpec((tm, tn), lambda i, j, lo=step, nblk=nblk: (i, jnp.clip(j - lo, 0, nblk - 1))))
        shapes.append(jax.ShapeDtypeStruct((m, width), dtype))
        step += nblk
    assert all(w.shape == (k, nb * tn) for w, nb in zip(weights, wnb))
    wspecs = [pl.BlockSpec((k, tn), lambda i, j, lo=lo, nb=nb: (0, jnp.clip(j - lo, 0, nb - 1)))
              for lo, nb in zip(wlo, wnb)]
    return pl.pallas_call(
        functools.partial(_in_proj_kernel, segs=tuple(segs), nw=len(weights), sub=sub),
        grid=(m // tm, step),
        in_specs=[pl.BlockSpec((tm, k), lambda i, j: (i, 0))] + wspecs,
        out_specs=ospecs,
        out_shape=shapes,
        compiler_params=_cparams("parallel", "arbitrary"),
        name="in_proj",
    )(a, *weights)


def _s5_kernel(u_ref, bb_ref, cb_ref, lam_ref, y_ref, bu_ref, st_ref, *, bsz, chunk, pitch, ntile):
    c = pl.program_id(1)

    @pl.when(c == 0)
    def _():
        st_ref[...] = jnp.zeros_like(st_ref)

    bb = bb_ref[0]
    for b in range(bsz):
        bu = jnp.dot(u_ref[b].astype(BF16), bb, preferred_element_type=F32)
        for k in range(2 * ntile):
            bu_ref[k, b * pitch:b * pitch + chunk, :] = bu[:, k * LANES:(k + 1) * LANES]

    lam = lam_ref[0]
    lr = [jnp.broadcast_to(lam[k:k + 1, :], (bsz, LANES)) for k in range(ntile)]
    li = [jnp.broadcast_to(lam[ntile + k:ntile + k + 1, :], (bsz, LANES)) for k in range(ntile)]

    def step(t, carry):
        new = []
        for k in range(ntile):
            sr, si = carry[2 * k], carry[2 * k + 1]
            rows = pl.ds(t, bsz, stride=pitch)
            nr = lr[k] * sr - li[k] * si + bu_ref[k, rows, :]
            ni = lr[k] * si + li[k] * sr + bu_ref[ntile + k, rows, :]
            bu_ref[k, rows, :] = nr
            bu_ref[ntile + k, rows, :] = ni
            new += [nr, ni]
        return tuple(new)

    init = []
    for k in range(ntile):
        init += [st_ref[k], st_ref[ntile + k]]
    fin = lax.fori_loop(0, chunk, step, tuple(init), unroll=2)
    for k in range(ntile):
        st_ref[k] = fin[2 * k]
        st_ref[ntile + k] = fin[2 * k + 1]

    cb = cb_ref[0]
    for b in range(bsz):
        st = jnp.concatenate([bu_ref[k, b * pitch:b * pitch + chunk, :] for k in range(2 * ntile)], axis=1)
        y_ref[b] = jnp.dot(st.astype(BF16), cb, preferred_element_type=F32)


def _s5_scan(u, bb, cb, lam):
    bsz, s, w = u.shape
    nslab = w // LANES
    ntile = lam.shape[1] // 2
    chunk, pitch = S5_CHUNK, S5_PITCH
    assert s % chunk == 0 and bsz <= SUBLANES
    kern = functools.partial(_s5_kernel, bsz=bsz, chunk=chunk, pitch=pitch, ntile=ntile)
    return pl.pallas_call(
        kern,
        grid=(nslab, s // chunk),
        in_specs=[pl.BlockSpec((bsz, chunk, LANES), lambda j, c: (0, c, j)),
                  pl.BlockSpec((1, LANES, 2 * ntile * LANES), lambda j, c: (j, 0, 0)),
                  pl.BlockSpec((1, 2 * ntile * LANES, LANES), lambda j, c: (j, 0, 0)),
                  pl.BlockSpec((1, 2 * ntile, LANES), lambda j, c: (j, 0, 0))],
        out_specs=pl.BlockSpec((bsz, chunk, LANES), lambda j, c: (0, c, j)),
        out_shape=jax.ShapeDtypeStruct((bsz, s, w), F32),
        scratch_shapes=[pltpu.VMEM((2 * ntile, bsz * pitch, LANES), F32),
                        pltpu.VMEM((2 * ntile, bsz, LANES), F32)],
        compiler_params=_cparams("parallel", "arbitrary"),
        name="s5_scan",
    )(u, bb, cb, lam)


def _s5_params(a_re, a_im, log_dt, b_re, b_im, c_re, c_im):
    g, p = a_re.shape
    gc = b_re.shape[2]
    gps = LANES // gc
    nslab = g // gps
    dt = jnp.exp(log_dt)[:, None]
    mag = jnp.exp(dt * a_re)
    lam_r = mag * jnp.cos(dt * a_im)
    lam_i = mag * jnp.sin(dt * a_im)
    den = a_re * a_re + a_im * a_im
    zr = ((lam_r - 1.0) * a_re + lam_i * a_im) / den
    zi = (lam_i * a_re - (lam_r - 1.0) * a_im) / den
    bbar_r = zr[..., None] * b_re - zi[..., None] * b_im
    bbar_i = zr[..., None] * b_im + zi[..., None] * b_re
    eye = jnp.eye(gps, dtype=F32)

    def in_slab(bm):
        bm = bm.reshape(nslab, gps, p, gc)
        return jnp.einsum("jgpc,gh->jgchp", bm, eye).reshape(nslab, gps * gc, gps * p)

    def out_slab(cm):
        cm = cm.reshape(nslab, gps, gc, p)
        return jnp.einsum("jgcp,gh->jhpgc", cm, eye).reshape(nslab, gps * p, gps * gc)

    bb = jnp.concatenate([in_slab(bbar_r), in_slab(bbar_i)], axis=2).astype(BF16)
    cb = jnp.concatenate([out_slab(c_re), -out_slab(c_im)], axis=1).astype(BF16)
    ntile = gps * p // LANES
    lam = jnp.concatenate([lam_r.reshape(nslab, ntile, LANES), lam_i.reshape(nslab, ntile, LANES)], axis=1)
    return bb, cb, lam


def _log_sigmoid(x):
    return jnp.minimum(x, 0.0) - jnp.log1p(jnp.exp(-jnp.abs(x)))


def _mlstm_kernel(qk_ref, v_ref, o_ref, gc_ref, gr_ref, bc_ref, br_ref, cw_ref, nw_ref, y_ref,
                  buf_ref, qs_ref, c_ref, n_ref, m_ref, *, nb, nh, dh, chunk, kconv):
    ci = pl.program_id(1)
    width = nh * dh
    halo = SUBLANES

    @pl.when(ci == 0)
    def _():
        buf_ref[:, 0:halo, :] = jnp.zeros((nb, halo, 2 * width), F32)
        c_ref[...] = jnp.zeros_like(c_ref)
        n_ref[...] = jnp.zeros_like(n_ref)
        m_ref[...] = jnp.zeros_like(m_ref)

    tt = lax.broadcasted_iota(jnp.int32, (chunk, chunk), 0)
    ss = lax.broadcasted_iota(jnp.int32, (chunk, chunk), 1)
    causal = ss <= tt
    tril = causal.astype(F32)
    triu = (tt <= ss).astype(F32)

    gates = []
    for bi in range(nb):
        buf_ref[bi, halo:halo + chunk, :] = qk_ref[bi]
        base = halo - (kconv - 1)
        conv = cw_ref[0:1, :] * buf_ref[bi, base:base + chunk, :]
        for j in range(1, kconv):
            conv = conv + cw_ref[j:j + 1, :] * buf_ref[bi, base + j:base + j + chunk, :]
        buf_ref[bi, 0:halo, :] = buf_ref[bi, chunk:chunk + halo, :]
        qs_ref[bi] = conv * jax.nn.sigmoid(conv)

        gcol = gc_ref[bi] + bc_ref[...]
        col_id = lax.broadcasted_iota(jnp.int32, gcol.shape, 1)
        lcol = jnp.where(col_id >= nh, _log_sigmoid(gcol), gcol)
        grow = gr_ref[bi] + br_ref[...]
        row_id = lax.broadcasted_iota(jnp.int32, grow.shape, 0)
        lrow = jnp.where(row_id >= nh, _log_sigmoid(grow), grow)
        bcol = jnp.dot(tril, lcol, preferred_element_type=F32, precision=lax.Precision.HIGHEST)
        brow = jnp.dot(lrow, triu, preferred_element_type=F32, precision=lax.Precision.HIGHEST)
        gates.append((lcol, lrow, bcol, brow))

    inv_sqrt = 1.0 / math.sqrt(dh)
    for h in range(nh):
        hs = slice(h * dh, (h + 1) * dh)
        ks = slice(width + h * dh, width + (h + 1) * dh)
        bs = range(nb)
        st = [bi * nh + h for bi in bs]
        qf = [qs_ref[bi, :, hs] for bi in bs]
        q = [x.astype(BF16) for x in qf]
        kf = [qs_ref[bi, :, ks] * inv_sqrt for bi in bs]
        k = [x.astype(BF16) for x in kf]
        v = [v_ref[bi, :, hs] for bi in bs]
        b_c = [gates[bi][2][:, nh + h:nh + h + 1] for bi in bs]
        i_c = [gates[bi][0][:, h:h + 1] for bi in bs]
        b_r = [gates[bi][3][nh + h:nh + h + 1, :] for bi in bs]
        i_r = [gates[bi][1][h:h + 1, :] for bi in bs]
        g_tot = [x[chunk - 1:chunk, :] for x in b_c]
        m_prev = [m_ref[s_][:, 0:1] for s_ in st]
        c_prev = [c_ref[s_] for s_ in st]
        n_prev = [n_ref[s_] for s_ in st]

        qk_t = [lax.dot_general(q[bi], k[bi], (((1,), (1,)), ((), ())), preferred_element_type=F32) for bi in bs]
        q_c = [jnp.dot(q[bi], c_prev[bi].astype(BF16), preferred_element_type=F32) for bi in bs]
        dmat = [jnp.where(causal, b_c[bi] - b_r[bi] + i_r[bi], NEG_INF) for bi in bs]
        inter_log = [b_c[bi] + m_prev[bi] for bi in bs]
        m_row = [jnp.maximum(inter_log[bi], jnp.max(dmat[bi], axis=1, keepdims=True)) for bi in bs]
        sc = [qk_t[bi] * jnp.exp(dmat[bi] - m_row[bi]) for bi in bs]
        inter_scale = [jnp.exp(inter_log[bi] - m_row[bi]) for bi in bs]
        sc_v = [jnp.dot(sc[bi].astype(BF16), v[bi], preferred_element_type=F32) for bi in bs]

        wlog = [g_tot[bi] - b_c[bi] + i_c[bi] for bi in bs]
        m_loc = [jnp.max(x, axis=0, keepdims=True) for x in wlog]
        kw = [kf[bi] * jnp.exp(wlog[bi] - m_loc[bi]) for bi in bs]
        c_chunk = [lax.dot_general(kw[bi].astype(BF16), v[bi], (((0,), (0,)), ((), ())),
                                   preferred_element_type=F32) for bi in bs]

        num = [sc_v[bi] + inter_scale[bi] * q_c[bi] for bi in bs]
        qn = [jnp.sum(qf[bi] * n_prev[bi], axis=1, keepdims=True) for bi in bs]
        den = [jnp.sum(sc[bi], axis=1, keepdims=True) + inter_scale[bi] * qn[bi] for bi in bs]
        hh = [num[bi] / jnp.maximum(jnp.abs(den[bi]), jnp.exp(-m_row[bi])) for bi in bs]
        mu = [jnp.mean(x, axis=1, keepdims=True) for x in hh]
        var = [jnp.mean(jnp.square(hh[bi] - mu[bi]), axis=1, keepdims=True) for bi in bs]
        for bi in bs:
            hn = (hh[bi] - mu[bi]) * lax.rsqrt(var[bi] + LN_EPS) * nw_ref[:, hs]
            y_ref[bi, :, hs] = (o_ref[bi, :, hs].astype(F32) * hn).astype(y_ref.dtype)

        for bi in bs:
            n_chunk = jnp.sum(kw[bi], axis=0, keepdims=True)
            m_new = jnp.maximum(g_tot[bi] + m_prev[bi], m_loc[bi])
            a = jnp.exp(g_tot[bi] + m_prev[bi] - m_new)
            bb = jnp.exp(m_loc[bi] - m_new)
            c_ref[st[bi]] = a * c_prev[bi] + bb * c_chunk[bi]
            n_ref[st[bi]] = a * n_prev[bi] + bb * n_chunk
            m_ref[st[bi]] = jnp.broadcast_to(m_new, (1, LANES))


def _mlstm(qk, v, osig, gcol, grow, bias_c, bias_r, conv_w, norm_w, nh, nb=ML_BATCH):
    bsz, s, w2 = qk.shape
    width = w2 // 2
    dh = width // nh
    chunk = ML_CHUNK
    kconv = conv_w.shape[0]
    nb = math.gcd(nb, bsz)
    assert s % chunk == 0 and kconv - 1 <= SUBLANES
    kern = functools.partial(_mlstm_kernel, nb=nb, nh=nh, dh=dh, chunk=chunk, kconv=kconv)
    return pl.pallas_call(
        kern,
        grid=(bsz // nb, s // chunk),
        in_specs=[pl.BlockSpec((nb, chunk, w2), lambda b, c: (b, c, 0)),
                  pl.BlockSpec((nb, chunk, width), lambda b, c: (b, c, 0)),
                  pl.BlockSpec((nb, chunk, width), lambda b, c: (b, c, 0)),
                  pl.BlockSpec((nb, chunk, LANES), lambda b, c: (b, c, 0)),
                  pl.BlockSpec((nb, 2 * nh, chunk), lambda b, c: (b, 0, c)),
                  _const_spec((1, LANES)),
                  _const_spec((2 * nh, 1)),
                  _const_spec((kconv, w2)),
                  _const_spec((1, width))],
        out_specs=pl.BlockSpec((nb, chunk, width), lambda b, c: (b, c, 0)),
        out_shape=jax.ShapeDtypeStruct((bsz, s, width), BF16),
        scratch_shapes=[pltpu.VMEM((nb, SUBLANES + chunk, w2), F32),
                        pltpu.VMEM((nb, chunk, w2), F32),
                        pltpu.VMEM((nb * nh, dh, dh), F32),
                        pltpu.VMEM((nb * nh, 1, dh), F32),
                        pltpu.VMEM((nb * nh, 1, LANES), F32)],
        compiler_params=_cparams("parallel", "arbitrary"),
        name="mlstm",
    )(qk, v, osig, gcol, grow, bias_c, bias_r, conv_w, norm_w)


def _layer_norm(x, g, b):
    mu = jnp.mean(x, axis=-1, keepdims=True)
    var = jnp.mean(jnp.square(x - mu), axis=-1, keepdims=True)
    return (x - mu) * lax.rsqrt(var + LN_EPS) * g + b


def _mix_kernel(yp_ref, u_ref, yb_ref, ga_ref, gb_ref, x_ref, d_ref, wg_ref, bg_ref, wa_ref, wb_ref, wo_ref,
                lg_ref, lb_ref, h_ref, ht_ref, *, alpha):
    y = jax.nn.gelu(yp_ref[...] + d_ref[...] * u_ref[...])
    gate = jax.nn.sigmoid(jnp.dot(y.astype(BF16), wg_ref[...], preferred_element_type=F32) + bg_ref[...])
    ya = (y * gate).astype(BF16)
    merged = (ga_ref[...].astype(F32) * jnp.dot(ya, wa_ref[...], preferred_element_type=F32)
              + gb_ref[...].astype(F32) * jnp.dot(yb_ref[...], wb_ref[...], preferred_element_type=F32))
    mix = jnp.dot(merged.astype(BF16), wo_ref[...], preferred_element_type=F32)
    h = _layer_norm(alpha * x_ref[...] + mix, lg_ref[...], lb_ref[...])
    h_ref[...] = h
    ht_ref[...] = h.T.astype(BF16)


def _mix(y_pre, u, y_b, gates, x, d, w_glu, b_glu, w_a, w_b, w_o, ln_g, ln_b, alpha, tm=256):
    t, dm = x.shape
    w = u.shape[1]
    assert t % tm == 0
    row = lambda width: pl.BlockSpec((tm, width), lambda i: (i, 0))
    return pl.pallas_call(
        functools.partial(_mix_kernel, alpha=alpha),
        grid=(t // tm,),
        in_specs=[row(w), row(w), row(w),
                  pl.BlockSpec((tm, dm), lambda i: (i, 0)),
                  pl.BlockSpec((tm, dm), lambda i: (i, 1)),
                  row(dm),
                  _const_spec((1, w)), _const_spec((w, w)), _const_spec((1, w)),
                  _const_spec((w, dm)), _const_spec((w, dm)), _const_spec((dm, dm)),
                  _const_spec((1, dm)), _const_spec((1, dm))],
        out_specs=[pl.BlockSpec((tm, dm), lambda i: (i, 0)),
                   pl.BlockSpec((dm, tm), lambda i: (0, i))],
        out_shape=[jax.ShapeDtypeStruct((t, dm), F32), jax.ShapeDtypeStruct((dm, t), BF16)],
        compiler_params=_cparams("parallel"),
        name="mix_ln1",
    )(y_pre, u, y_b, gates, gates, x, d, w_glu, b_glu, w_a, w_b, w_o, ln_g, ln_b)


def _oddeven_merge(lo, hi, r):
    step = r * 2
    if step < hi - lo:
        yield from _oddeven_merge(lo, hi, step)
        yield from _oddeven_merge(lo + r, hi, step)
        yield from [(i, i + r) for i in range(lo + r, hi - r, step)]
    else:
        yield (lo, lo + r)


def _oddeven_merge_sort(lo, hi):
    if hi - lo >= 1:
        mid = lo + (hi - lo) // 2
        yield from _oddeven_merge_sort(lo, mid)
        yield from _oddeven_merge_sort(mid + 1, hi)
        yield from _oddeven_merge(lo, hi, 1)


def _sorted_topk_rows(s, dst_ref, k):
    g = s.shape[0] // SUBLANES
    w = [s[j * SUBLANES:(j + 1) * SUBLANES, :] for j in range(g)]
    for i, j in _oddeven_merge_sort(0, g - 1):
        w[i], w[j] = jnp.maximum(w[i], w[j]), jnp.minimum(w[i], w[j])
    for r in range(k):
        mx = jnp.max(w[0], axis=0, keepdims=True)
        dst_ref[r:r + 1, :] = mx
        left = k - 1 - r
        if left > 0:
            hit = w[0] == mx
            for lvl in range(min(g, left)):
                below = w[lvl + 1] if lvl + 1 < g else NEG_INF
                w[lvl] = jnp.where(hit, below, w[lvl])


def _route_kernel(h_ref, wq_ref, keys_ref, thr_ref, e1_ref, s2_ref, e2_ref, a_ref, b_ref, cand_ref,
                  *, nh, nk, half, topk, ncand_rows):
    q = jnp.dot(h_ref[...].astype(BF16), wq_ref[...], preferred_element_type=F32).astype(BF16)
    tb = q.shape[0]
    nrank = topk + 1
    for h in range(nh):
        s1 = lax.dot_general(keys_ref[2 * h], q[:, (2 * h) * half:(2 * h + 1) * half],
                             (((1,), (1,)), ((), ())), preferred_element_type=F32)
        s2 = lax.dot_general(keys_ref[2 * h + 1], q[:, (2 * h + 1) * half:(2 * h + 2) * half],
                             (((1,), (1,)), ((), ())), preferred_element_type=F32)
        _sorted_topk_rows(s1, a_ref, nrank)
        _sorted_topk_rows(s2, b_ref, nrank)
        a = a_ref[0:nrank, :]
        b = b_ref[0:nrank, :]
        cand_ref[...] = jnp.full((ncand_rows, tb), NEG_INF, F32)
        off = 0
        for i in range(nrank):
            n_i = nrank // (i + 1)
            cand_ref[off:off + n_i, :] = a[i:i + 1, :] + b[0:n_i, :]
            off += n_i
        cur = cand_ref[...]
        top = a[0:1, :] + b[0:1, :]
        z = jnp.zeros((1, tb), F32)
        kth = top
        for r in range(topk):
            kth = jnp.max(cur, axis=0, keepdims=True)
            z = z + jnp.exp(kth - top)
            cur = jnp.where(cur == kth, NEG_INF, cur)
        nxt = jnp.max(cur, axis=0, keepdims=True)
        tau = 0.5 * (kth + nxt)
        thr_ref[h] = tau - s1
        e1_ref[h] = jnp.exp(s1 - a[0:1, :])
        s2_ref[h] = s2
        e2_ref[h] = jnp.exp(s2 - b[0:1, :]) / z


def _route(h, wq, keys2, nh, tb=256):
    t, dm = h.shape
    nk, half = keys2.shape[1], keys2.shape[2]
    topk = PEER_TOPK
    nrank = topk + 1
    ncand = sum(nrank // (i + 1) for i in range(nrank))
    ncand_rows = -(-ncand // SUBLANES) * SUBLANES
    rank_rows = -(-nrank // SUBLANES) * SUBLANES
    assert t % tb == 0 and nk > nrank
    kern = functools.partial(_route_kernel, nh=nh, nk=nk, half=half, topk=topk, ncand_rows=ncand_rows)
    out = jax.ShapeDtypeStruct((nh, nk, t), F32)
    ospec = pl.BlockSpec((nh, nk, tb), lambda i: (0, 0, i))
    return pl.pallas_call(
        kern,
        grid=(t // tb,),
        in_specs=[pl.BlockSpec((tb, dm), lambda i: (i, 0)),
                  _const_spec(wq.shape), _const_spec(keys2.shape)],
        out_specs=[ospec] * 4,
        out_shape=[out] * 4,
        scratch_shapes=[pltpu.VMEM((rank_rows, tb), F32), pltpu.VMEM((rank_rows, tb), F32),
                        pltpu.VMEM((ncand_rows, tb), F32)],
        compiler_params=_cparams("parallel"),
        name="peer_route",
    )(h, wq, keys2)


_GELU_K1 = -2.0 * math.sqrt(2.0 / math.pi) * math.log2(math.e)
_GELU_K2 = _GELU_K1 * 0.044715


def _gelu_tanh(x):
    return x / (1.0 + jnp.exp2(x * (_GELU_K1 + _GELU_K2 * (x * x))))


def _expert_kernel(xt_ref, u_ref, vt_ref, thr_ref, e1_ref, s2_ref, e2_ref, o_ref, sa_ref, sb_ref,
                   *, nh, nk, rows, sub):
    e = pl.program_id(1)

    @pl.when(e == 0)
    def _():
        o_ref[...] = jnp.zeros_like(o_ref)
        sa_ref[...] = jnp.zeros_like(sa_ref)

    def step(rd_ref, wr_ref):
        for q in range(xt_ref.shape[1] // sub):
            cs = slice(q * sub, (q + 1) * sub)
            wr_ref[:, cs] = jnp.dot(u_ref[...], xt_ref[:, cs], preferred_element_type=F32)
            parts = []
            for r in range(rows):
                g = None
                for h in range(nh):
                    sel = jnp.where(s2_ref[h, :, cs] >= thr_ref[h, r:r + 1, cs],
                                    e2_ref[h, :, cs] * e1_ref[h, r:r + 1, cs], 0.0)
                    g = sel if g is None else g + sel
                parts.append((_gelu_tanh(rd_ref[r * nk:(r + 1) * nk, cs]) * g).astype(BF16))
            act = jnp.concatenate(parts, axis=0)
            o_ref[:, cs] += jnp.dot(vt_ref[...], act, preferred_element_type=F32)

    @pl.when(lax.rem(e, 2) == 0)
    def _():
        step(sa_ref, sb_ref)

    @pl.when(lax.rem(e, 2) == 1)
    def _():
        step(sb_ref, sa_ref)


def _experts(xt, u_tab, vt_tab, thr, e1, s2, e2, tb=1024, rows=8, sub=256):
    dm, t = xt.shape
    ne = u_tab.shape[0]
    nh, nk, _ = thr.shape
    neb = rows * nk
    tb = min(tb, t)
    assert t % tb == 0 and tb % sub == 0 and ne % neb == 0 and ne == nk * nk
    nblk = ne // neb
    kern = functools.partial(_expert_kernel, nh=nh, nk=nk, rows=rows, sub=sub)
    once = pl.Buffered(1)
    score_blk = lambda e: jnp.minimum(e, nblk - 1)
    gate_blk = lambda e: jnp.maximum(e - 1, 0)
    return pl.pallas_call(
        kern,
        grid=(t // tb, nblk + 1),
        in_specs=[pl.BlockSpec((dm, tb), lambda i, e: (0, i), pipeline_mode=once),
                  pl.BlockSpec((neb, dm), lambda i, e: (score_blk(e), 0)),
                  pl.BlockSpec((dm, neb), lambda i, e: (0, gate_blk(e))),
                  pl.BlockSpec((nh, rows, tb), lambda i, e: (0, gate_blk(e), i)),
                  pl.BlockSpec((nh, rows, tb), lambda i, e: (0, gate_blk(e), i)),
                  pl.BlockSpec((nh, nk, tb), lambda i, e: (0, 0, i), pipeline_mode=once),
                  pl.BlockSpec((nh, nk, tb), lambda i, e: (0, 0, i), pipeline_mode=once)],
        out_specs=pl.BlockSpec((dm, tb), lambda i, e: (0, i), pipeline_mode=once),
        out_shape=jax.ShapeDtypeStruct((dm, t), F32),
        scratch_shapes=[pltpu.VMEM((neb, tb), F32), pltpu.VMEM((neb, tb), F32)],
        compiler_params=_cparams("parallel", "arbitrary"),
        name="peer_experts",
    )(xt, u_tab, vt_tab, thr, e1, s2, e2)


def _final_kernel(h_ref, yt_ref, p_ref, lg_ref, lb_ref, wg_ref, wp_ref, o_ref, *, alpha):
    h = _layer_norm(alpha * h_ref[...] + yt_ref[...].T, lg_ref[...], lb_ref[...])
    gate = jax.nn.sigmoid(jnp.dot(h.astype(BF16), wg_ref[...], preferred_element_type=F32))
    proj = jnp.dot(p_ref[...].astype(BF16), wp_ref[...], preferred_element_type=F32)
    o_ref[...] = h + gate * proj


def _final(h, yt, p, ln_g, ln_b, w_gate, w_proj, alpha, tm=512):
    t, dm = h.shape
    pd = p.shape[1]
    assert t % tm == 0
    return pl.pallas_call(
        functools.partial(_final_kernel, alpha=alpha),
        grid=(t // tm,),
        in_specs=[pl.BlockSpec((tm, dm), lambda i: (i, 0)),
                  pl.BlockSpec((dm, tm), lambda i: (0, i)),
                  pl.BlockSpec((tm, pd), lambda i: (i, 0)),
                  _const_spec((1, dm)), _const_spec((1, dm)),
                  _const_spec((dm, dm)), _const_spec((pd, dm))],
        out_specs=pl.BlockSpec((tm, dm), lambda i: (i, 0)),
        out_shape=jax.ShapeDtypeStruct((t, dm), F32),
        compiler_params=_cparams("parallel"),
        name="ln2_ple",
    )(h, yt, p, ln_g, ln_b, w_gate, w_proj)


def _layer(h, p, w_in, b_igate, b_fgate, conv_qk, mh_norm_w, a_re, a_im, log_dt, b_re, b_im, c_re, c_im, d_skip,
           w_glu, b_glu, w_up_ssm, w_up_ml, w_out, ln1_g, ln1_b, peer_wq, peer_keys, peer_u, peer_v,
           ln2_g, ln2_b, ple_w_gate, ple_w_proj, alpha):
    bsz, s, dm = h.shape
    t = bsz * s
    nh = b_igate.shape[0]
    ssm_w = d_skip.shape[0]
    ml_w = mh_norm_w.shape[0]
    x2 = h.reshape(t, dm)
    xb = x2.astype(BF16)

    o0 = ssm_w
    o1 = o0 + 2 * ml_w
    o2 = o1 + ml_w
    o3 = o2 + ml_w
    o4 = o3 + 2 * nh
    u, qk, v, osig, gates = _in_proj(
        xb, [w_in[:, :o3].astype(BF16), w_in[:, o4:].astype(BF16)],
        [(0, ssm_w, F32, False), (0, 2 * ml_w, F32, False), (0, ml_w, BF16, False), (0, ml_w, BF16, True),
         (1, w_in.shape[1] - o4, BF16, True)])
    w_if = jnp.zeros((dm, LANES), F32).at[:, :2 * nh].set(w_in[:, o3:o4]).astype(BF16)
    gif = _matmul(xb, w_if, F32)

    bb, cb, lam = _s5_params(a_re, a_im, log_dt, b_re, b_im, c_re, c_im)
    y_pre = _s5_scan(u.reshape(bsz, s, ssm_w), bb, cb, lam)

    gcol = gif.reshape(bsz, s, LANES)
    grow = jnp.swapaxes(gcol[:, :, :2 * nh], 1, 2)
    bias = jnp.concatenate([b_igate, b_fgate])
    bias_c = jnp.zeros((1, LANES), F32).at[0, :2 * nh].set(bias)
    y_b = _mlstm(qk.reshape(bsz, s, 2 * ml_w), v.reshape(bsz, s, ml_w), osig.reshape(bsz, s, ml_w),
                 gcol, grow, bias_c, bias[:, None], conv_qk, mh_norm_w[None, :], nh)

    h1, h1t = _mix(y_pre.reshape(t, ssm_w), u, y_b.reshape(t, ml_w), gates, x2, d_skip[None, :],
                   w_glu.astype(BF16), b_glu[None, :], w_up_ssm.astype(BF16), w_up_ml.astype(BF16),
                   w_out.astype(BF16), ln1_g[None, :], ln1_b[None, :], alpha)

    ph, _, nk, half = peer_keys.shape
    keys2 = peer_keys.reshape(ph * 2, nk, half).astype(BF16)
    thr, e1, s2, e2 = _route(h1, peer_wq.astype(BF16), keys2, ph)
    y2t = _experts(h1t, peer_u.astype(BF16), peer_v.T.astype(BF16), thr, e1, s2, e2)

    out = _final(h1, y2t, p.reshape(t, -1), ln2_g[None, :], ln2_b[None, :],
                 ple_w_gate.astype(BF16), ple_w_proj.astype(BF16), alpha)
    return out.reshape(bsz, s, dm)


def kernel(x, p, w_in, b_igate, b_fgate, conv_qk, mh_norm_w, ssm_a_re, ssm_a_im, ssm_log_dt, ssm_b_re, ssm_b_im,
           ssm_c_re, ssm_c_im, ssm_d, w_glu, b_glu, w_up_ssm, w_up_ml, w_out, ln1_g, ln1_b, peer_wq, peer_keys,
           peer_u, peer_v, ln2_g, ln2_b, ple_w_gate, ple_w_proj):
    depth = w_in.shape[0]
    alpha = (2 * depth) ** 0.25
    h = x
    for i in range(depth):
        h = _layer(h, p[i], w_in[i], b_igate[i], b_fgate[i], conv_qk[i], mh_norm_w[i], ssm_a_re[i], ssm_a_im[i],
                   ssm_log_dt[i], ssm_b_re[i], ssm_b_im[i], ssm_c_re[i], ssm_c_im[i], ssm_d[i], w_glu[i], b_glu[i],
                   w_up_ssm[i], w_up_ml[i], w_out[i], ln1_g[i], ln1_b[i], peer_wq[i], peer_keys[i], peer_u[i],
                   peer_v[i], ln2_g[i], ln2_b[i], ple_w_gate[i], ple_w_proj[i], alpha)
    return h
```

```python
import functools
import math

import jax
import jax.numpy as jnp
from jax import lax
from jax.experimental import pallas as pl
from jax.experimental.pallas import tpu as pltpu

F32 = jnp.float32
BF16 = jnp.bfloat16

LN_EPS = 1e-5
PEER_TOPK = 16
LANES = 128
SUBLANES = 8
VMEM_LIMIT = 56 * 1024 * 1024

ML_CHUNK = 128
ML_BATCH = 4
S5_CHUNK = 128
S5_PITCH = S5_CHUNK + SUBLANES
NEG_INF = float("-inf")


def _cparams(*sem):
    return pltpu.CompilerParams(dimension_semantics=sem, vmem_limit_bytes=VMEM_LIMIT)


def _const_spec(shape):
    nd = len(shape)
    return pl.BlockSpec(shape, lambda *_: (0,) * nd, pipeline_mode=pl.Buffered(1))


def _mm_kernel(a_ref, w_ref, o_ref):
    o_ref[...] = jnp.dot(a_ref[...], w_ref[...], preferred_element_type=F32).astype(o_ref.dtype)


def _matmul(a, w, out_dtype, tm=1024, tn=512):
    m, k = a.shape
    n = w.shape[1]
    tm, tn = min(tm, m), min(tn, n)
    assert m % tm == 0 and n % tn == 0
    return pl.pallas_call(
        _mm_kernel,
        grid=(m // tm, n // tn),
        in_specs=[pl.BlockSpec((tm, k), lambda i, j: (i, 0)),
                  pl.BlockSpec((k, tn), lambda i, j: (0, j))],
        out_specs=pl.BlockSpec((tm, tn), lambda i, j: (i, j)),
        out_shape=jax.ShapeDtypeStruct((m, n), out_dtype),
        compiler_params=_cparams("parallel", "parallel"),
        name="gate_proj",
    )(a, w)


def _in_proj_kernel(a_ref, *refs, segs, nw, sub):
    j = pl.program_id(1)
    w_refs, o_refs = refs[:nw], refs[nw:]
    for o_ref, (lo, hi, wi, sig) in zip(o_refs, segs):
        @pl.when((j >= lo) & (j < hi))
        def _(o_ref=o_ref, w_ref=w_refs[wi], sig=sig):
            for q in range(o_ref.shape[1] // sub):
                cs = slice(q * sub, (q + 1) * sub)
                acc = jnp.dot(a_ref[...], w_ref[:, cs], preferred_element_type=F32)
                o_ref[:, cs] = (jax.nn.sigmoid(acc) if sig else acc).astype(o_ref.dtype)


def _in_proj(a, weights, outs, tm=1024, tn=1024, sub=256):
    m, k = a.shape
    tm = min(tm, m)
    assert m % tm == 0 and all(o[1] % tn == 0 for o in outs)
    segs, ospecs, shapes, step = [], [], [], 0
    wlo = [None] * len(weights)
    wnb = [0] * len(weights)
    for wi, width, dtype, sig in outs:
        nblk = width // tn
        if wlo[wi] is None:
            wlo[wi] = step
        assert wlo[wi] + wnb[wi] == step, "outputs sharing a weight must be adjacent"
        wnb[wi] += nblk
        segs.append((step, step + nblk, wi, sig))
        ospecs.append(pl.BlockSpec((tm, tn), lambda i, j, lo=step, nblk=nblk: (i, jnp.clip(j - lo, 0, nblk - 1))))
        shapes.append(jax.ShapeDtypeStruct((m, width), dtype))
        step += nblk
    assert all(w.shape == (k, nb * tn) for w, nb in zip(weights, wnb))
    wspecs = [pl.BlockSpec((k, tn), lambda i, j, lo=lo, nb=nb: (0, jnp.clip(j - lo, 0, nb - 1)))
              for lo, nb in zip(wlo, wnb)]
    return pl.pallas_call(
        functools.partial(_in_proj_kernel, segs=tuple(segs), nw=len(weights), sub=sub),
        grid=(m // tm, step),
        in_specs=[pl.BlockSpec((tm, k), lambda i, j: (i, 0))] + wspecs,
        out_specs=ospecs,
        out_shape=shapes,
        compiler_params=_cparams("parallel", "arbitrary"),
        name="in_proj",
    )(a, *weights)


def _s5_kernel(u_ref, bb_ref, cb_ref, lam_ref, y_ref, bu_ref, st_ref, *, bsz, chunk, pitch, ntile):
    c = pl.program_id(1)

    @pl.when(c == 0)
    def _():
        st_ref[...] = jnp.zeros_like(st_ref)

    bb = bb_ref[0]
    for b in range(bsz):
        bu = jnp.dot(u_ref[b].astype(BF16), bb, preferred_element_type=F32)
        for k in range(2 * ntile):
            bu_ref[k, b * pitch:b * pitch + chunk, :] = bu[:, k * LANES:(k + 1) * LANES]

    lam = lam_ref[0]
    lr = [jnp.broadcast_to(lam[k:k + 1, :], (bsz, LANES)) for k in range(ntile)]
    li = [jnp.broadcast_to(lam[ntile + k:ntile + k + 1, :], (bsz, LANES)) for k in range(ntile)]

    def step(t, carry):
        new = []
        for k in range(ntile):
            sr, si = carry[2 * k], carry[2 * k + 1]
            rows = pl.ds(t, bsz, stride=pitch)
            nr = lr[k] * sr - li[k] * si + bu_ref[k, rows, :]
            ni = lr[k] * si + li[k] * sr + bu_ref[ntile + k, rows, :]
            bu_ref[k, rows, :] = nr
            bu_ref[ntile + k, rows, :] = ni
            new += [nr, ni]
        return tuple(new)

    init = []
    for k in range(ntile):
        init += [st_ref[k], st_ref[ntile + k]]
    fin = lax.fori_loop(0, chunk, step, tuple(init), unroll=2)
    for k in range(ntile):
        st_ref[k] = fin[2 * k]
        st_ref[ntile + k] = fin[2 * k + 1]

    cb = cb_ref[0]
    for b in range(bsz):
        st = jnp.concatenate([bu_ref[k, b * pitch:b * pitch + chunk, :] for k in range(2 * ntile)], axis=1)
        y_ref[b] = jnp.dot(st.astype(BF16), cb, preferred_element_type=F32)


def _s5_scan(u, bb, cb, lam):
    bsz, s, w = u.shape
    nslab = w // LANES
    ntile = lam.shape[1] // 2
    chunk, pitch = S5_CHUNK, S5_PITCH
    assert s % chunk == 0 and bsz <= SUBLANES
    kern = functools.partial(_s5_kernel, bsz=bsz, chunk=chunk, pitch=pitch, ntile=ntile)
    return pl.pallas_call(
        kern,
        grid=(nslab, s // chunk),
        in_specs=[pl.BlockSpec((bsz, chunk, LANES), lambda j, c: (0, c, j)),
                  pl.BlockSpec((1, LANES, 2 * ntile * LANES), lambda j, c: (j, 0, 0)),
                  pl.BlockSpec((1, 2 * ntile * LANES, LANES), lambda j, c: (j, 0, 0)),
                  pl.BlockSpec((1, 2 * ntile, LANES), lambda j, c: (j, 0, 0))],
        out_specs=pl.BlockSpec((bsz, chunk, LANES), lambda j, c: (0, c, j)),
        out_shape=jax.ShapeDtypeStruct((bsz, s, w), F32),
        scratch_shapes=[pltpu.VMEM((2 * ntile, bsz * pitch, LANES), F32),
                        pltpu.VMEM((2 * ntile, bsz, LANES), F32)],
        compiler_params=_cparams("parallel", "arbitrary"),
        name="s5_scan",
    )(u, bb, cb, lam)


def _s5_params(a_re, a_im, log_dt, b_re, b_im, c_re, c_im):
    g, p = a_re.shape
    gc = b_re.shape[2]
    gps = LANES // gc
    nslab = g // gps
    dt = jnp.exp(log_dt)[:, None]
    mag = jnp.exp(dt * a_re)
    lam_r = mag * jnp.cos(dt * a_im)
    lam_i = mag * jnp.sin(dt * a_im)
    den = a_re * a_re + a_im * a_im
    zr = ((lam_r - 1.0) * a_re + lam_i * a_im) / den
    zi = (lam_i * a_re - (lam_r - 1.0) * a_im) / den
    bbar_r = zr[..., None] * b_re - zi[..., None] * b_im
    bbar_i = zr[..., None] * b_im + zi[..., None] * b_re
    eye = jnp.eye(gps, dtype=F32)

    def in_slab(bm):
        bm = bm.reshape(nslab, gps, p, gc)
        return jnp.einsum("jgpc,gh->jgchp", bm, eye).reshape(nslab, gps * gc, gps * p)

    def out_slab(cm):
        cm = cm.reshape(nslab, gps, gc, p)
        return jnp.einsum("jgcp,gh->jhpgc", cm, eye).reshape(nslab, gps * p, gps * gc)

    bb = jnp.concatenate([in_slab(bbar_r), in_slab(bbar_i)], axis=2).astype(BF16)
    cb = jnp.concatenate([out_slab(c_re), -out_slab(c_im)], axis=1).astype(BF16)
    ntile = gps * p // LANES
    lam = jnp.concatenate([lam_r.reshape(nslab, ntile, LANES), lam_i.reshape(nslab, ntile, LANES)], axis=1)
    return bb, cb, lam


def _log_sigmoid(x):
    return jnp.minimum(x, 0.0) - jnp.log1p(jnp.exp(-jnp.abs(x)))


def _mlstm_kernel(qk_ref, v_ref, o_ref, gc_ref, gr_ref, bc_ref, br_ref, cw_ref, nw_ref, y_ref,
                  buf_ref, qs_ref, c_ref, n_ref, m_ref, *, nb, nh, dh, chunk, kconv):
    ci = pl.program_id(1)
    width = nh * dh
    halo = SUBLANES

    @pl.when(ci == 0)
    def _():
        buf_ref[:, 0:halo, :] = jnp.zeros((nb, halo, 2 * width), F32)
        c_ref[...] = jnp.zeros_like(c_ref)
        n_ref[...] = jnp.zeros_like(n_ref)
        m_ref[...] = jnp.zeros_like(m_ref)

    tt = lax.broadcasted_iota(jnp.int32, (chunk, chunk), 0)
    ss = lax.broadcasted_iota(jnp.int32, (chunk, chunk), 1)
    causal = ss <= tt
    tril = causal.astype(F32)
    triu = (tt <= ss).astype(F32)

    gates = []
    for bi in range(nb):
        buf_ref[bi, halo:halo + chunk, :] = qk_ref[bi]
        base = halo - (kconv - 1)
        conv = cw_ref[0:1, :] * buf_ref[bi, base:base + chunk, :]
        for j in range(1, kconv):
            conv = conv + cw_ref[j:j + 1, :] * buf_ref[bi, base + j:base + j + chunk, :]
        buf_ref[bi, 0:halo, :] = buf_ref[bi, chunk:chunk + halo, :]
        qs_ref[bi] = conv * jax.nn.sigmoid(conv)

        gcol = gc_ref[bi] + bc_ref[...]
        col_id = lax.broadcasted_iota(jnp.int32, gcol.shape, 1)
        lcol = jnp.where(col_id >= nh, _log_sigmoid(gcol), gcol)
        grow = gr_ref[bi] + br_ref[...]
        row_id = lax.broadcasted_iota(jnp.int32, grow.shape, 0)
        lrow = jnp.where(row_id >= nh, _log_sigmoid(grow), grow)
        bcol = jnp.dot(tril, lcol, preferred_element_type=F32, precision=lax.Precision.HIGHEST)
        brow = jnp.dot(lrow, triu, preferred_element_type=F32, precision=lax.Precision.HIGHEST)
        gates.append((lcol, lrow, bcol, brow))

    inv_sqrt = 1.0 / math.sqrt(dh)
    for h in range(nh):
        hs = slice(h * dh, (h + 1) * dh)
        ks = slice(width + h * dh, width + (h + 1) * dh)
        bs = range(nb)
        st = [bi * nh + h for bi in bs]
        qf = [qs_ref[bi, :, hs] for bi in bs]
        q = [x.astype(BF16) for x in qf]
        kf = [qs_ref[bi, :, ks] * inv_sqrt for bi in bs]
        k = [x.astype(BF16) for x in kf]
        v = [v_ref[bi, :, hs] for bi in bs]
        b_c = [gates[bi][2][:, nh + h:nh + h + 1] for bi in bs]
        i_c = [gates[bi][0][:, h:h + 1] for bi in bs]
        b_r = [gates[bi][3][nh + h:nh + h + 1, :] for bi in bs]
        i_r = [gates[bi][1][h:h + 1, :] for bi in bs]
        g_tot = [x[chunk - 1:chunk, :] for x in b_c]
        m_prev = [m_ref[s_][:, 0:1] for s_ in st]
        c_prev = [c_ref[s_] for s_ in st]
        n_prev = [n_ref[s_] for s_ in st]

        qk_t = [lax.dot_general(q[bi], k[bi], (((1,), (1,)), ((), ())), preferred_element_type=F32) for bi in bs]
        q_c = [jnp.dot(q[bi], c_prev[bi].astype(BF16), preferred_element_type=F32) for bi in bs]
        dmat = [jnp.where(causal, b_c[bi] - b_r[bi] + i_r[bi], NEG_INF) for bi in bs]
        inter_log = [b_c[bi] + m_prev[bi] for bi in bs]
        m_row = [jnp.maximum(inter_log[bi], jnp.max(dmat[bi], axis=1, keepdims=True)) for bi in bs]
        sc = [qk_t[bi] * jnp.exp(dmat[bi] - m_row[bi]) for bi in bs]
        inter_scale = [jnp.exp(inter_log[bi] - m_row[bi]) for bi in bs]
        sc_v = [jnp.dot(sc[bi].astype(BF16), v[bi], preferred_element_type=F32) for bi in bs]

        wlog = [g_tot[bi] - b_c[bi] + i_c[bi] for bi in bs]
        m_loc = [jnp.max(x, axis=0, keepdims=True) for x in wlog]
        kw = [kf[bi] * jnp.exp(wlog[bi] - m_loc[bi]) for bi in bs]
        c_chunk = [lax.dot_general(kw[bi].astype(BF16), v[bi], (((0,), (0,)), ((), ())),
                                   preferred_element_type=F32) for bi in bs]

        num = [sc_v[bi] + inter_scale[bi] * q_c[bi] for bi in bs]
        qn = [jnp.sum(qf[bi] * n_prev[bi], axis=1, keepdims=True) for bi in bs]
        den = [jnp.sum(sc[bi], axis=1, keepdims=True) + inter_scale[bi] * qn[bi] for bi in bs]
        hh = [num[bi] / jnp.maximum(jnp.abs(den[bi]), jnp.exp(-m_row[bi])) for bi in bs]
        mu = [jnp.mean(x, axis=1, keepdims=True) for x in hh]
        var = [jnp.mean(jnp.square(hh[bi] - mu[bi]), axis=1, keepdims=True) for bi in bs]
        for bi in bs:
            hn = (hh[bi] - mu[bi]) * lax.rsqrt(var[bi] + LN_EPS) * nw_ref[:, hs]
            y_ref[bi, :, hs] = (o_ref[bi, :, hs].astype(F32) * hn).astype(y_ref.dtype)

        for bi in bs:
            n_chunk = jnp.sum(kw[bi], axis=0, keepdims=True)
            m_new = jnp.maximum(g_tot[bi] + m_prev[bi], m_loc[bi])
            a = jnp.exp(g_tot[bi] + m_prev[bi] - m_new)
            bb = jnp.exp(m_loc[bi] - m_new)
            c_ref[st[bi]] = a * c_prev[bi] + bb * c_chunk[bi]
            n_ref[st[bi]] = a * n_prev[bi] + bb * n_chunk
            m_ref[st[bi]] = jnp.broadcast_to(m_new, (1, LANES))


def _mlstm(qk, v, osig, gcol, grow, bias_c, bias_r, conv_w, norm_w, nh, nb=ML_BATCH):
    bsz, s, w2 = qk.shape
    width = w2 // 2
    dh = width // nh
    chunk = ML_CHUNK
    kconv = conv_w.shape[0]
    nb = math.gcd(nb, bsz)
    assert s % chunk == 0 and kconv - 1 <= SUBLANES
    kern = functools.partial(_mlstm_kernel, nb=nb, nh=nh, dh=dh, chunk=chunk, kconv=kconv)
    return pl.pallas_call(
        kern,
        grid=(bsz // nb, s // chunk),
        in_specs=[pl.BlockSpec((nb, chunk, w2), lambda b, c: (b, c, 0)),
                  pl.BlockSpec((nb, chunk, width), lambda b, c: (b, c, 0)),
                  pl.BlockSpec((nb, chunk, width), lambda b, c: (b, c, 0)),
                  pl.BlockSpec((nb, chunk, LANES), lambda b, c: (b, c, 0)),
                  pl.BlockSpec((nb, 2 * nh, chunk), lambda b, c: (b, 0, c)),
                  _const_spec((1, LANES)),
                  _const_spec((2 * nh, 1)),
                  _const_spec((kconv, w2)),
                  _const_spec((1, width))],
        out_specs=pl.BlockSpec((nb, chunk, width), lambda b, c: (b, c, 0)),
        out_shape=jax.ShapeDtypeStruct((bsz, s, width), BF16),
        scratch_shapes=[pltpu.VMEM((nb, SUBLANES + chunk, w2), F32),
                        pltpu.VMEM((nb, chunk, w2), F32),
                        pltpu.VMEM((nb * nh, dh, dh), F32),
                        pltpu.VMEM((nb * nh, 1, dh), F32),
                        pltpu.VMEM((nb * nh, 1, LANES), F32)],
        compiler_params=_cparams("parallel", "arbitrary"),
        name="mlstm",
    )(qk, v, osig, gcol, grow, bias_c, bias_r, conv_w, norm_w)


def _layer_norm(x, g, b):
    mu = jnp.mean(x, axis=-1, keepdims=True)
    var = jnp.mean(jnp.square(x - mu), axis=-1, keepdims=True)
    return (x - mu) * lax.rsqrt(var + LN_EPS) * g + b


def _mix_kernel(yp_ref, u_ref, yb_ref, ga_ref, gb_ref, x_ref, d_ref, wg_ref, bg_ref, wa_ref, wb_ref, wo_ref,
                lg_ref, lb_ref, h_ref, ht_ref, *, alpha):
    y = jax.nn.gelu(yp_ref[...] + d_ref[...] * u_ref[...])
    gate = jax.nn.sigmoid(jnp.dot(y.astype(BF16), wg_ref[...], preferred_element_type=F32) + bg_ref[...])
    ya = (y * gate).astype(BF16)
    merged = (ga_ref[...].astype(F32) * jnp.dot(ya, wa_ref[...], preferred_element_type=F32)
              + gb_ref[...].astype(F32) * jnp.dot(yb_ref[...], wb_ref[...], preferred_element_type=F32))
    mix = jnp.dot(merged.astype(BF16), wo_ref[...], preferred_element_type=F32)
    h = _layer_norm(alpha * x_ref[...] + mix, lg_ref[...], lb_ref[...])
    h_ref[...] = h
    ht_ref[...] = h.T.astype(BF16)


def _mix(y_pre, u, y_b, gates, x, d, w_glu, b_glu, w_a, w_b, w_o, ln_g, ln_b, alpha, tm=256):
    t, dm = x.shape
    w = u.shape[1]
    assert t % tm == 0
    row = lambda width: pl.BlockSpec((tm, width), lambda i: (i, 0))
    return pl.pallas_call(
        functools.partial(_mix_kernel, alpha=alpha),
        grid=(t // tm,),
        in_specs=[row(w), row(w), row(w),
                  pl.BlockSpec((tm, dm), lambda i: (i, 0)),
                  pl.BlockSpec((tm, dm), lambda i: (i, 1)),
                  row(dm),
                  _const_spec((1, w)), _const_spec((w, w)), _const_spec((1, w)),
                  _const_spec((w, dm)), _const_spec((w, dm)), _const_spec((dm, dm)),
                  _const_spec((1, dm)), _const_spec((1, dm))],
        out_specs=[pl.BlockSpec((tm, dm), lambda i: (i, 0)),
                   pl.BlockSpec((dm, tm), lambda i: (0, i))],
        out_shape=[jax.ShapeDtypeStruct((t, dm), F32), jax.ShapeDtypeStruct((dm, t), BF16)],
        compiler_params=_cparams("parallel"),
        name="mix_ln1",
    )(y_pre, u, y_b, gates, gates, x, d, w_glu, b_glu, w_a, w_b, w_o, ln_g, ln_b)


def _oddeven_merge(lo, hi, r):
    step = r * 2
    if step < hi - lo:
        yield from _oddeven_merge(lo, hi, step)
        yield from _oddeven_merge(lo + r, hi, step)
        yield from [(i, i + r) for i in range(lo + r, hi - r, step)]
    else:
        yield (lo, lo + r)


def _oddeven_merge_sort(lo, hi):
    if hi - lo >= 1:
        mid = lo + (hi - lo) // 2
        yield from _oddeven_merge_sort(lo, mid)
        yield from _oddeven_merge_sort(mid + 1, hi)
        yield from _oddeven_merge(lo, hi, 1)


def _sorted_topk_rows(s, dst_ref, k):
    g = s.shape[0] // SUBLANES
    w = [s[j * SUBLANES:(j + 1) * SUBLANES, :] for j in range(g)]
    for i, j in _oddeven_merge_sort(0, g - 1):
        w[i], w[j] = jnp.maximum(w[i], w[j]), jnp.minimum(w[i], w[j])
    for r in range(k):
        mx = jnp.max(w[0], axis=0, keepdims=True)
        dst_ref[r:r + 1, :] = mx
        left = k - 1 - r
        if left > 0:
            hit = w[0] == mx
            for lvl in range(min(g, left)):
                below = w[lvl + 1] if lvl + 1 < g else NEG_INF
                w[lvl] = jnp.where(hit, below, w[lvl])


def _route_kernel(h_ref, wq_ref, keys_ref, cnt_ref, e1_ref, rank_ref, e2_ref, a_ref, b_ref, cand_ref,
                  *, nh, nk, half, topk, ncand_rows):
    q = jnp.dot(h_ref[...].astype(BF16), wq_ref[...], preferred_element_type=F32).astype(BF16)
    tb = q.shape[0]
    nrank = topk + 1
    for h in range(nh):
        s1 = lax.dot_general(keys_ref[2 * h], q[:, (2 * h) * half:(2 * h + 1) * half],
                             (((1,), (1,)), ((), ())), preferred_element_type=F32)
        s2 = lax.dot_general(keys_ref[2 * h + 1], q[:, (2 * h + 1) * half:(2 * h + 2) * half],
                             (((1,), (1,)), ((), ())), preferred_element_type=F32)
        _sorted_topk_rows(s1, a_ref, nrank)
        _sorted_topk_rows(s2, b_ref, nrank)
        a = a_ref[0:nrank, :]
        b = b_ref[0:nrank, :]
        cand_ref[...] = jnp.full((ncand_rows, tb), NEG_INF, F32)
        off = 0
        for i in range(nrank):
            n_i = nrank // (i + 1)
            cand_ref[off:off + n_i, :] = a[i:i + 1, :] + b[0:n_i, :]
            off += n_i
        cur = cand_ref[...]
        top = a[0:1, :] + b[0:1, :]
        z = jnp.zeros((1, tb), F32)
        kth = top
        for r in range(topk):
            kth = jnp.max(cur, axis=0, keepdims=True)
            z = z + jnp.exp(kth - top)
            cur = jnp.where(cur == kth, NEG_INF, cur)
        nxt = jnp.max(cur, axis=0, keepdims=True)
        tau = 0.5 * (kth + nxt)
        thr = tau - s1
        cnt = jnp.zeros_like(s1)
        for r in range(topk):
            cnt = jnp.where(b[r:r + 1, :] >= thr, float(r + 1), cnt)
        rank = jnp.full_like(s2, float(nrank))
        for r in reversed(range(nrank)):
            rank = jnp.where(s2 >= b[r:r + 1, :], float(r), rank)
        cnt_ref[h] = cnt
        e1_ref[h] = jnp.exp(s1 - a[0:1, :])
        rank_ref[h] = rank.astype(rank_ref.dtype)
        e2_ref[h] = (jnp.exp(s2 - b[0:1, :]) / z).astype(e2_ref.dtype)


def _route(h, wq, keys2, nh, tb=256):
    t, dm = h.shape
    nk, half = keys2.shape[1], keys2.shape[2]
    topk = PEER_TOPK
    nrank = topk + 1
    ncand = sum(nrank // (i + 1) for i in range(nrank))
    ncand_rows = -(-ncand // SUBLANES) * SUBLANES
    rank_rows = -(-nrank // SUBLANES) * SUBLANES
    assert t % tb == 0 and nk > nrank
    kern = functools.partial(_route_kernel, nh=nh, nk=nk, half=half, topk=topk, ncand_rows=ncand_rows)
    ospec = pl.BlockSpec((nh, nk, tb), lambda i: (0, 0, i))
    return pl.pallas_call(
        kern,
        grid=(t // tb,),
        in_specs=[pl.BlockSpec((tb, dm), lambda i: (i, 0)),
                  _const_spec(wq.shape), _const_spec(keys2.shape)],
        out_specs=[ospec] * 4,
        out_shape=[jax.ShapeDtypeStruct((nh, nk, t), dt) for dt in (F32, F32, BF16, BF16)],
        scratch_shapes=[pltpu.VMEM((rank_rows, tb), F32), pltpu.VMEM((rank_rows, tb), F32),
                        pltpu.VMEM((ncand_rows, tb), F32)],
        compiler_params=_cparams("parallel"),
        name="peer_route",
    )(h, wq, keys2)


_GELU_K1 = -2.0 * math.sqrt(2.0 / math.pi) * math.log2(math.e)
_GELU_K2 = _GELU_K1 * 0.044715


def _gelu_tanh(x):
    return x / (1.0 + jnp.exp2(x * (_GELU_K1 + _GELU_K2 * (x * x))))


def _expert_kernel(xt_ref, u_ref, vt_ref, cnt_ref, e1_ref, rank_ref, e2_ref, o_ref, sa_ref, sb_ref,
                   *, nh, nk, rows, sub):
    e = pl.program_id(1)

    @pl.when(e == 0)
    def _():
        o_ref[...] = jnp.zeros_like(o_ref)
        sa_ref[...] = jnp.zeros_like(sa_ref)

    def step(rd_ref, wr_ref):
        for q in range(xt_ref.shape[1] // sub):
            cs = slice(q * sub, (q + 1) * sub)
            wr_ref[:, cs] = jnp.dot(u_ref[...], xt_ref[:, cs], preferred_element_type=F32)
            parts = []
            for r in range(rows):
                g = None
                for h in range(nh):
                    sel = jnp.where(rank_ref[h, :, cs] < cnt_ref[h, r:r + 1, cs].astype(BF16),
                                    e2_ref[h, :, cs] * e1_ref[h, r:r + 1, cs].astype(BF16), 0.0)
                    g = sel if g is None else g + sel
                parts.append(_gelu_tanh(rd_ref[r * nk:(r + 1) * nk, cs]).astype(BF16) * g)
            act = jnp.concatenate(parts, axis=0)
            o_ref[:, cs] += jnp.dot(vt_ref[...], act, preferred_element_type=F32)

    @pl.when(lax.rem(e, 2) == 0)
    def _():
        step(sa_ref, sb_ref)

    @pl.when(lax.rem(e, 2) == 1)
    def _():
        step(sb_ref, sa_ref)


def _experts(xt, u_tab, vt_tab, cnt, e1, rank, e2, tb=1024, rows=8, sub=256):
    dm, t = xt.shape
    ne = u_tab.shape[0]
    nh, nk, _ = cnt.shape
    neb = rows * nk
    tb = min(tb, t)
    assert t % tb == 0 and tb % sub == 0 and ne % neb == 0 and ne == nk * nk
    nblk = ne // neb
    kern = functools.partial(_expert_kernel, nh=nh, nk=nk, rows=rows, sub=sub)
    once = pl.Buffered(1)
    score_blk = lambda e: jnp.minimum(e, nblk - 1)
    gate_blk = lambda e: jnp.maximum(e - 1, 0)
    return pl.pallas_call(
        kern,
        grid=(t // tb, nblk + 1),
        in_specs=[pl.BlockSpec((dm, tb), lambda i, e: (0, i), pipeline_mode=once),
                  pl.BlockSpec((neb, dm), lambda i, e: (score_blk(e), 0)),
                  pl.BlockSpec((dm, neb), lambda i, e: (0, gate_blk(e))),
                  pl.BlockSpec((nh, rows, tb), lambda i, e: (0, gate_blk(e), i)),
                  pl.BlockSpec((nh, rows, tb), lambda i, e: (0, gate_blk(e), i)),
                  pl.BlockSpec((nh, nk, tb), lambda i, e: (0, 0, i), pipeline_mode=once),
                  pl.BlockSpec((nh, nk, tb), lambda i, e: (0, 0, i), pipeline_mode=once)],
        out_specs=pl.BlockSpec((dm, tb), lambda i, e: (0, i), pipeline_mode=once),
        out_shape=jax.ShapeDtypeStruct((dm, t), F32),
        scratch_shapes=[pltpu.VMEM((neb, tb), F32), pltpu.VMEM((neb, tb), F32)],
        compiler_params=_cparams("parallel", "arbitrary"),
        name="peer_experts",
    )(xt, u_tab, vt_tab, cnt, e1, rank, e2)


def _final_kernel(h_ref, yt_ref, p_ref, lg_ref, lb_ref, wg_ref, wp_ref, o_ref, *, alpha):
    h = _layer_norm(alpha * h_ref[...] + yt_ref[...].T, lg_ref[...], lb_ref[...])
    gate = jax.nn.sigmoid(jnp.dot(h.astype(BF16), wg_ref[...], preferred_element_type=F32))
    proj = jnp.dot(p_ref[...].astype(BF16), wp_ref[...], preferred_element_type=F32)
    o_ref[...] = h + gate * proj


def _final(h, yt, p, ln_g, ln_b, w_gate, w_proj, alpha, tm=512):
    t, dm = h.shape
    pd = p.shape[1]
    assert t % tm == 0
    return pl.pallas_call(
        functools.partial(_final_kernel, alpha=alpha),
        grid=(t // tm,),
        in_specs=[pl.BlockSpec((tm, dm), lambda i: (i, 0)),
                  pl.BlockSpec((dm, tm), lambda i: (0, i)),
                  pl.BlockSpec((tm, pd), lambda i: (i, 0)),
                  _const_spec((1, dm)), _const_spec((1, dm)),
                  _const_spec((dm, dm)), _const_spec((pd, dm))],
        out_specs=pl.BlockSpec((tm, dm), lambda i: (i, 0)),
        out_shape=jax.ShapeDtypeStruct((t, dm), F32),
        compiler_params=_cparams("parallel"),
        name="ln2_ple",
    )(h, yt, p, ln_g, ln_b, w_gate, w_proj)


def _layer(h, p, w_in, b_igate, b_fgate, conv_qk, mh_norm_w, a_re, a_im, log_dt, b_re, b_im, c_re, c_im, d_skip,
           w_glu, b_glu, w_up_ssm, w_up_ml, w_out, ln1_g, ln1_b, peer_wq, peer_keys, peer_u, peer_v,
           ln2_g, ln2_b, ple_w_gate, ple_w_proj, alpha):
    bsz, s, dm = h.shape
    t = bsz * s
    nh = b_igate.shape[0]
    ssm_w = d_skip.shape[0]
    ml_w = mh_norm_w.shape[0]
    x2 = h.reshape(t, dm)
    xb = x2.astype(BF16)

    o0 = ssm_w
    o1 = o0 + 2 * ml_w
    o2 = o1 + ml_w
    o3 = o2 + ml_w
    o4 = o3 + 2 * nh
    u, qk, v, osig, gates = _in_proj(
        xb, [w_in[:, :o3].astype(BF16), w_in[:, o4:].astype(BF16)],
        [(0, ssm_w, F32, False), (0, 2 * ml_w, F32, False), (0, ml_w, BF16, False), (0, ml_w, BF16, True),
         (1, w_in.shape[1] - o4, BF16, True)])
    w_if = jnp.zeros((dm, LANES), F32).at[:, :2 * nh].set(w_in[:, o3:o4]).astype(BF16)
    gif = _matmul(xb, w_if, F32)

    bb, cb, lam = _s5_params(a_re, a_im, log_dt, b_re, b_im, c_re, c_im)
    y_pre = _s5_scan(u.reshape(bsz, s, ssm_w), bb, cb, lam)

    gcol = gif.reshape(bsz, s, LANES)
    grow = jnp.swapaxes(gcol[:, :, :2 * nh], 1, 2)
    bias = jnp.concatenate([b_igate, b_fgate])
    bias_c = jnp.zeros((1, LANES), F32).at[0, :2 * nh].set(bias)
    y_b = _mlstm(qk.reshape(bsz, s, 2 * ml_w), v.reshape(bsz, s, ml_w), osig.reshape(bsz, s, ml_w),
                 gcol, grow, bias_c, bias[:, None], conv_qk, mh_norm_w[None, :], nh)

    h1, h1t = _mix(y_pre.reshape(t, ssm_w), u, y_b.reshape(t, ml_w), gates, x2, d_skip[None, :],
                   w_glu.astype(BF16), b_glu[None, :], w_up_ssm.astype(BF16), w_up_ml.astype(BF16),
                   w_out.astype(BF16), ln1_g[None, :], ln1_b[None, :], alpha)

    ph, _, nk, half = peer_keys.shape
    keys2 = peer_keys.reshape(ph * 2, nk, half).astype(BF16)
    cnt, e1, rank, e2 = _route(h1, peer_wq.astype(BF16), keys2, ph)
    y2t = _experts(h1t, peer_u.astype(BF16), peer_v.T.astype(BF16), cnt, e1, rank, e2)

    out = _final(h1, y2t, p.reshape(t, -1), ln2_g[None, :], ln2_b[None, :],
                 ple_w_gate.astype(BF16), ple_w_proj.astype(BF16), alpha)
    return out.reshape(bsz, s, dm)


def kernel(x, p, w_in, b_igate, b_fgate, conv_qk, mh_norm_w, ssm_a_re, ssm_a_im, ssm_log_dt, ssm_b_re, ssm_b_im,
           ssm_c_re, ssm_c_im, ssm_d, w_glu, b_glu, w_up_ssm, w_up_ml, w_out, ln1_g, ln1_b, peer_wq, peer_keys,
           peer_u, peer_v, ln2_g, ln2_b, ple_w_gate, ple_w_proj):
    depth = w_in.shape[0]
    alpha = (2 * depth) ** 0.25
    h = x
    for i in range(depth):
        h = _layer(h, p[i], w_in[i], b_igate[i], b_fgate[i], conv_qk[i], mh_norm_w[i], ssm_a_re[i], ssm_a_im[i],
                   ssm_log_dt[i], ssm_b_re[i], ssm_b_im[i], ssm_c_re[i], ssm_c_im[i], ssm_d[i], w_glu[i], b_glu[i],
                   w_up_ssm[i], w_up_ml[i], w_out[i], ln1_g[i], ln1_b[i], peer_wq[i], peer_keys[i], peer_u[i],
                   peer_v[i], ln2_g[i], ln2_b[i], ple_w_gate[i], ple_w_proj[i], alpha)
    return h
```

```python
import functools
import math

import jax
import jax.numpy as jnp
from jax import lax
from jax.experimental import pallas as pl
from jax.experimental.pallas import tpu as pltpu

F32 = jnp.float32
BF16 = jnp.bfloat16

LN_EPS = 1e-5
PEER_TOPK = 16
LANES = 128
SUBLANES = 8
VMEM_LIMIT = 56 * 1024 * 1024

ML_CHUNK = 128
ML_BATCH = 4
S5_CHUNK = 128
S5_PITCH = S5_CHUNK + SUBLANES
NEG_INF = float("-inf")


def _cparams(*sem):
    return pltpu.CompilerParams(dimension_semantics=sem, vmem_limit_bytes=VMEM_LIMIT)


def _const_spec(shape):
    nd = len(shape)
    return pl.BlockSpec(shape, lambda *_: (0,) * nd, pipeline_mode=pl.Buffered(1))


def _mm_kernel(a_ref, w_ref, o_ref):
    o_ref[...] = jnp.dot(a_ref[...], w_ref[...], preferred_element_type=F32).astype(o_ref.dtype)


def _matmul(a, w, out_dtype, tm=1024, tn=512):
    m, k = a.shape
    n = w.shape[1]
    tm, tn = min(tm, m), min(tn, n)
    assert m % tm == 0 and n % tn == 0
    return pl.pallas_call(
        _mm_kernel,
        grid=(m // tm, n // tn),
        in_specs=[pl.BlockSpec((tm, k), lambda i, j: (i, 0)),
                  pl.BlockSpec((k, tn), lambda i, j: (0, j))],
        out_specs=pl.BlockSpec((tm, tn), lambda i, j: (i, j)),
        out_shape=jax.ShapeDtypeStruct((m, n), out_dtype),
        compiler_params=_cparams("parallel", "parallel"),
        name="gate_proj",
    )(a, w)


def _in_proj_kernel(a_ref, *refs, segs, nw, sub):
    j = pl.program_id(1)
    w_refs, o_refs = refs[:nw], refs[nw:]
    for o_ref, (lo, hi, wi, sig) in zip(o_refs, segs):
        @pl.when((j >= lo) & (j < hi))
        def _(o_ref=o_ref, w_ref=w_refs[wi], sig=sig):
            for q in range(o_ref.shape[1] // sub):
                cs = slice(q * sub, (q + 1) * sub)
                acc = jnp.dot(a_ref[...], w_ref[:, cs], preferred_element_type=F32)
                o_ref[:, cs] = (jax.nn.sigmoid(acc) if sig else acc).astype(o_ref.dtype)


def _in_proj(a, weights, outs, tm=1024, tn=1024, sub=256):
    m, k = a.shape
    tm = min(tm, m)
    assert m % tm == 0 and all(o[1] % tn == 0 for o in outs)
    segs, ospecs, shapes, step = [], [], [], 0
    wlo = [None] * len(weights)
    wnb = [0] * len(weights)
    for wi, width, dtype, sig in outs:
        nblk = width // tn
        if wlo[wi] is None:
            wlo[wi] = step
        assert wlo[wi] + wnb[wi] == step, "outputs sharing a weight must be adjacent"
        wnb[wi] += nblk
        segs.append((step, step + nblk, wi, sig))
        ospecs.append(pl.BlockSpec((tm, tn), lambda i, j, lo=step, nblk=nblk: (i, jnp.clip(j - lo, 0, nblk - 1))))
        shapes.append(jax.ShapeDtypeStruct((m, width), dtype))
        step += nblk
    assert all(w.shape == (k, nb * tn) for w, nb in zip(weights, wnb))
    wspecs = [pl.BlockSpec((k, tn), lambda i, j, lo=lo, nb=nb: (0, jnp.clip(j - lo, 0, nb - 1)))
              for lo, nb in zip(wlo, wnb)]
    return pl.pallas_call(
        functools.partial(_in_proj_kernel, segs=tuple(segs), nw=len(weights), sub=sub),
        grid=(m // tm, step),
        in_specs=[pl.BlockSpec((tm, k), lambda i, j: (i, 0))] + wspecs,
        out_specs=ospecs,
        out_shape=shapes,
        compiler_params=_cparams("parallel", "arbitrary"),
        name="in_proj",
    )(a, *weights)


def _s5_kernel(u_ref, bb_ref, cb_ref, lam_ref, y_ref, bu_ref, st_ref, *, bsz, chunk, pitch, ntile):
    c = pl.program_id(1)

    @pl.when(c == 0)
    def _():
        st_ref[...] = jnp.zeros_like(st_ref)

    ub = jnp.concatenate([u_ref[b].astype(BF16) for b in range(bsz)], axis=0)
    bu = jnp.dot(ub, bb_ref[0], preferred_element_type=F32)
    for b in range(bsz):
        for k in range(2 * ntile):
            bu_ref[k, b * pitch:b * pitch + chunk, :] = bu[b * chunk:(b + 1) * chunk, k * LANES:(k + 1) * LANES]

    lam = lam_ref[0]
    lr = [jnp.broadcast_to(lam[k:k + 1, :], (bsz, LANES)) for k in range(ntile)]
    li = [jnp.broadcast_to(lam[ntile + k:ntile + k + 1, :], (bsz, LANES)) for k in range(ntile)]

    def step(t, carry):
        new = []
        for k in range(ntile):
            sr, si = carry[2 * k], carry[2 * k + 1]
            rows = pl.ds(t, bsz, stride=pitch)
            nr = lr[k] * sr - li[k] * si + bu_ref[k, rows, :]
            ni = lr[k] * si + li[k] * sr + bu_ref[ntile + k, rows, :]
            bu_ref[k, rows, :] = nr
            bu_ref[ntile + k, rows, :] = ni
            new += [nr, ni]
        return tuple(new)

    init = []
    for k in range(ntile):
        init += [st_ref[k], st_ref[ntile + k]]
    fin = lax.fori_loop(0, chunk, step, tuple(init), unroll=2)
    for k in range(ntile):
        st_ref[k] = fin[2 * k]
        st_ref[ntile + k] = fin[2 * k + 1]

    st = jnp.concatenate(
        [jnp.concatenate([bu_ref[k, b * pitch:b * pitch + chunk, :].astype(BF16) for k in range(2 * ntile)], axis=1)
         for b in range(bsz)], axis=0)
    y = jnp.dot(st, cb_ref[0], preferred_element_type=F32)
    for b in range(bsz):
        y_ref[b] = y[b * chunk:(b + 1) * chunk, :]


def _s5_scan(u, bb, cb, lam):
    bsz, s, w = u.shape
    nslab = w // LANES
    ntile = lam.shape[1] // 2
    chunk, pitch = S5_CHUNK, S5_PITCH
    assert s % chunk == 0 and bsz <= SUBLANES
    kern = functools.partial(_s5_kernel, bsz=bsz, chunk=chunk, pitch=pitch, ntile=ntile)
    return pl.pallas_call(
        kern,
        grid=(nslab, s // chunk),
        in_specs=[pl.BlockSpec((bsz, chunk, LANES), lambda j, c: (0, c, j)),
                  pl.BlockSpec((1, LANES, 2 * ntile * LANES), lambda j, c: (j, 0, 0)),
                  pl.BlockSpec((1, 2 * ntile * LANES, LANES), lambda j, c: (j, 0, 0)),
                  pl.BlockSpec((1, 2 * ntile, LANES), lambda j, c: (j, 0, 0))],
        out_specs=pl.BlockSpec((bsz, chunk, LANES), lambda j, c: (0, c, j)),
        out_shape=jax.ShapeDtypeStruct((bsz, s, w), F32),
        scratch_shapes=[pltpu.VMEM((2 * ntile, bsz * pitch, LANES), F32),
                        pltpu.VMEM((2 * ntile, bsz, LANES), F32)],
        compiler_params=_cparams("parallel", "arbitrary"),
        name="s5_scan",
    )(u, bb, cb, lam)


def _s5_params(a_re, a_im, log_dt, b_re, b_im, c_re, c_im):
    g, p = a_re.shape
    gc = b_re.shape[2]
    gps = LANES // gc
    nslab = g // gps
    dt = jnp.exp(log_dt)[:, None]
    mag = jnp.exp(dt * a_re)
    lam_r = mag * jnp.cos(dt * a_im)
    lam_i = mag * jnp.sin(dt * a_im)
    den = a_re * a_re + a_im * a_im
    zr = ((lam_r - 1.0) * a_re + lam_i * a_im) / den
    zi = (lam_i * a_re - (lam_r - 1.0) * a_im) / den
    bbar_r = zr[..., None] * b_re - zi[..., None] * b_im
    bbar_i = zr[..., None] * b_im + zi[..., None] * b_re
    eye = jnp.eye(gps, dtype=F32)

    def in_slab(bm):
        bm = bm.reshape(nslab, gps, p, gc)
        return jnp.einsum("jgpc,gh->jgchp", bm, eye).reshape(nslab, gps * gc, gps * p)

    def out_slab(cm):
        cm = cm.reshape(nslab, gps, gc, p)
        return jnp.einsum("jgcp,gh->jhpgc", cm, eye).reshape(nslab, gps * p, gps * gc)

    bb = jnp.concatenate([in_slab(bbar_r), in_slab(bbar_i)], axis=2).astype(BF16)
    cb = jnp.concatenate([out_slab(c_re), -out_slab(c_im)], axis=1).astype(BF16)
    ntile = gps * p // LANES
    lam = jnp.concatenate([lam_r.reshape(nslab, ntile, LANES), lam_i.reshape(nslab, ntile, LANES)], axis=1)
    return bb, cb, lam


def _log_sigmoid(x):
    return jnp.minimum(x, 0.0) - jnp.log1p(jnp.exp(-jnp.abs(x)))


def _mlstm_kernel(qk_ref, v_ref, o_ref, gc_ref, gr_ref, bc_ref, br_ref, cw_ref, nw_ref, y_ref,
                  buf_ref, qs_ref, c_ref, n_ref, m_ref, *, nb, nh, dh, chunk, kconv):
    ci = pl.program_id(1)
    width = nh * dh
    halo = SUBLANES

    @pl.when(ci == 0)
    def _():
        buf_ref[:, 0:halo, :] = jnp.zeros((nb, halo, 2 * width), F32)
        c_ref[...] = jnp.zeros_like(c_ref)
        n_ref[...] = jnp.zeros_like(n_ref)
        m_ref[...] = jnp.zeros_like(m_ref)

    tt = lax.broadcasted_iota(jnp.int32, (chunk, chunk), 0)
    ss = lax.broadcasted_iota(jnp.int32, (chunk, chunk), 1)
    causal = ss <= tt
    tril = causal.astype(F32)
    triu = (tt <= ss).astype(F32)

    gates = []
    for bi in range(nb):
        buf_ref[bi, halo:halo + chunk, :] = qk_ref[bi]
        base = halo - (kconv - 1)
        conv = cw_ref[0:1, :] * buf_ref[bi, base:base + chunk, :]
        for j in range(1, kconv):
            conv = conv + cw_ref[j:j + 1, :] * buf_ref[bi, base + j:base + j + chunk, :]
        buf_ref[bi, 0:halo, :] = buf_ref[bi, chunk:chunk + halo, :]
        qs_ref[bi] = conv * jax.nn.sigmoid(conv)

        gcol = gc_ref[bi] + bc_ref[...]
        col_id = lax.broadcasted_iota(jnp.int32, gcol.shape, 1)
        lcol = jnp.where(col_id >= nh, _log_sigmoid(gcol), gcol)
        grow = gr_ref[bi] + br_ref[...]
        row_id = lax.broadcasted_iota(jnp.int32, grow.shape, 0)
        lrow = jnp.where(row_id >= nh, _log_sigmoid(grow), grow)
        bcol = jnp.dot(tril, lcol, preferred_element_type=F32, precision=lax.Precision.HIGHEST)
        brow = jnp.dot(lrow, triu, preferred_element_type=F32, precision=lax.Precision.HIGHEST)
        gates.append((lcol, lrow, bcol, brow))

    inv_sqrt = 1.0 / math.sqrt(dh)
    for h in range(nh):
        hs = slice(h * dh, (h + 1) * dh)
        ks = slice(width + h * dh, width + (h + 1) * dh)
        bs = range(nb)
        st = [bi * nh + h for bi in bs]
        qf = [qs_ref[bi, :, hs] for bi in bs]
        q = [x.astype(BF16) for x in qf]
        kf = [qs_ref[bi, :, ks] * inv_sqrt for bi in bs]
        k = [x.astype(BF16) for x in kf]
        v = [v_ref[bi, :, hs] for bi in bs]
        b_c = [gates[bi][2][:, nh + h:nh + h + 1] for bi in bs]
        i_c = [gates[bi][0][:, h:h + 1] for bi in bs]
        b_r = [gates[bi][3][nh + h:nh + h + 1, :] for bi in bs]
        i_r = [gates[bi][1][h:h + 1, :] for bi in bs]
        g_tot = [x[chunk - 1:chunk, :] for x in b_c]
        m_prev = [m_ref[s_][:, 0:1] for s_ in st]
        c_prev = [c_ref[s_] for s_ in st]
        n_prev = [n_ref[s_] for s_ in st]

        qk_t = [lax.dot_general(q[bi], k[bi], (((1,), (1,)), ((), ())), preferred_element_type=F32) for bi in bs]
        q_c = [jnp.dot(q[bi], c_prev[bi].astype(BF16), preferred_element_type=F32) for bi in bs]
        dmat = [jnp.where(causal, b_c[bi] - b_r[bi] + i_r[bi], NEG_INF) for bi in bs]
        inter_log = [b_c[bi] + m_prev[bi] for bi in bs]
        m_row = [jnp.maximum(inter_log[bi], jnp.max(dmat[bi], axis=1, keepdims=True)) for bi in bs]
        sc = [qk_t[bi] * jnp.exp(dmat[bi] - m_row[bi]) for bi in bs]
        inter_scale = [jnp.exp(inter_log[bi] - m_row[bi]) for bi in bs]
        sc_v = [jnp.dot(sc[bi].astype(BF16), v[bi], preferred_element_type=F32) for bi in bs]

        wlog = [g_tot[bi] - b_c[bi] + i_c[bi] for bi in bs]
        m_loc = [jnp.max(x, axis=0, keepdims=True) for x in wlog]
        kw = [kf[bi] * jnp.exp(wlog[bi] - m_loc[bi]) for bi in bs]
        c_chunk = [lax.dot_general(kw[bi].astype(BF16), v[bi], (((0,), (0,)), ((), ())),
                                   preferred_element_type=F32) for bi in bs]

        num = [sc_v[bi] + inter_scale[bi] * q_c[bi] for bi in bs]
        qn = [jnp.sum(qf[bi] * n_prev[bi], axis=1, keepdims=True) for bi in bs]
        den = [jnp.sum(sc[bi], axis=1, keepdims=True) + inter_scale[bi] * qn[bi] for bi in bs]
        hh = [num[bi] / jnp.maximum(jnp.abs(den[bi]), jnp.exp(-m_row[bi])) for bi in bs]
        mu = [jnp.mean(x, axis=1, keepdims=True) for x in hh]
        var = [jnp.mean(jnp.square(hh[bi] - mu[bi]), axis=1, keepdims=True) for bi in bs]
        for bi in bs:
            hn = (hh[bi] - mu[bi]) * lax.rsqrt(var[bi] + LN_EPS) * nw_ref[:, hs]
            y_ref[bi, :, hs] = (o_ref[bi, :, hs].astype(F32) * hn).astype(y_ref.dtype)

        for bi in bs:
            n_chunk = jnp.sum(kw[bi], axis=0, keepdims=True)
            m_new = jnp.maximum(g_tot[bi] + m_prev[bi], m_loc[bi])
            a = jnp.exp(g_tot[bi] + m_prev[bi] - m_new)
            bb = jnp.exp(m_loc[bi] - m_new)
            c_ref[st[bi]] = a * c_prev[bi] + bb * c_chunk[bi]
            n_ref[st[bi]] = a * n_prev[bi] + bb * n_chunk
            m_ref[st[bi]] = jnp.broadcast_to(m_new, (1, LANES))


def _mlstm(qk, v, osig, gcol, grow, bias_c, bias_r, conv_w, norm_w, nh, nb=ML_BATCH):
    bsz, s, w2 = qk.shape
    width = w2 // 2
    dh = width // nh
    chunk = ML_CHUNK
    kconv = conv_w.shape[0]
    nb = math.gcd(nb, bsz)
    assert s % chunk == 0 and kconv - 1 <= SUBLANES
    kern = functools.partial(_mlstm_kernel, nb=nb, nh=nh, dh=dh, chunk=chunk, kconv=kconv)
    return pl.pallas_call(
        kern,
        grid=(bsz // nb, s // chunk),
        in_specs=[pl.BlockSpec((nb, chunk, w2), lambda b, c: (b, c, 0)),
                  pl.BlockSpec((nb, chunk, width), lambda b, c: (b, c, 0)),
                  pl.BlockSpec((nb, chunk, width), lambda b, c: (b, c, 0)),
                  pl.BlockSpec((nb, chunk, LANES), lambda b, c: (b, c, 0)),
                  pl.BlockSpec((nb, 2 * nh, chunk), lambda b, c: (b, 0, c)),
                  _const_spec((1, LANES)),
                  _const_spec((2 * nh, 1)),
                  _const_spec((kconv, w2)),
                  _const_spec((1, width))],
        out_specs=pl.BlockSpec((nb, chunk, width), lambda b, c: (b, c, 0)),
        out_shape=jax.ShapeDtypeStruct((bsz, s, width), BF16),
        scratch_shapes=[pltpu.VMEM((nb, SUBLANES + chunk, w2), F32),
                        pltpu.VMEM((nb, chunk, w2), F32),
                        pltpu.VMEM((nb * nh, dh, dh), F32),
                        pltpu.VMEM((nb * nh, 1, dh), F32),
                        pltpu.VMEM((nb * nh, 1, LANES), F32)],
        compiler_params=_cparams("parallel", "arbitrary"),
        name="mlstm",
    )(qk, v, osig, gcol, grow, bias_c, bias_r, conv_w, norm_w)


def _layer_norm(x, g, b):
    mu = jnp.mean(x, axis=-1, keepdims=True)
    var = jnp.mean(jnp.square(x - mu), axis=-1, keepdims=True)
    return (x - mu) * lax.rsqrt(var + LN_EPS) * g + b


def _mix_kernel(yp_ref, u_ref, yb_ref, ga_ref, gb_ref, x_ref, d_ref, wg_ref, bg_ref, wa_ref, wb_ref, wo_ref,
                lg_ref, lb_ref, h_ref, ht_ref, *, alpha):
    y = jax.nn.gelu(yp_ref[...] + d_ref[...] * u_ref[...])
    gate = jax.nn.sigmoid(jnp.dot(y.astype(BF16), wg_ref[...], preferred_element_type=F32) + bg_ref[...])
    ya = (y * gate).astype(BF16)
    merged = (ga_ref[...].astype(F32) * jnp.dot(ya, wa_ref[...], preferred_element_type=F32)
              + gb_ref[...].astype(F32) * jnp.dot(yb_ref[...], wb_ref[...], preferred_element_type=F32))
    mix = jnp.dot(merged.astype(BF16), wo_ref[...], preferred_element_type=F32)
    h = _layer_norm(alpha * x_ref[...] + mix, lg_ref[...], lb_ref[...])
    h_ref[...] = h
    ht_ref[...] = h.T.astype(BF16)


def _mix(y_pre, u, y_b, gates, x, d, w_glu, b_glu, w_a, w_b, w_o, ln_g, ln_b, alpha, tm=256):
    t, dm = x.shape
    w = u.shape[1]
    assert t % tm == 0
    row = lambda width: pl.BlockSpec((tm, width), lambda i: (i, 0))
    return pl.pallas_call(
        functools.partial(_mix_kernel, alpha=alpha),
        grid=(t // tm,),
        in_specs=[row(w), row(w), row(w),
                  pl.BlockSpec((tm, dm), lambda i: (i, 0)),
                  pl.BlockSpec((tm, dm), lambda i: (i, 1)),
                  row(dm),
                  _const_spec((1, w)), _const_spec((w, w)), _const_spec((1, w)),
                  _const_spec((w, dm)), _const_spec((w, dm)), _const_spec((dm, dm)),
                  _const_spec((1, dm)), _const_spec((1, dm))],
        out_specs=[pl.BlockSpec((tm, dm), lambda i: (i, 0)),
                   pl.BlockSpec((dm, tm), lambda i: (0, i))],
        out_shape=[jax.ShapeDtypeStruct((t, dm), F32), jax.ShapeDtypeStruct((dm, t), BF16)],
        compiler_params=_cparams("parallel"),
        name="mix_ln1",
    )(y_pre, u, y_b, gates, gates, x, d, w_glu, b_glu, w_a, w_b, w_o, ln_g, ln_b)


def _oddeven_merge(lo, hi, r):
    step = r * 2
    if step < hi - lo:
        yield from _oddeven_merge(lo, hi, step)
        yield from _oddeven_merge(lo + r, hi, step)
        yield from [(i, i + r) for i in range(lo + r, hi - r, step)]
    else:
        yield (lo, lo + r)


def _oddeven_merge_sort(lo, hi):
    if hi - lo >= 1:
        mid = lo + (hi - lo) // 2
        yield from _oddeven_merge_sort(lo, mid)
        yield from _oddeven_merge_sort(mid + 1, hi)
        yield from _oddeven_merge(lo, hi, 1)


def _sorted_topk_rows(s, dst_ref, k):
    g = s.shape[0] // SUBLANES
    w = [s[j * SUBLANES:(j + 1) * SUBLANES, :] for j in range(g)]
    for i, j in _oddeven_merge_sort(0, g - 1):
        w[i], w[j] = jnp.maximum(w[i], w[j]), jnp.minimum(w[i], w[j])
    for r in range(k):
        mx = jnp.max(w[0], axis=0, keepdims=True)
        dst_ref[r:r + 1, :] = mx
        left = k - 1 - r
        if left > 0:
            hit = w[0] == mx
            for lvl in range(min(g, left)):
                below = w[lvl + 1] if lvl + 1 < g else NEG_INF
                w[lvl] = jnp.where(hit, below, w[lvl])


def _route_kernel(h_ref, wq_ref, keys_ref, cnt_ref, e1_ref, rank_ref, e2_ref, sa_ref, sb_ref, a_ref, b_ref, cand_ref,
                  *, nh, nk, half, topk, ncand_rows):
    i = pl.program_id(0)
    tb = h_ref.shape[0]
    nrank = topk + 1

    @pl.when(i == 0)
    def _():
        sa_ref[...] = jnp.zeros_like(sa_ref)

    def step(rd_ref, wr_ref):
        hb = h_ref[...].astype(BF16)
        hh = nh // 2
        qs = []
        for g in range(2):
            qs.append(jnp.dot(hb, wq_ref[:, g * hh * 2 * half:(g + 1) * hh * 2 * half],
                              preferred_element_type=F32).astype(BF16))
            for h in range(g * nh // 4, (g + 1) * nh // 4):
                _rank_head(rd_ref, h)
        for g in range(2):
            for c in range(2 * hh):
                wr_ref[g * 2 * hh + c] = lax.dot_general(keys_ref[g * 2 * hh + c], qs[g][:, c * half:(c + 1) * half],
                                                         (((1,), (1,)), ((), ())), preferred_element_type=F32)
            for h in range(hh + g * nh // 4, hh + (g + 1) * nh // 4):
                _rank_head(rd_ref, h)

    def _rank_head(rd_ref, h):
        s1 = rd_ref[2 * h]
        s2 = rd_ref[2 * h + 1]
        _sorted_topk_rows(s1, a_ref, nrank)
        _sorted_topk_rows(s2, b_ref, nrank)
        a = a_ref[0:nrank, :]
        b = b_ref[0:nrank, :]
        cand_ref[...] = jnp.full((ncand_rows, tb), NEG_INF, F32)
        off = 0
        for i in range(nrank):
            n_i = nrank // (i + 1)
            cand_ref[off:off + n_i, :] = a[i:i + 1, :] + b[0:n_i, :]
            off += n_i
        cur = cand_ref[...]
        top = a[0:1, :] + b[0:1, :]
        z = jnp.zeros((1, tb), F32)
        kth = top
        for r in range(topk):
            kth = jnp.max(cur, axis=0, keepdims=True)
            z = z + jnp.exp(kth - top)
            cur = jnp.where(cur == kth, NEG_INF, cur)
        nxt = jnp.max(cur, axis=0, keepdims=True)
        tau = 0.5 * (kth + nxt)
        thr = tau - s1
        cnt = jnp.zeros_like(s1)
        for r in range(topk):
            cnt = jnp.where(b[r:r + 1, :] >= thr, float(r + 1), cnt)
        rank = jnp.full_like(s2, float(nrank))
        for r in reversed(range(nrank)):
            rank = jnp.where(s2 >= b[r:r + 1, :], float(r), rank)
        cnt_ref[h] = cnt
        e1_ref[h] = jnp.exp(s1 - a[0:1, :])
        rank_ref[h] = rank.astype(rank_ref.dtype)
        e2_ref[h] = (jnp.exp(s2 - b[0:1, :]) / z).astype(e2_ref.dtype)

    @pl.when(lax.rem(i, 2) == 0)
    def _():
        step(sa_ref, sb_ref)

    @pl.when(lax.rem(i, 2) == 1)
    def _():
        step(sb_ref, sa_ref)


def _route(h, wq, keys2, nh, tb=256):
    t, dm = h.shape
    nk, half = keys2.shape[1], keys2.shape[2]
    topk = PEER_TOPK
    nrank = topk + 1
    ncand = sum(nrank // (r + 1) for r in range(nrank))
    ncand_rows = -(-ncand // SUBLANES) * SUBLANES
    rank_rows = -(-nrank // SUBLANES) * SUBLANES
    assert t % tb == 0 and nk > nrank and nh % 4 == 0
    nblk = t // tb
    kern = functools.partial(_route_kernel, nh=nh, nk=nk, half=half, topk=topk, ncand_rows=ncand_rows)
    ospec = pl.BlockSpec((nh, nk, tb), lambda i: (0, 0, jnp.maximum(i - 1, 0)))
    slot = pltpu.VMEM((2 * nh, nk, tb), F32)
    return pl.pallas_call(
        kern,
        grid=(nblk + 1,),
        in_specs=[pl.BlockSpec((tb, dm), lambda i: (jnp.minimum(i, nblk - 1), 0)),
                  _const_spec(wq.shape), _const_spec(keys2.shape)],
        out_specs=[ospec] * 4,
        out_shape=[jax.ShapeDtypeStruct((nh, nk, t), dt) for dt in (F32, F32, BF16, BF16)],
        scratch_shapes=[slot, slot, pltpu.VMEM((rank_rows, tb), F32), pltpu.VMEM((rank_rows, tb), F32),
                        pltpu.VMEM((ncand_rows, tb), F32)],
        compiler_params=_cparams("arbitrary"),
        name="peer_route",
    )(h, wq, keys2)


_GELU_K1 = -2.0 * math.sqrt(2.0 / math.pi) * math.log2(math.e)
_GELU_K2 = _GELU_K1 * 0.044715


def _gelu_tanh(x):
    return x / (1.0 + jnp.exp2(x * (_GELU_K1 + _GELU_K2 * (x * x))))


def _expert_kernel(xt_ref, u_ref, vt_ref, cnt_ref, e1_ref, rank_ref, e2_ref, o_ref, sa_ref, sb_ref,
                   *, nh, nk, rows, sub):
    e = pl.program_id(1)

    @pl.when(e == 0)
    def _():
        o_ref[...] = jnp.zeros_like(o_ref)
        sa_ref[...] = jnp.zeros_like(sa_ref)

    def step(rd_ref, wr_ref):
        for q in range(xt_ref.shape[1] // sub):
            cs = slice(q * sub, (q + 1) * sub)
            wr_ref[:, cs] = jnp.dot(u_ref[...], xt_ref[:, cs], preferred_element_type=F32)
            parts = []
            for r in range(rows):
                g = None
                for h in range(nh):
                    sel = jnp.where(rank_ref[h, :, cs] < cnt_ref[h, r:r + 1, cs].astype(BF16),
                                    e2_ref[h, :, cs] * e1_ref[h, r:r + 1, cs].astype(BF16), 0.0)
                    g = sel if g is None else g + sel
                parts.append(_gelu_tanh(rd_ref[r * nk:(r + 1) * nk, cs]).astype(BF16) * g)
            act = jnp.concatenate(parts, axis=0)
            o_ref[:, cs] += jnp.dot(vt_ref[...], act, preferred_element_type=F32)

    @pl.when(lax.rem(e, 2) == 0)
    def _():
        step(sa_ref, sb_ref)

    @pl.when(lax.rem(e, 2) == 1)
    def _():
        step(sb_ref, sa_ref)


def _experts(xt, u_tab, vt_tab, cnt, e1, rank, e2, tb=1024, rows=8, sub=256):
    dm, t = xt.shape
    ne = u_tab.shape[0]
    nh, nk, _ = cnt.shape
    neb = rows * nk
    tb = min(tb, t)
    assert t % tb == 0 and tb % sub == 0 and ne % neb == 0 and ne == nk * nk
    nblk = ne // neb
    kern = functools.partial(_expert_kernel, nh=nh, nk=nk, rows=rows, sub=sub)
    once = pl.Buffered(1)
    score_blk = lambda e: jnp.minimum(e, nblk - 1)
    gate_blk = lambda e: jnp.maximum(e - 1, 0)
    return pl.pallas_call(
        kern,
        grid=(t // tb, nblk + 1),
        in_specs=[pl.BlockSpec((dm, tb), lambda i, e: (0, i), pipeline_mode=once),
                  pl.BlockSpec((neb, dm), lambda i, e: (score_blk(e), 0)),
                  pl.BlockSpec((dm, neb), lambda i, e: (0, gate_blk(e))),
                  pl.BlockSpec((nh, rows, tb), lambda i, e: (0, gate_blk(e), i)),
                  pl.BlockSpec((nh, rows, tb), lambda i, e: (0, gate_blk(e), i)),
                  pl.BlockSpec((nh, nk, tb), lambda i, e: (0, 0, i), pipeline_mode=once),
                  pl.BlockSpec((nh, nk, tb), lambda i, e: (0, 0, i), pipeline_mode=once)],
        out_specs=pl.BlockSpec((dm, tb), lambda i, e: (0, i), pipeline_mode=once),
        out_shape=jax.ShapeDtypeStruct((dm, t), F32),
        scratch_shapes=[pltpu.VMEM((neb, tb), F32), pltpu.VMEM((neb, tb), F32)],
        compiler_params=_cparams("parallel", "arbitrary"),
        name="peer_experts",
    )(xt, u_tab, vt_tab, cnt, e1, rank, e2)


def _final_kernel(h_ref, yt_ref, p_ref, lg_ref, lb_ref, wg_ref, wp_ref, o_ref, *, alpha):
    h = _layer_norm(alpha * h_ref[...] + yt_ref[...].T, lg_ref[...], lb_ref[...])
    gate = jax.nn.sigmoid(jnp.dot(h.astype(BF16), wg_ref[...], preferred_element_type=F32))
    proj = jnp.dot(p_ref[...].astype(BF16), wp_ref[...], preferred_element_type=F32)
    o_ref[...] = h + gate * proj


def _final(h, yt, p, ln_g, ln_b, w_gate, w_proj, alpha, tm=512):
    t, dm = h.shape
    pd = p.shape[1]
    assert t % tm == 0
    return pl.pallas_call(
        functools.partial(_final_kernel, alpha=alpha),
        grid=(t // tm,),
        in_specs=[pl.BlockSpec((tm, dm), lambda i: (i, 0)),
                  pl.BlockSpec((dm, tm), lambda i: (0, i)),
                  pl.BlockSpec((tm, pd), lambda i: (i, 0)),
                  _const_spec((1, dm)), _const_spec((1, dm)),
                  _const_spec((dm, dm)), _const_spec((pd, dm))],
        out_specs=pl.BlockSpec((tm, dm), lambda i: (i, 0)),
        out_shape=jax.ShapeDtypeStruct((t, dm), F32),
        compiler_params=_cparams("parallel"),
        name="ln2_ple",
    )(h, yt, p, ln_g, ln_b, w_gate, w_proj)


def _layer(h, p, w_in, b_igate, b_fgate, conv_qk, mh_norm_w, a_re, a_im, log_dt, b_re, b_im, c_re, c_im, d_skip,
           w_glu, b_glu, w_up_ssm, w_up_ml, w_out, ln1_g, ln1_b, peer_wq, peer_keys, peer_u, peer_v,
           ln2_g, ln2_b, ple_w_gate, ple_w_proj, alpha):
    bsz, s, dm = h.shape
    t = bsz * s
    nh = b_igate.shape[0]
    ssm_w = d_skip.shape[0]
    ml_w = mh_norm_w.shape[0]
    x2 = h.reshape(t, dm)
    xb = x2.astype(BF16)

    o0 = ssm_w
    o1 = o0 + 2 * ml_w
    o2 = o1 + ml_w
    o3 = o2 + ml_w
    o4 = o3 + 2 * nh
    u, qk, v, osig, gates = _in_proj(
        xb, [w_in[:, :o3].astype(BF16), w_in[:, o4:].astype(BF16)],
        [(0, ssm_w, F32, False), (0, 2 * ml_w, F32, False), (0, ml_w, BF16, False), (0, ml_w, BF16, True),
         (1, w_in.shape[1] - o4, BF16, True)])
    w_if = jnp.zeros((dm, LANES), F32).at[:, :2 * nh].set(w_in[:, o3:o4]).astype(BF16)
    gif = _matmul(xb, w_if, F32)

    bb, cb, lam = _s5_params(a_re, a_im, log_dt, b_re, b_im, c_re, c_im)
    y_pre = _s5_scan(u.reshape(bsz, s, ssm_w), bb, cb, lam)

    gcol = gif.reshape(bsz, s, LANES)
    grow = jnp.swapaxes(gcol[:, :, :2 * nh], 1, 2)
    bias = jnp.concatenate([b_igate, b_fgate])
    bias_c = jnp.zeros((1, LANES), F32).at[0, :2 * nh].set(bias)
    y_b = _mlstm(qk.reshape(bsz, s, 2 * ml_w), v.reshape(bsz, s, ml_w), osig.reshape(bsz, s, ml_w),
                 gcol, grow, bias_c, bias[:, None], conv_qk, mh_norm_w[None, :], nh)

    h1, h1t = _mix(y_pre.reshape(t, ssm_w), u, y_b.reshape(t, ml_w), gates, x2, d_skip[None, :],
                   w_glu.astype(BF16), b_glu[None, :], w_up_ssm.astype(BF16), w_up_ml.astype(BF16),
                   w_out.astype(BF16), ln1_g[None, :], ln1_b[None, :], alpha)

    ph, _, nk, half = peer_keys.shape
    keys2 = peer_keys.reshape(ph * 2, nk, half).astype(BF16)
    cnt, e1, rank, e2 = _route(h1, peer_wq.astype(BF16), keys2, ph)
    y2t = _experts(h1t, peer_u.astype(BF16), peer_v.T.astype(BF16), cnt, e1, rank, e2)

    out = _final(h1, y2t, p.reshape(t, -1), ln2_g[None, :], ln2_b[None, :],
                 ple_w_gate.astype(BF16), ple_w_proj.astype(BF16), alpha)
    return out.reshape(bsz, s, dm)


def kernel(x, p, w_in, b_igate, b_fgate, conv_qk, mh_norm_w, ssm_a_re, ssm_a_im, ssm_log_dt, ssm_b_re, ssm_b_im,
           ssm_c_re, ssm_c_im, ssm_d, w_glu, b_glu, w_up_ssm, w_up_ml, w_out, ln1_g, ln1_b, peer_wq, peer_keys,
           peer_u, peer_v, ln2_g, ln2_b, ple_w_gate, ple_w_proj):
    depth = w_in.shape[0]
    alpha = (2 * depth) ** 0.25
    h = x
    for i in range(depth):
        h = _layer(h, p[i], w_in[i], b_igate[i], b_fgate[i], conv_qk[i], mh_norm_w[i], ssm_a_re[i], ssm_a_im[i],
                   ssm_log_dt[i], ssm_b_re[i], ssm_b_im[i], ssm_c_re[i], ssm_c_im[i], ssm_d[i], w_glu[i], b_glu[i],
                   w_up_ssm[i], w_up_ml[i], w_out[i], ln1_g[i], ln1_b[i], peer_wq[i], peer_keys[i], peer_u[i],
                   peer_v[i], ln2_g[i], ln2_b[i], ple_w_gate[i], ple_w_proj[i], alpha)
    return h
```

```python
import functools
import math

import jax
import jax.numpy as jnp
from jax import lax
from jax.experimental import pallas as pl
from jax.experimental.pallas import tpu as pltpu

F32 = jnp.float32
BF16 = jnp.bfloat16

LN_EPS = 1e-5
PEER_TOPK = 16
LANES = 128
SUBLANES = 8
VMEM_LIMIT = 56 * 1024 * 1024

ML_CHUNK = 128
ML_BATCH = 4
S5_CHUNK = 256
S5_PITCH = S5_CHUNK + SUBLANES
NEG_INF = float("-inf")


def _cparams(*sem):
    return pltpu.CompilerParams(dimension_semantics=sem, vmem_limit_bytes=VMEM_LIMIT)


def _const_spec(shape):
    nd = len(shape)
    return pl.BlockSpec(shape, lambda *_: (0,) * nd, pipeline_mode=pl.Buffered(1))


def _mm_kernel(a_ref, w_ref, o_ref):
    o_ref[...] = jnp.dot(a_ref[...], w_ref[...], preferred_element_type=F32).astype(o_ref.dtype)


def _matmul(a, w, out_dtype, tm=1024, tn=512):
    m, k = a.shape
    n = w.shape[1]
    tm, tn = min(tm, m), min(tn, n)
    assert m % tm == 0 and n % tn == 0
    return pl.pallas_call(
        _mm_kernel,
        grid=(m // tm, n // tn),
        in_specs=[pl.BlockSpec((tm, k), lambda i, j: (i, 0)),
                  pl.BlockSpec((k, tn), lambda i, j: (0, j))],
        out_specs=pl.BlockSpec((tm, tn), lambda i, j: (i, j)),
        out_shape=jax.ShapeDtypeStruct((m, n), out_dtype),
        compiler_params=_cparams("parallel", "parallel"),
        name="gate_proj",
    )(a, w)


def _in_proj_kernel(a_ref, *refs, segs, nw, sub):
    j = pl.program_id(1)
    w_refs, o_refs = refs[:nw], refs[nw:]
    for o_ref, (lo, hi, wi, sig) in zip(o_refs, segs):
        @pl.when((j >= lo) & (j < hi))
        def _(o_ref=o_ref, w_ref=w_refs[wi], sig=sig):
            for q in range(o_ref.shape[1] // sub):
                cs = slice(q * sub, (q + 1) * sub)
                acc = jnp.dot(a_ref[...], w_ref[:, cs], preferred_element_type=F32)
                o_ref[:, cs] = (jax.nn.sigmoid(acc) if sig else acc).astype(o_ref.dtype)


def _in_proj(a, weights, outs, tm=1024, tn=1024, sub=256):
    m, k = a.shape
    tm = min(tm, m)
    assert m % tm == 0 and all(o[1] % tn == 0 for o in outs)
    segs, ospecs, shapes, step = [], [], [], 0
    wlo = [None] * len(weights)
    wnb = [0] * len(weights)
    for wi, width, dtype, sig in outs:
        nblk = width // tn
        if wlo[wi] is None:
            wlo[wi] = step
        assert wlo[wi] + wnb[wi] == step, "outputs sharing a weight must be adjacent"
        wnb[wi] += nblk
        segs.append((step, step + nblk, wi, sig))
        ospecs.append(pl.BlockSpec((tm, tn), lambda i, j, lo=step, nblk=nblk: (i, jnp.clip(j - lo, 0, nblk - 1))))
        shapes.append(jax.ShapeDtypeStruct((m, width), dtype))
        step += nblk
    assert all(w.shape == (k, nb * tn) for w, nb in zip(weights, wnb))
    wspecs = [pl.BlockSpec((k, tn), lambda i, j, lo=lo, nb=nb: (0, jnp.clip(j - lo, 0, nb - 1)))
              for lo, nb in zip(wlo, wnb)]
    return pl.pallas_call(
        functools.partial(_in_proj_kernel, segs=tuple(segs), nw=len(weights), sub=sub),
        grid=(m // tm, step),
        in_specs=[pl.BlockSpec((tm, k), lambda i, j: (i, 0))] + wspecs,
        out_specs=ospecs,
        out_shape=shapes,
        compiler_params=_cparams("parallel", "arbitrary"),
        name="in_proj",
    )(a, *weights)


def _s5_kernel(u_ref, bb_ref, cb_ref, lam_ref, y_ref, bu_ref, st_ref, *, bsz, chunk, pitch, ntile):
    c = pl.program_id(1)

    @pl.when(c == 0)
    def _():
        st_ref[...] = jnp.zeros_like(st_ref)

    bb = bb_ref[0]
    for b in range(bsz):
        bu = jnp.dot(u_ref[b].astype(BF16), bb, preferred_element_type=F32)
        for k in range(2 * ntile):
            bu_ref[k, b * pitch:b * pitch + chunk, :] = bu[:, k * LANES:(k + 1) * LANES]

    lam = lam_ref[0]
    lr = [jnp.broadcast_to(lam[k:k + 1, :], (bsz, LANES)) for k in range(ntile)]
    li = [jnp.broadcast_to(lam[ntile + k:ntile + k + 1, :], (bsz, LANES)) for k in range(ntile)]

    def step(t, carry):
        new = []
        for k in range(ntile):
            sr, si = carry[2 * k], carry[2 * k + 1]
            rows = pl.ds(t, bsz, stride=pitch)
            nr = lr[k] * sr - li[k] * si + bu_ref[k, rows, :]
            ni = lr[k] * si + li[k] * sr + bu_ref[ntile + k, rows, :]
            bu_ref[k, rows, :] = nr
            bu_ref[ntile + k, rows, :] = ni
            new += [nr, ni]
        return tuple(new)

    init = []
    for k in range(ntile):
        init += [st_ref[k], st_ref[ntile + k]]
    fin = lax.fori_loop(0, chunk, step, tuple(init), unroll=2)
    for k in range(ntile):
        st_ref[k] = fin[2 * k]
        st_ref[ntile + k] = fin[2 * k + 1]

    cb = cb_ref[0]
    for b in range(bsz):
        st = jnp.concatenate([bu_ref[k, b * pitch:b * pitch + chunk, :] for k in range(2 * ntile)], axis=1)
        y_ref[b] = jnp.dot(st.astype(BF16), cb, preferred_element_type=F32)


def _s5_scan(u, bb, cb, lam):
    bsz, s, w = u.shape
    nslab = w // LANES
    ntile = lam.shape[1] // 2
    chunk, pitch = S5_CHUNK, S5_PITCH
    assert s % chunk == 0 and bsz <= SUBLANES
    kern = functools.partial(_s5_kernel, bsz=bsz, chunk=chunk, pitch=pitch, ntile=ntile)
    return pl.pallas_call(
        kern,
        grid=(nslab, s // chunk),
        in_specs=[pl.BlockSpec((bsz, chunk, LANES), lambda j, c: (0, c, j)),
                  pl.BlockSpec((1, LANES, 2 * ntile * LANES), lambda j, c: (j, 0, 0)),
                  pl.BlockSpec((1, 2 * ntile * LANES, LANES), lambda j, c: (j, 0, 0)),
                  pl.BlockSpec((1, 2 * ntile, LANES), lambda j, c: (j, 0, 0))],
        out_specs=pl.BlockSpec((bsz, chunk, LANES), lambda j, c: (0, c, j)),
        out_shape=jax.ShapeDtypeStruct((bsz, s, w), F32),
        scratch_shapes=[pltpu.VMEM((2 * ntile, bsz * pitch, LANES), F32),
                        pltpu.VMEM((2 * ntile, bsz, LANES), F32)],
        compiler_params=_cparams("parallel", "arbitrary"),
        name="s5_scan",
    )(u, bb, cb, lam)


def _s5_params(a_re, a_im, log_dt, b_re, b_im, c_re, c_im):
    g, p = a_re.shape
    gc = b_re.shape[2]
    gps = LANES // gc
    nslab = g // gps
    dt = jnp.exp(log_dt)[:, None]
    mag = jnp.exp(dt * a_re)
    lam_r = mag * jnp.cos(dt * a_im)
    lam_i = mag * jnp.sin(dt * a_im)
    den = a_re * a_re + a_im * a_im
    zr = ((lam_r - 1.0) * a_re + lam_i * a_im) / den
    zi = (lam_i * a_re - (lam_r - 1.0) * a_im) / den
    bbar_r = zr[..., None] * b_re - zi[..., None] * b_im
    bbar_i = zr[..., None] * b_im + zi[..., None] * b_re
    eye = jnp.eye(gps, dtype=F32)

    def in_slab(bm):
        bm = bm.reshape(nslab, gps, p, gc)
        return jnp.einsum("jgpc,gh->jgchp", bm, eye).reshape(nslab, gps * gc, gps * p)

    def out_slab(cm):
        cm = cm.reshape(nslab, gps, gc, p)
        return jnp.einsum("jgcp,gh->jhpgc", cm, eye).reshape(nslab, gps * p, gps * gc)

    bb = jnp.concatenate([in_slab(bbar_r), in_slab(bbar_i)], axis=2).astype(BF16)
    cb = jnp.concatenate([out_slab(c_re), -out_slab(c_im)], axis=1).astype(BF16)
    ntile = gps * p // LANES
    lam = jnp.concatenate([lam_r.reshape(nslab, ntile, LANES), lam_i.reshape(nslab, ntile, LANES)], axis=1)
    return bb, cb, lam


def _log_sigmoid(x):
    return jnp.minimum(x, 0.0) - jnp.log1p(jnp.exp(-jnp.abs(x)))


def _mlstm_kernel(qk_ref, v_ref, o_ref, gc_ref, gr_ref, bc_ref, br_ref, cw_ref, nw_ref, y_ref,
                  buf_ref, qs_ref, c_ref, n_ref, m_ref, *, nb, nh, dh, chunk, kconv):
    ci = pl.program_id(1)
    width = nh * dh
    halo = SUBLANES

    @pl.when(ci == 0)
    def _():
        buf_ref[:, 0:halo, :] = jnp.zeros((nb, halo, 2 * width), F32)
        c_ref[...] = jnp.zeros_like(c_ref)
        n_ref[...] = jnp.zeros_like(n_ref)
        m_ref[...] = jnp.zeros_like(m_ref)

    tt = lax.broadcasted_iota(jnp.int32, (chunk, chunk), 0)
    ss = lax.broadcasted_iota(jnp.int32, (chunk, chunk), 1)
    causal = ss <= tt
    tril = causal.astype(F32)
    triu = (tt <= ss).astype(F32)

    gates = []
    for bi in range(nb):
        buf_ref[bi, halo:halo + chunk, :] = qk_ref[bi]
        base = halo - (kconv - 1)
        conv = cw_ref[0:1, :] * buf_ref[bi, base:base + chunk, :]
        for j in range(1, kconv):
            conv = conv + cw_ref[j:j + 1, :] * buf_ref[bi, base + j:base + j + chunk, :]
        buf_ref[bi, 0:halo, :] = buf_ref[bi, chunk:chunk + halo, :]
        qs_ref[bi] = conv * jax.nn.sigmoid(conv)

        gcol = gc_ref[bi] + bc_ref[...]
        col_id = lax.broadcasted_iota(jnp.int32, gcol.shape, 1)
        lcol = jnp.where(col_id >= nh, _log_sigmoid(gcol), gcol)
        grow = gr_ref[bi] + br_ref[...]
        row_id = lax.broadcasted_iota(jnp.int32, grow.shape, 0)
        lrow = jnp.where(row_id >= nh, _log_sigmoid(grow), grow)
        bcol = jnp.dot(tril, lcol, preferred_element_type=F32, precision=lax.Precision.HIGHEST)
        brow = jnp.dot(lrow, triu, preferred_element_type=F32, precision=lax.Precision.HIGHEST)
        gates.append((lcol, lrow, bcol, brow))

    inv_sqrt = 1.0 / math.sqrt(dh)
    for h in range(nh):
        hs = slice(h * dh, (h + 1) * dh)
        ks = slice(width + h * dh, width + (h + 1) * dh)
        bs = range(nb)
        st = [bi * nh + h for bi in bs]
        qf = [qs_ref[bi, :, hs] for bi in bs]
        q = [x.astype(BF16) for x in qf]
        kf = [qs_ref[bi, :, ks] * inv_sqrt for bi in bs]
        k = [x.astype(BF16) for x in kf]
        v = [v_ref[bi, :, hs] for bi in bs]
        b_c = [gates[bi][2][:, nh + h:nh + h + 1] for bi in bs]
        i_c = [gates[bi][0][:, h:h + 1] for bi in bs]
        b_r = [gates[bi][3][nh + h:nh + h + 1, :] for bi in bs]
        i_r = [gates[bi][1][h:h + 1, :] for bi in bs]
        g_tot = [x[chunk - 1:chunk, :] for x in b_c]
        m_prev = [m_ref[s_][:, 0:1] for s_ in st]
        c_prev = [c_ref[s_] for s_ in st]
        n_prev = [n_ref[s_] for s_ in st]

        qk_t = [lax.dot_general(q[bi], k[bi], (((1,), (1,)), ((), ())), preferred_element_type=F32) for bi in bs]
        q_c = [jnp.dot(q[bi], c_prev[bi].astype(BF16), preferred_element_type=F32) for bi in bs]
        dmat = [jnp.where(causal, b_c[bi] - b_r[bi] + i_r[bi], NEG_INF) for bi in bs]
        inter_log = [b_c[bi] + m_prev[bi] for bi in bs]
        m_row = [jnp.maximum(inter_log[bi], jnp.max(dmat[bi], axis=1, keepdims=True)) for bi in bs]
        sc = [qk_t[bi] * jnp.exp(dmat[bi] - m_row[bi]) for bi in bs]
        inter_scale = [jnp.exp(inter_log[bi] - m_row[bi]) for bi in bs]
        sc_v = [jnp.dot(sc[bi].astype(BF16), v[bi], preferred_element_type=F32) for bi in bs]

        wlog = [g_tot[bi] - b_c[bi] + i_c[bi] for bi in bs]
        m_loc = [jnp.max(x, axis=0, keepdims=True) for x in wlog]
        kw = [kf[bi] * jnp.exp(wlog[bi] - m_loc[bi]) for bi in bs]
        c_chunk = [lax.dot_general(kw[bi].astype(BF16), v[bi], (((0,), (0,)), ((), ())),
                                   preferred_element_type=F32) for bi in bs]

        num = [sc_v[bi] + inter_scale[bi] * q_c[bi] for bi in bs]
        qn = [jnp.sum(qf[bi] * n_prev[bi], axis=1, keepdims=True) for bi in bs]
        den = [jnp.sum(sc[bi], axis=1, keepdims=True) + inter_scale[bi] * qn[bi] for bi in bs]
        hh = [num[bi] / jnp.maximum(jnp.abs(den[bi]), jnp.exp(-m_row[bi])) for bi in bs]
        mu = [jnp.mean(x, axis=1, keepdims=True) for x in hh]
        var = [jnp.mean(jnp.square(hh[bi] - mu[bi]), axis=1, keepdims=True) for bi in bs]
        for bi in bs:
            hn = (hh[bi] - mu[bi]) * lax.rsqrt(var[bi] + LN_EPS) * nw_ref[:, hs]
            y_ref[bi, :, hs] = (o_ref[bi, :, hs].astype(F32) * hn).astype(y_ref.dtype)

        for bi in bs:
            n_chunk = jnp.sum(kw[bi], axis=0, keepdims=True)
            m_new = jnp.maximum(g_tot[bi] + m_prev[bi], m_loc[bi])
            a = jnp.exp(g_tot[bi] + m_prev[bi] - m_new)
            bb = jnp.exp(m_loc[bi] - m_new)
            c_ref[st[bi]] = a * c_prev[bi] + bb * c_chunk[bi]
            n_ref[st[bi]] = a * n_prev[bi] + bb * n_chunk
            m_ref[st[bi]] = jnp.broadcast_to(m_new, (1, LANES))


def _mlstm(qk, v, osig, gcol, grow, bias_c, bias_r, conv_w, norm_w, nh, nb=ML_BATCH):
    bsz, s, w2 = qk.shape
    width = w2 // 2
    dh = width // nh
    chunk = ML_CHUNK
    kconv = conv_w.shape[0]
    nb = math.gcd(nb, bsz)
    assert s % chunk == 0 and kconv - 1 <= SUBLANES
    kern = functools.partial(_mlstm_kernel, nb=nb, nh=nh, dh=dh, chunk=chunk, kconv=kconv)
    return pl.pallas_call(
        kern,
        grid=(bsz // nb, s // chunk),
        in_specs=[pl.BlockSpec((nb, chunk, w2), lambda b, c: (b, c, 0)),
                  pl.BlockSpec((nb, chunk, width), lambda b, c: (b, c, 0)),
                  pl.BlockSpec((nb, chunk, width), lambda b, c: (b, c, 0)),
                  pl.BlockSpec((nb, chunk, LANES), lambda b, c: (b, c, 0)),
                  pl.BlockSpec((nb, 2 * nh, chunk), lambda b, c: (b, 0, c)),
                  _const_spec((1, LANES)),
                  _const_spec((2 * nh, 1)),
                  _const_spec((kconv, w2)),
                  _const_spec((1, width))],
        out_specs=pl.BlockSpec((nb, chunk, width), lambda b, c: (b, c, 0)),
        out_shape=jax.ShapeDtypeStruct((bsz, s, width), BF16),
        scratch_shapes=[pltpu.VMEM((nb, SUBLANES + chunk, w2), F32),
                        pltpu.VMEM((nb, chunk, w2), F32),
                        pltpu.VMEM((nb * nh, dh, dh), F32),
                        pltpu.VMEM((nb * nh, 1, dh), F32),
                        pltpu.VMEM((nb * nh, 1, LANES), F32)],
        compiler_params=_cparams("parallel", "arbitrary"),
        name="mlstm",
    )(qk, v, osig, gcol, grow, bias_c, bias_r, conv_w, norm_w)


def _layer_norm(x, g, b):
    mu = jnp.mean(x, axis=-1, keepdims=True)
    var = jnp.mean(jnp.square(x - mu), axis=-1, keepdims=True)
    return (x - mu) * lax.rsqrt(var + LN_EPS) * g + b


def _mix_kernel(yp_ref, u_ref, yb_ref, ga_ref, gb_ref, x_ref, d_ref, wg_ref, bg_ref, wa_ref, wb_ref, wo_ref,
                lg_ref, lb_ref, h_ref, ht_ref, *, alpha):
    y = jax.nn.gelu(yp_ref[...] + d_ref[...] * u_ref[...])
    gate = jax.nn.sigmoid(jnp.dot(y.astype(BF16), wg_ref[...], preferred_element_type=F32) + bg_ref[...])
    ya = (y * gate).astype(BF16)
    merged = (ga_ref[...].astype(F32) * jnp.dot(ya, wa_ref[...], preferred_element_type=F32)
              + gb_ref[...].astype(F32) * jnp.dot(yb_ref[...], wb_ref[...], preferred_element_type=F32))
    mix = jnp.dot(merged.astype(BF16), wo_ref[...], preferred_element_type=F32)
    h = _layer_norm(alpha * x_ref[...] + mix, lg_ref[...], lb_ref[...])
    h_ref[...] = h
    ht_ref[...] = h.T.astype(BF16)


def _mix(y_pre, u, y_b, gates, x, d, w_glu, b_glu, w_a, w_b, w_o, ln_g, ln_b, alpha, tm=256):
    t, dm = x.shape
    w = u.shape[1]
    assert t % tm == 0
    row = lambda width: pl.BlockSpec((tm, width), lambda i: (i, 0))
    return pl.pallas_call(
        functools.partial(_mix_kernel, alpha=alpha),
        grid=(t // tm,),
        in_specs=[row(w), row(w), row(w),
                  pl.BlockSpec((tm, dm), lambda i: (i, 0)),
                  pl.BlockSpec((tm, dm), lambda i: (i, 1)),
                  row(dm),
                  _const_spec((1, w)), _const_spec((w, w)), _const_spec((1, w)),
                  _const_spec((w, dm)), _const_spec((w, dm)), _const_spec((dm, dm)),
                  _const_spec((1, dm)), _const_spec((1, dm))],
        out_specs=[pl.BlockSpec((tm, dm), lambda i: (i, 0)),
                   pl.BlockSpec((dm, tm), lambda i: (0, i))],
        out_shape=[jax.ShapeDtypeStruct((t, dm), F32), jax.ShapeDtypeStruct((dm, t), BF16)],
        compiler_params=_cparams("parallel"),
        name="mix_ln1",
    )(y_pre, u, y_b, gates, gates, x, d, w_glu, b_glu, w_a, w_b, w_o, ln_g, ln_b)


def _oddeven_merge(lo, hi, r):
    step = r * 2
    if step < hi - lo:
        yield from _oddeven_merge(lo, hi, step)
        yield from _oddeven_merge(lo + r, hi, step)
        yield from [(i, i + r) for i in range(lo + r, hi - r, step)]
    else:
        yield (lo, lo + r)


def _oddeven_merge_sort(lo, hi):
    if hi - lo >= 1:
        mid = lo + (hi - lo) // 2
        yield from _oddeven_merge_sort(lo, mid)
        yield from _oddeven_merge_sort(mid + 1, hi)
        yield from _oddeven_merge(lo, hi, 1)


def _sorted_topk_rows(arrays, dst_refs, k):
    g = arrays[0].shape[0] // SUBLANES
    ws = [[s[j * SUBLANES:(j + 1) * SUBLANES, :] for j in range(g)] for s in arrays]
    for i, j in _oddeven_merge_sort(0, g - 1):
        for w in ws:
            w[i], w[j] = jnp.maximum(w[i], w[j]), jnp.minimum(w[i], w[j])
    for r in range(k):
        left = k - 1 - r
        for w, dst_ref in zip(ws, dst_refs):
            mx = jnp.max(w[0], axis=0, keepdims=True)
            dst_ref[r:r + 1, :] = mx
            if left > 0:
                hit = w[0] == mx
                for lvl in range(min(g, left)):
                    below = w[lvl + 1] if lvl + 1 < g else NEG_INF
                    w[lvl] = jnp.where(hit, below, w[lvl])


def _route_kernel(h_ref, wq_ref, keys_ref, cnt_ref, e1_ref, rank_ref, e2_ref, a_ref, b_ref, cand_ref,
                  *, nh, nk, half, topk, ncand_rows):
    q = jnp.dot(h_ref[...].astype(BF16), wq_ref[...], preferred_element_type=F32).astype(BF16)
    tb = q.shape[0]
    nrank = topk + 1
    for h in range(nh):
        s1 = lax.dot_general(keys_ref[2 * h], q[:, (2 * h) * half:(2 * h + 1) * half],
                             (((1,), (1,)), ((), ())), preferred_element_type=F32)
        s2 = lax.dot_general(keys_ref[2 * h + 1], q[:, (2 * h + 1) * half:(2 * h + 2) * half],
                             (((1,), (1,)), ((), ())), preferred_element_type=F32)
        _sorted_topk_rows([s1, s2], [a_ref, b_ref], nrank)
        a = a_ref[0:nrank, :]
        b = b_ref[0:nrank, :]
        cand_ref[...] = jnp.full((ncand_rows, tb), NEG_INF, F32)
        off = 0
        for i in range(nrank):
            n_i = nrank // (i + 1)
            cand_ref[off:off + n_i, :] = a[i:i + 1, :] + b[0:n_i, :]
            off += n_i
        cur = cand_ref[...]
        top = a[0:1, :] + b[0:1, :]
        z = jnp.zeros((1, tb), F32)
        kth = top
        for r in range(topk):
            kth = jnp.max(cur, axis=0, keepdims=True)
            z = z + jnp.exp(kth - top)
            cur = jnp.where(cur == kth, NEG_INF, cur)
        nxt = jnp.max(cur, axis=0, keepdims=True)
        tau = 0.5 * (kth + nxt)
        thr = tau - s1
        cnt = jnp.zeros_like(s1)
        for r in range(topk):
            cnt = jnp.where(b[r:r + 1, :] >= thr, float(r + 1), cnt)
        rank = jnp.full_like(s2, float(nrank))
        for r in reversed(range(nrank)):
            rank = jnp.where(s2 >= b[r:r + 1, :], float(r), rank)
        cnt_ref[h] = cnt
        e1_ref[h] = jnp.exp(s1 - a[0:1, :])
        rank_ref[h] = rank.astype(rank_ref.dtype)
        e2_ref[h] = (jnp.exp(s2 - b[0:1, :]) / z).astype(e2_ref.dtype)


def _route(h, wq, keys2, nh, tb=512):
    t, dm = h.shape
    nk, half = keys2.shape[1], keys2.shape[2]
    topk = PEER_TOPK
    nrank = topk + 1
    ncand = sum(nrank // (r + 1) for r in range(nrank))
    ncand_rows = -(-ncand // SUBLANES) * SUBLANES
    rank_rows = -(-nrank // SUBLANES) * SUBLANES
    tb = min(tb, t)
    assert t % tb == 0 and nk > nrank
    kern = functools.partial(_route_kernel, nh=nh, nk=nk, half=half, topk=topk, ncand_rows=ncand_rows)
    ospec = pl.BlockSpec((nh, nk, tb), lambda i: (0, 0, i))
    return pl.pallas_call(
        kern,
        grid=(t // tb,),
        in_specs=[pl.BlockSpec((tb, dm), lambda i: (i, 0)),
                  _const_spec(wq.shape), _const_spec(keys2.shape)],
        out_specs=[ospec] * 4,
        out_shape=[jax.ShapeDtypeStruct((nh, nk, t), dt) for dt in (F32, F32, BF16, BF16)],
        scratch_shapes=[pltpu.VMEM((rank_rows, tb), F32), pltpu.VMEM((rank_rows, tb), F32),
                        pltpu.VMEM((ncand_rows, tb), F32)],
        compiler_params=_cparams("parallel"),
        name="peer_route",
    )(h, wq, keys2)


_GELU_K1 = -2.0 * math.sqrt(2.0 / math.pi) * math.log2(math.e)
_GELU_K2 = _GELU_K1 * 0.044715


def _gelu_tanh(x):
    return x / (1.0 + jnp.exp2(x * (_GELU_K1 + _GELU_K2 * (x * x))))


def _expert_kernel(xt_ref, u_ref, vt_ref, cnt_ref, e1_ref, rank_ref, e2_ref, o_ref, sa_ref, sb_ref,
                   *, nh, nk, rows, sub):
    e = pl.program_id(1)

    @pl.when(e == 0)
    def _():
        o_ref[...] = jnp.zeros_like(o_ref)
        sa_ref[...] = jnp.zeros_like(sa_ref)

    def step(rd_ref, wr_ref):
        for q in range(xt_ref.shape[1] // sub):
            cs = slice(q * sub, (q + 1) * sub)
            wr_ref[:, cs] = jnp.dot(u_ref[...], xt_ref[:, cs], preferred_element_type=F32)
            parts = []
            for r in range(rows):
                g = None
                for h in range(nh):
                    sel = jnp.where(rank_ref[h, :, cs] < cnt_ref[h, r:r + 1, cs].astype(BF16),
                                    e2_ref[h, :, cs] * e1_ref[h, r:r + 1, cs].astype(BF16), 0.0)
                    g = sel if g is None else g + sel
                parts.append(_gelu_tanh(rd_ref[r * nk:(r + 1) * nk, cs]).astype(BF16) * g)
            act = jnp.concatenate(parts, axis=0)
            o_ref[:, cs] += jnp.dot(vt_ref[...], act, preferred_element_type=F32)

    @pl.when(lax.rem(e, 2) == 0)
    def _():
        step(sa_ref, sb_ref)

    @pl.when(lax.rem(e, 2) == 1)
    def _():
        step(sb_ref, sa_ref)


def _experts(xt, u_tab, vt_tab, cnt, e1, rank, e2, tb=1024, rows=8, sub=256):
    dm, t = xt.shape
    ne = u_tab.shape[0]
    nh, nk, _ = cnt.shape
    neb = rows * nk
    tb = min(tb, t)
    assert t % tb == 0 and tb % sub == 0 and ne % neb == 0 and ne == nk * nk
    nblk = ne // neb
    kern = functools.partial(_expert_kernel, nh=nh, nk=nk, rows=rows, sub=sub)
    once = pl.Buffered(1)
    score_blk = lambda e: jnp.minimum(e, nblk - 1)
    gate_blk = lambda e: jnp.maximum(e - 1, 0)
    return pl.pallas_call(
        kern,
        grid=(t // tb, nblk + 1),
        in_specs=[pl.BlockSpec((dm, tb), lambda i, e: (0, i), pipeline_mode=once),
                  pl.BlockSpec((neb, dm), lambda i, e: (score_blk(e), 0)),
                  pl.BlockSpec((dm, neb), lambda i, e: (0, gate_blk(e))),
                  pl.BlockSpec((nh, rows, tb), lambda i, e: (0, gate_blk(e), i)),
                  pl.BlockSpec((nh, rows, tb), lambda i, e: (0, gate_blk(e), i)),
                  pl.BlockSpec((nh, nk, tb), lambda i, e: (0, 0, i), pipeline_mode=once),
                  pl.BlockSpec((nh, nk, tb), lambda i, e: (0, 0, i), pipeline_mode=once)],
        out_specs=pl.BlockSpec((dm, tb), lambda i, e: (0, i), pipeline_mode=once),
        out_shape=jax.ShapeDtypeStruct((dm, t), F32),
        scratch_shapes=[pltpu.VMEM((neb, tb), F32), pltpu.VMEM((neb, tb), F32)],
        compiler_params=_cparams("parallel", "arbitrary"),
        name="peer_experts",
    )(xt, u_tab, vt_tab, cnt, e1, rank, e2)


def _final_kernel(h_ref, yt_ref, p_ref, lg_ref, lb_ref, wg_ref, wp_ref, o_ref, *, alpha):
    h = _layer_norm(alpha * h_ref[...] + yt_ref[...].T, lg_ref[...], lb_ref[...])
    gate = jax.nn.sigmoid(jnp.dot(h.astype(BF16), wg_ref[...], preferred_element_type=F32))
    proj = jnp.dot(p_ref[...].astype(BF16), wp_ref[...], preferred_element_type=F32)
    o_ref[...] = h + gate * proj


def _final(h, yt, p, ln_g, ln_b, w_gate, w_proj, alpha, tm=512):
    t, dm = h.shape
    pd = p.shape[1]
    assert t % tm == 0
    return pl.pallas_call(
        functools.partial(_final_kernel, alpha=alpha),
        grid=(t // tm,),
        in_specs=[pl.BlockSpec((tm, dm), lambda i: (i, 0)),
                  pl.BlockSpec((dm, tm), lambda i: (0, i)),
                  pl.BlockSpec((tm, pd), lambda i: (i, 0)),
                  _const_spec((1, dm)), _const_spec((1, dm)),
                  _const_spec((dm, dm)), _const_spec((pd, dm))],
        out_specs=pl.BlockSpec((tm, dm), lambda i: (i, 0)),
        out_shape=jax.ShapeDtypeStruct((t, dm), F32),
        compiler_params=_cparams("parallel"),
        name="ln2_ple",
    )(h, yt, p, ln_g, ln_b, w_gate, w_proj)


def _layer(h, p, w_in, b_igate, b_fgate, conv_qk, mh_norm_w, a_re, a_im, log_dt, b_re, b_im, c_re, c_im, d_skip,
           w_glu, b_glu, w_up_ssm, w_up_ml, w_out, ln1_g, ln1_b, peer_wq, peer_keys, peer_u, peer_v,
           ln2_g, ln2_b, ple_w_gate, ple_w_proj, alpha):
    bsz, s, dm = h.shape
    t = bsz * s
    nh = b_igate.shape[0]
    ssm_w = d_skip.shape[0]
    ml_w = mh_norm_w.shape[0]
    x2 = h.reshape(t, dm)
    xb = x2.astype(BF16)

    o0 = ssm_w
    o1 = o0 + 2 * ml_w
    o2 = o1 + ml_w
    o3 = o2 + ml_w
    o4 = o3 + 2 * nh
    u, qk, v, osig, gates = _in_proj(
        xb, [w_in[:, :o3].astype(BF16), w_in[:, o4:].astype(BF16)],
        [(0, ssm_w, F32, False), (0, 2 * ml_w, F32, False), (0, ml_w, BF16, False), (0, ml_w, BF16, True),
         (1, w_in.shape[1] - o4, BF16, True)])
    w_if = jnp.zeros((dm, LANES), F32).at[:, :2 * nh].set(w_in[:, o3:o4]).astype(BF16)
    gif = _matmul(xb, w_if, F32)

    bb, cb, lam = _s5_params(a_re, a_im, log_dt, b_re, b_im, c_re, c_im)
    y_pre = _s5_scan(u.reshape(bsz, s, ssm_w), bb, cb, lam)

    gcol = gif.reshape(bsz, s, LANES)
    grow = jnp.swapaxes(gcol[:, :, :2 * nh], 1, 2)
    bias = jnp.concatenate([b_igate, b_fgate])
    bias_c = jnp.zeros((1, LANES), F32).at[0, :2 * nh].set(bias)
    y_b = _mlstm(qk.reshape(bsz, s, 2 * ml_w), v.reshape(bsz, s, ml_w), osig.reshape(bsz, s, ml_w),
                 gcol, grow, bias_c, bias[:, None], conv_qk, mh_norm_w[None, :], nh)

    h1, h1t = _mix(y_pre.reshape(t, ssm_w), u, y_b.reshape(t, ml_w), gates, x2, d_skip[None, :],
                   w_glu.astype(BF16), b_glu[None, :], w_up_ssm.astype(BF16), w_up_ml.astype(BF16),
                   w_out.astype(BF16), ln1_g[None, :], ln1_b[None, :], alpha)

    ph, _, nk, half = peer_keys.shape
    keys2 = peer_keys.reshape(ph * 2, nk, half).astype(BF16)
    cnt, e1, rank, e2 = _route(h1, peer_wq.astype(BF16), keys2, ph)
    y2t = _experts(h1t, peer_u.astype(BF16), peer_v.T.astype(BF16), cnt, e1, rank, e2)

    out = _final(h1, y2t, p.reshape(t, -1), ln2_g[None, :], ln2_b[None, :],
                 ple_w_gate.astype(BF16), ple_w_proj.astype(BF16), alpha)
    return out.reshape(bsz, s, dm)


def kernel(x, p, w_in, b_igate, b_fgate, conv_qk, mh_norm_w, ssm_a_re, ssm_a_im, ssm_log_dt, ssm_b_re, ssm_b_im,
           ssm_c_re, ssm_c_im, ssm_d, w_glu, b_glu, w_up_ssm, w_up_ml, w_out, ln1_g, ln1_b, peer_wq, peer_keys,
           peer_u, peer_v, ln2_g, ln2_b, ple_w_gate, ple_w_proj):
    depth = w_in.shape[0]
    alpha = (2 * depth) ** 0.25
    h = x
    for i in range(depth):
        h = _layer(h, p[i], w_in[i], b_igate[i], b_fgate[i], conv_qk[i], mh_norm_w[i], ssm_a_re[i], ssm_a_im[i],
                   ssm_log_dt[i], ssm_b_re[i], ssm_b_im[i], ssm_c_re[i], ssm_c_im[i], ssm_d[i], w_glu[i], b_glu[i],
                   w_up_ssm[i], w_up_ml[i], w_out[i], ln1_g[i], ln1_b[i], peer_wq[i], peer_keys[i], peer_u[i],
                   peer_v[i], ln2_g[i], ln2_b[i], ple_w_gate[i], ple_w_proj[i], alpha)
    return h
```

```python
import functools
import math

import jax
import jax.numpy as jnp
from jax import lax
from jax.experimental import pallas as pl
from jax.experimental.pallas import tpu as pltpu

F32 = jnp.float32
BF16 = jnp.bfloat16

LN_EPS = 1e-5
PEER_TOPK = 16
LANES = 128
SUBLANES = 8
VMEM_LIMIT = 56 * 1024 * 1024

ML_CHUNK = 256
ML_BATCH = 4
S5_CHUNK = 512
S5_PITCH = S5_CHUNK + SUBLANES
NEG_INF = float("-inf")


def _cparams(*sem):
    return pltpu.CompilerParams(dimension_semantics=sem, vmem_limit_bytes=VMEM_LIMIT)


def _const_spec(shape):
    nd = len(shape)
    return pl.BlockSpec(shape, lambda *_: (0,) * nd, pipeline_mode=pl.Buffered(1))


def _mm_kernel(a_ref, w_ref, o_ref):
    o_ref[...] = jnp.dot(a_ref[...], w_ref[...], preferred_element_type=F32).astype(o_ref.dtype)


def _matmul(a, w, out_dtype, tm=1024, tn=512):
    m, k = a.shape
    n = w.shape[1]
    tm, tn = min(tm, m), min(tn, n)
    assert m % tm == 0 and n % tn == 0
    return pl.pallas_call(
        _mm_kernel,
        grid=(m // tm, n // tn),
        in_specs=[pl.BlockSpec((tm, k), lambda i, j: (i, 0)),
                  pl.BlockSpec((k, tn), lambda i, j: (0, j))],
        out_specs=pl.BlockSpec((tm, tn), lambda i, j: (i, j)),
        out_shape=jax.ShapeDtypeStruct((m, n), out_dtype),
        compiler_params=_cparams("parallel", "parallel"),
        name="gate_proj",
    )(a, w)


def _in_proj_kernel(a_ref, *refs, segs, nw, sub):
    j = pl.program_id(1)
    w_refs, o_refs = refs[:nw], refs[nw:]
    for o_ref, (lo, hi, wi, sig) in zip(o_refs, segs):
        @pl.when((j >= lo) & (j < hi))
        def _(o_ref=o_ref, w_ref=w_refs[wi], sig=sig):
            for q in range(o_ref.shape[1] // sub):
                cs = slice(q * sub, (q + 1) * sub)
                acc = jnp.dot(a_ref[...], w_ref[:, cs], preferred_element_type=F32)
                o_ref[:, cs] = (jax.nn.sigmoid(acc) if sig else acc).astype(o_ref.dtype)


def _in_proj(a, weights, outs, tm=1024, tn=1024, sub=256):
    m, k = a.shape
    tm = min(tm, m)
    assert m % tm == 0 and all(o[1] % tn == 0 for o in outs)
    segs, ospecs, shapes, step = [], [], [], 0
    wlo = [None] * len(weights)
    wnb = [0] * len(weights)
    for wi, width, dtype, sig in outs:
        nblk = width // tn
        if wlo[wi] is None:
            wlo[wi] = step
        assert wlo[wi] + wnb[wi] == step, "outputs sharing a weight must be adjacent"
        wnb[wi] += nblk
        segs.append((step, step + nblk, wi, sig))
        ospecs.append(pl.BlockSpec((tm, tn), lambda i, j, lo=step, nblk=nblk: (i, jnp.clip(j - lo, 0, nblk - 1))))
        shapes.append(jax.ShapeDtypeStruct((m, width), dtype))
        step += nblk
    assert all(w.shape == (k, nb * tn) for w, nb in zip(weights, wnb))
    wspecs = [pl.BlockSpec((k, tn), lambda i, j, lo=lo, nb=nb: (0, jnp.clip(j - lo, 0, nb - 1)))
              for lo, nb in zip(wlo, wnb)]
    return pl.pallas_call(
        functools.partial(_in_proj_kernel, segs=tuple(segs), nw=len(weights), sub=sub),
        grid=(m // tm, step),
        in_specs=[pl.BlockSpec((tm, k), lambda i, j: (i, 0))] + wspecs,
        out_specs=ospecs,
        out_shape=shapes,
        compiler_params=_cparams("parallel", "arbitrary"),
        name="in_proj",
    )(a, *weights)


def _s5_kernel(u_ref, bb_ref, cb_ref, lam_ref, y_ref, bu_ref, st_ref, *, bsz, chunk, pitch, ntile):
    c = pl.program_id(1)

    @pl.when(c == 0)
    def _():
        st_ref[...] = jnp.zeros_like(st_ref)

    bb = bb_ref[0]
    for b in range(bsz):
        bu = jnp.dot(u_ref[b].astype(BF16), bb, preferred_element_type=F32)
        for k in range(2 * ntile):
            bu_ref[k, b * pitch:b * pitch + chunk, :] = bu[:, k * LANES:(k + 1) * LANES]

    lam = lam_ref[0]
    lr = [jnp.broadcast_to(lam[k:k + 1, :], (bsz, LANES)) for k in range(ntile)]
    li = [jnp.broadcast_to(lam[ntile + k:ntile + k + 1, :], (bsz, LANES)) for k in range(ntile)]

    def step(t, carry):
        new = []
        for k in range(ntile):
            sr, si = carry[2 * k], carry[2 * k + 1]
            rows = pl.ds(t, bsz, stride=pitch)
            nr = lr[k] * sr - li[k] * si + bu_ref[k, rows, :]
            ni = lr[k] * si + li[k] * sr + bu_ref[ntile + k, rows, :]
            bu_ref[k, rows, :] = nr
            bu_ref[ntile + k, rows, :] = ni
            new += [nr, ni]
        return tuple(new)

    init = []
    for k in range(ntile):
        init += [st_ref[k], st_ref[ntile + k]]
    fin = lax.fori_loop(0, chunk, step, tuple(init), unroll=2)
    for k in range(ntile):
        st_ref[k] = fin[2 * k]
        st_ref[ntile + k] = fin[2 * k + 1]

    cb = cb_ref[0]
    for b in range(bsz):
        st = jnp.concatenate([bu_ref[k, b * pitch:b * pitch + chunk, :] for k in range(2 * ntile)], axis=1)
        y_ref[b] = jnp.dot(st.astype(BF16), cb, preferred_element_type=F32)


def _s5_scan(u, bb, cb, lam):
    bsz, s, w = u.shape
    nslab = w // LANES
    ntile = lam.shape[1] // 2
    chunk, pitch = S5_CHUNK, S5_PITCH
    assert s % chunk == 0 and bsz <= SUBLANES
    kern = functools.partial(_s5_kernel, bsz=bsz, chunk=chunk, pitch=pitch, ntile=ntile)
    return pl.pallas_call(
        kern,
        grid=(nslab, s // chunk),
        in_specs=[pl.BlockSpec((bsz, chunk, LANES), lambda j, c: (0, c, j)),
                  pl.BlockSpec((1, LANES, 2 * ntile * LANES), lambda j, c: (j, 0, 0)),
                  pl.BlockSpec((1, 2 * ntile * LANES, LANES), lambda j, c: (j, 0, 0)),
                  pl.BlockSpec((1, 2 * ntile, LANES), lambda j, c: (j, 0, 0))],
        out_specs=pl.BlockSpec((bsz, chunk, LANES), lambda j, c: (0, c, j)),
        out_shape=jax.ShapeDtypeStruct((bsz, s, w), F32),
        scratch_shapes=[pltpu.VMEM((2 * ntile, bsz * pitch, LANES), F32),
                        pltpu.VMEM((2 * ntile, bsz, LANES), F32)],
        compiler_params=_cparams("parallel", "arbitrary"),
        name="s5_scan",
    )(u, bb, cb, lam)


def _s5_params(a_re, a_im, log_dt, b_re, b_im, c_re, c_im):
    g, p = a_re.shape
    gc = b_re.shape[2]
    gps = LANES // gc
    nslab = g // gps
    dt = jnp.exp(log_dt)[:, None]
    mag = jnp.exp(dt * a_re)
    lam_r = mag * jnp.cos(dt * a_im)
    lam_i = mag * jnp.sin(dt * a_im)
    den = a_re * a_re + a_im * a_im
    zr = ((lam_r - 1.0) * a_re + lam_i * a_im) / den
    zi = (lam_i * a_re - (lam_r - 1.0) * a_im) / den
    bbar_r = zr[..., None] * b_re - zi[..., None] * b_im
    bbar_i = zr[..., None] * b_im + zi[..., None] * b_re
    eye = jnp.eye(gps, dtype=F32)

    def in_slab(bm):
        bm = bm.reshape(nslab, gps, p, gc)
        return jnp.einsum("jgpc,gh->jgchp", bm, eye).reshape(nslab, gps * gc, gps * p)

    def out_slab(cm):
        cm = cm.reshape(nslab, gps, gc, p)
        return jnp.einsum("jgcp,gh->jhpgc", cm, eye).reshape(nslab, gps * p, gps * gc)

    bb = jnp.concatenate([in_slab(bbar_r), in_slab(bbar_i)], axis=2).astype(BF16)
    cb = jnp.concatenate([out_slab(c_re), -out_slab(c_im)], axis=1).astype(BF16)
    ntile = gps * p // LANES
    lam = jnp.concatenate([lam_r.reshape(nslab, ntile, LANES), lam_i.reshape(nslab, ntile, LANES)], axis=1)
    return bb, cb, lam


def _log_sigmoid(x):
    return jnp.minimum(x, 0.0) - jnp.log1p(jnp.exp(-jnp.abs(x)))


def _mlstm_kernel(qk_ref, v_ref, o_ref, gc_ref, gr_ref, bc_ref, br_ref, cw_ref, nw_ref, y_ref,
                  buf_ref, qs_ref, c_ref, n_ref, m_ref, *, nb, nh, dh, chunk, kconv):
    ci = pl.program_id(1)
    width = nh * dh
    halo = SUBLANES

    @pl.when(ci == 0)
    def _():
        buf_ref[:, 0:halo, :] = jnp.zeros((nb, halo, 2 * width), F32)
        c_ref[...] = jnp.zeros_like(c_ref)
        n_ref[...] = jnp.zeros_like(n_ref)
        m_ref[...] = jnp.zeros_like(m_ref)

    tt = lax.broadcasted_iota(jnp.int32, (chunk, chunk), 0)
    ss = lax.broadcasted_iota(jnp.int32, (chunk, chunk), 1)
    causal = ss <= tt
    tril = causal.astype(F32)
    triu = (tt <= ss).astype(F32)

    gates = []
    for bi in range(nb):
        buf_ref[bi, halo:halo + chunk, :] = qk_ref[bi]
        base = halo - (kconv - 1)
        conv = cw_ref[0:1, :] * buf_ref[bi, base:base + chunk, :]
        for j in range(1, kconv):
            conv = conv + cw_ref[j:j + 1, :] * buf_ref[bi, base + j:base + j + chunk, :]
        buf_ref[bi, 0:halo, :] = buf_ref[bi, chunk:chunk + halo, :]
        qs_ref[bi] = conv * jax.nn.sigmoid(conv)

        gcol = gc_ref[bi] + bc_ref[...]
        col_id = lax.broadcasted_iota(jnp.int32, gcol.shape, 1)
        lcol = jnp.where(col_id >= nh, _log_sigmoid(gcol), gcol)
        grow = gr_ref[bi] + br_ref[...]
        row_id = lax.broadcasted_iota(jnp.int32, grow.shape, 0)
        lrow = jnp.where(row_id >= nh, _log_sigmoid(grow), grow)
        bcol = jnp.dot(tril, lcol, preferred_element_type=F32, precision=lax.Precision.HIGHEST)
        brow = jnp.dot(lrow, triu, preferred_element_type=F32, precision=lax.Precision.HIGHEST)
        gates.append((lcol, lrow, bcol, brow))

    inv_sqrt = 1.0 / math.sqrt(dh)
    for h in range(nh):
        hs = slice(h * dh, (h + 1) * dh)
        ks = slice(width + h * dh, width + (h + 1) * dh)
        bs = range(nb)
        st = [bi * nh + h for bi in bs]
        qf = [qs_ref[bi, :, hs] for bi in bs]
        q = [x.astype(BF16) for x in qf]
        kf = [qs_ref[bi, :, ks] * inv_sqrt for bi in bs]
        k = [x.astype(BF16) for x in kf]
        v = [v_ref[bi, :, hs] for bi in bs]
        b_c = [gates[bi][2][:, nh + h:nh + h + 1] for bi in bs]
        i_c = [gates[bi][0][:, h:h + 1] for bi in bs]
        b_r = [gates[bi][3][nh + h:nh + h + 1, :] for bi in bs]
        i_r = [gates[bi][1][h:h + 1, :] for bi in bs]
        g_tot = [x[chunk - 1:chunk, :] for x in b_c]
        m_prev = [m_ref[s_][:, 0:1] for s_ in st]
        c_prev = [c_ref[s_] for s_ in st]
        n_prev = [n_ref[s_] for s_ in st]

        qk_t = [lax.dot_general(q[bi], k[bi], (((1,), (1,)), ((), ())), preferred_element_type=F32) for bi in bs]
        q_c = [jnp.dot(q[bi], c_prev[bi].astype(BF16), preferred_element_type=F32) for bi in bs]
        dmat = [jnp.where(causal, b_c[bi] - b_r[bi] + i_r[bi], NEG_INF) for bi in bs]
        inter_log = [b_c[bi] + m_prev[bi] for bi in bs]
        m_row = [jnp.maximum(inter_log[bi], jnp.max(dmat[bi], axis=1, keepdims=True)) for bi in bs]
        sc = [qk_t[bi] * jnp.exp(dmat[bi] - m_row[bi]) for bi in bs]
        inter_scale = [jnp.exp(inter_log[bi] - m_row[bi]) for bi in bs]
        sc_v = [jnp.dot(sc[bi].astype(BF16), v[bi], preferred_element_type=F32) for bi in bs]

        wlog = [g_tot[bi] - b_c[bi] + i_c[bi] for bi in bs]
        m_loc = [jnp.max(x, axis=0, keepdims=True) for x in wlog]
        kw = [kf[bi] * jnp.exp(wlog[bi] - m_loc[bi]) for bi in bs]
        c_chunk = [lax.dot_general(kw[bi].astype(BF16), v[bi], (((0,), (0,)), ((), ())),
                                   preferred_element_type=F32) for bi in bs]

        num = [sc_v[bi] + inter_scale[bi] * q_c[bi] for bi in bs]
        qn = [jnp.sum(qf[bi] * n_prev[bi], axis=1, keepdims=True) for bi in bs]
        den = [jnp.sum(sc[bi], axis=1, keepdims=True) + inter_scale[bi] * qn[bi] for bi in bs]
        hh = [num[bi] / jnp.maximum(jnp.abs(den[bi]), jnp.exp(-m_row[bi])) for bi in bs]
        mu = [jnp.mean(x, axis=1, keepdims=True) for x in hh]
        var = [jnp.mean(jnp.square(hh[bi] - mu[bi]), axis=1, keepdims=True) for bi in bs]
        for bi in bs:
            hn = (hh[bi] - mu[bi]) * lax.rsqrt(var[bi] + LN_EPS) * nw_ref[:, hs]
            y_ref[bi, :, hs] = (o_ref[bi, :, hs].astype(F32) * hn).astype(y_ref.dtype)

        for bi in bs:
            n_chunk = jnp.sum(kw[bi], axis=0, keepdims=True)
            m_new = jnp.maximum(g_tot[bi] + m_prev[bi], m_loc[bi])
            a = jnp.exp(g_tot[bi] + m_prev[bi] - m_new)
            bb = jnp.exp(m_loc[bi] - m_new)
            c_ref[st[bi]] = a * c_prev[bi] + bb * c_chunk[bi]
            n_ref[st[bi]] = a * n_prev[bi] + bb * n_chunk
            m_ref[st[bi]] = jnp.broadcast_to(m_new, (1, LANES))


def _mlstm(qk, v, osig, gcol, grow, bias_c, bias_r, conv_w, norm_w, nh, nb=ML_BATCH):
    bsz, s, w2 = qk.shape
    width = w2 // 2
    dh = width // nh
    chunk = ML_CHUNK
    kconv = conv_w.shape[0]
    nb = math.gcd(nb, bsz)
    assert s % chunk == 0 and kconv - 1 <= SUBLANES
    kern = functools.partial(_mlstm_kernel, nb=nb, nh=nh, dh=dh, chunk=chunk, kconv=kconv)
    return pl.pallas_call(
        kern,
        grid=(bsz // nb, s // chunk),
        in_specs=[pl.BlockSpec((nb, chunk, w2), lambda b, c: (b, c, 0)),
                  pl.BlockSpec((nb, chunk, width), lambda b, c: (b, c, 0)),
                  pl.BlockSpec((nb, chunk, width), lambda b, c: (b, c, 0)),
                  pl.BlockSpec((nb, chunk, LANES), lambda b, c: (b, c, 0)),
                  pl.BlockSpec((nb, 2 * nh, chunk), lambda b, c: (b, 0, c)),
                  _const_spec((1, LANES)),
                  _const_spec((2 * nh, 1)),
                  _const_spec((kconv, w2)),
                  _const_spec((1, width))],
        out_specs=pl.BlockSpec((nb, chunk, width), lambda b, c: (b, c, 0)),
        out_shape=jax.ShapeDtypeStruct((bsz, s, width), BF16),
        scratch_shapes=[pltpu.VMEM((nb, SUBLANES + chunk, w2), F32),
                        pltpu.VMEM((nb, chunk, w2), F32),
                        pltpu.VMEM((nb * nh, dh, dh), F32),
                        pltpu.VMEM((nb * nh, 1, dh), F32),
                        pltpu.VMEM((nb * nh, 1, LANES), F32)],
        compiler_params=_cparams("parallel", "arbitrary"),
        name="mlstm",
    )(qk, v, osig, gcol, grow, bias_c, bias_r, conv_w, norm_w)


def _layer_norm(x, g, b):
    mu = jnp.mean(x, axis=-1, keepdims=True)
    var = jnp.mean(jnp.square(x - mu), axis=-1, keepdims=True)
    return (x - mu) * lax.rsqrt(var + LN_EPS) * g + b


def _mix_kernel(yp_ref, u_ref, yb_ref, ga_ref, gb_ref, x_ref, d_ref, wg_ref, bg_ref, wa_ref, wb_ref, wo_ref,
                lg_ref, lb_ref, h_ref, ht_ref, *, alpha):
    y = jax.nn.gelu(yp_ref[...] + d_ref[...] * u_ref[...])
    gate = jax.nn.sigmoid(jnp.dot(y.astype(BF16), wg_ref[...], preferred_element_type=F32) + bg_ref[...])
    ya = (y * gate).astype(BF16)
    merged = (ga_ref[...].astype(F32) * jnp.dot(ya, wa_ref[...], preferred_element_type=F32)
              + gb_ref[...].astype(F32) * jnp.dot(yb_ref[...], wb_ref[...], preferred_element_type=F32))
    mix = jnp.dot(merged.astype(BF16), wo_ref[...], preferred_element_type=F32)
    h = _layer_norm(alpha * x_ref[...] + mix, lg_ref[...], lb_ref[...])
    h_ref[...] = h
    ht_ref[...] = h.T.astype(BF16)


def _mix(y_pre, u, y_b, gates, x, d, w_glu, b_glu, w_a, w_b, w_o, ln_g, ln_b, alpha, tm=256):
    t, dm = x.shape
    w = u.shape[1]
    assert t % tm == 0
    row = lambda width: pl.BlockSpec((tm, width), lambda i: (i, 0))
    return pl.pallas_call(
        functools.partial(_mix_kernel, alpha=alpha),
        grid=(t // tm,),
        in_specs=[row(w), row(w), row(w),
                  pl.BlockSpec((tm, dm), lambda i: (i, 0)),
                  pl.BlockSpec((tm, dm), lambda i: (i, 1)),
                  row(dm),
                  _const_spec((1, w)), _const_spec((w, w)), _const_spec((1, w)),
                  _const_spec((w, dm)), _const_spec((w, dm)), _const_spec((dm, dm)),
                  _const_spec((1, dm)), _const_spec((1, dm))],
        out_specs=[pl.BlockSpec((tm, dm), lambda i: (i, 0)),
                   pl.BlockSpec((dm, tm), lambda i: (0, i))],
        out_shape=[jax.ShapeDtypeStruct((t, dm), F32), jax.ShapeDtypeStruct((dm, t), BF16)],
        compiler_params=_cparams("parallel"),
        name="mix_ln1",
    )(y_pre, u, y_b, gates, gates, x, d, w_glu, b_glu, w_a, w_b, w_o, ln_g, ln_b)


def _oddeven_merge(lo, hi, r):
    step = r * 2
    if step < hi - lo:
        yield from _oddeven_merge(lo, hi, step)
        yield from _oddeven_merge(lo + r, hi, step)
        yield from [(i, i + r) for i in range(lo + r, hi - r, step)]
    else:
        yield (lo, lo + r)


def _oddeven_merge_sort(lo, hi):
    if hi - lo >= 1:
        mid = lo + (hi - lo) // 2
        yield from _oddeven_merge_sort(lo, mid)
        yield from _oddeven_merge_sort(mid + 1, hi)
        yield from _oddeven_merge(lo, hi, 1)


def _sorted_topk_rows(arrays, dst_refs, k):
    g = arrays[0].shape[0] // SUBLANES
    ws = [[s[j * SUBLANES:(j + 1) * SUBLANES, :] for j in range(g)] for s in arrays]
    for i, j in _oddeven_merge_sort(0, g - 1):
        for w in ws:
            w[i], w[j] = jnp.maximum(w[i], w[j]), jnp.minimum(w[i], w[j])
    for r in range(k):
        left = k - 1 - r
        for w, dst_ref in zip(ws, dst_refs):
            mx = jnp.max(w[0], axis=0, keepdims=True)
            dst_ref[r:r + 1, :] = mx
            if left > 0:
                hit = w[0] == mx
                for lvl in range(min(g, left)):
                    below = w[lvl + 1] if lvl + 1 < g else NEG_INF
                    w[lvl] = jnp.where(hit, below, w[lvl])


def _route_kernel(h_ref, wq_ref, keys_ref, cnt_ref, e1_ref, rank_ref, e2_ref, a_ref, b_ref, cand_ref,
                  *, nh, nk, half, topk, ncand_rows):
    q = jnp.dot(h_ref[...].astype(BF16), wq_ref[...], preferred_element_type=F32).astype(BF16)
    tb = q.shape[0]
    nrank = topk + 1
    for h in range(nh):
        s1 = lax.dot_general(keys_ref[2 * h], q[:, (2 * h) * half:(2 * h + 1) * half],
                             (((1,), (1,)), ((), ())), preferred_element_type=F32)
        s2 = lax.dot_general(keys_ref[2 * h + 1], q[:, (2 * h + 1) * half:(2 * h + 2) * half],
                             (((1,), (1,)), ((), ())), preferred_element_type=F32)
        _sorted_topk_rows([s1, s2], [a_ref, b_ref], nrank)
        a = a_ref[0:nrank, :]
        b = b_ref[0:nrank, :]
        cand_ref[...] = jnp.full((ncand_rows, tb), NEG_INF, F32)
        off = 0
        for i in range(nrank):
            n_i = nrank // (i + 1)
            cand_ref[off:off + n_i, :] = a[i:i + 1, :] + b[0:n_i, :]
            off += n_i
        cur = cand_ref[...]
        top = a[0:1, :] + b[0:1, :]
        z = jnp.zeros((1, tb), F32)
        kth = top
        for r in range(topk):
            kth = jnp.max(cur, axis=0, keepdims=True)
            z = z + jnp.exp(kth - top)
            cur = jnp.where(cur == kth, NEG_INF, cur)
        nxt = jnp.max(cur, axis=0, keepdims=True)
        tau = 0.5 * (kth + nxt)
        thr = tau - s1
        cnt = jnp.zeros_like(s1)
        for r in range(topk):
            cnt = jnp.where(b[r:r + 1, :] >= thr, float(r + 1), cnt)
        rank = jnp.full_like(s2, float(nrank))
        for r in reversed(range(nrank)):
            rank = jnp.where(s2 >= b[r:r + 1, :], float(r), rank)
        cnt_ref[h] = cnt
        e1_ref[h] = jnp.exp(s1 - a[0:1, :])
        rank_ref[h] = rank.astype(rank_ref.dtype)
        e2_ref[h] = (jnp.exp(s2 - b[0:1, :]) / z).astype(e2_ref.dtype)


def _route(h, wq, keys2, nh, tb=512):
    t, dm = h.shape
    nk, half = keys2.shape[1], keys2.shape[2]
    topk = PEER_TOPK
    nrank = topk + 1
    ncand = sum(nrank // (r + 1) for r in range(nrank))
    ncand_rows = -(-ncand // SUBLANES) * SUBLANES
    rank_rows = -(-nrank // SUBLANES) * SUBLANES
    tb = min(tb, t)
    assert t % tb == 0 and nk > nrank
    kern = functools.partial(_route_kernel, nh=nh, nk=nk, half=half, topk=topk, ncand_rows=ncand_rows)
    ospec = pl.BlockSpec((nh, nk, tb), lambda i: (0, 0, i))
    return pl.pallas_call(
        kern,
        grid=(t // tb,),
        in_specs=[pl.BlockSpec((tb, dm), lambda i: (i, 0)),
                  _const_spec(wq.shape), _const_spec(keys2.shape)],
        out_specs=[ospec] * 4,
        out_shape=[jax.ShapeDtypeStruct((nh, nk, t), dt) for dt in (F32, F32, BF16, BF16)],
        scratch_shapes=[pltpu.VMEM((rank_rows, tb), F32), pltpu.VMEM((rank_rows, tb), F32),
                        pltpu.VMEM((ncand_rows, tb), F32)],
        compiler_params=_cparams("parallel"),
        name="peer_route",
    )(h, wq, keys2)


_GELU_K1 = -2.0 * math.sqrt(2.0 / math.pi) * math.log2(math.e)
_GELU_K2 = _GELU_K1 * 0.044715


def _gelu_tanh(x):
    return x / (1.0 + jnp.exp2(x * (_GELU_K1 + _GELU_K2 * (x * x))))


def _expert_kernel(xt_ref, u_ref, vt_ref, cnt_ref, e1_ref, rank_ref, e2_ref, o_ref, sa_ref, sb_ref,
                   *, nh, nk, rows, sub):
    e = pl.program_id(1)

    @pl.when(e == 0)
    def _():
        o_ref[...] = jnp.zeros_like(o_ref)
        sa_ref[...] = jnp.zeros_like(sa_ref)

    def step(rd_ref, wr_ref):
        for q in range(xt_ref.shape[1] // sub):
            cs = slice(q * sub, (q + 1) * sub)
            wr_ref[:, cs] = jnp.dot(u_ref[...], xt_ref[:, cs], preferred_element_type=F32)
            parts = []
            for r in range(rows):
                g = None
                for h in range(nh):
                    sel = jnp.where(rank_ref[h, :, cs] < cnt_ref[h, r:r + 1, cs].astype(BF16),
                                    e2_ref[h, :, cs] * e1_ref[h, r:r + 1, cs].astype(BF16), 0.0)
                    g = sel if g is None else g + sel
                parts.append(_gelu_tanh(rd_ref[r * nk:(r + 1) * nk, cs]).astype(BF16) * g)
            act = jnp.concatenate(parts, axis=0)
            o_ref[:, cs] += jnp.dot(vt_ref[...], act, preferred_element_type=F32)

    @pl.when(lax.rem(e, 2) == 0)
    def _():
        step(sa_ref, sb_ref)

    @pl.when(lax.rem(e, 2) == 1)
    def _():
        step(sb_ref, sa_ref)


def _experts(xt, u_tab, vt_tab, cnt, e1, rank, e2, tb=1024, rows=8, sub=256):
    dm, t = xt.shape
    ne = u_tab.shape[0]
    nh, nk, _ = cnt.shape
    neb = rows * nk
    tb = min(tb, t)
    assert t % tb == 0 and tb % sub == 0 and ne % neb == 0 and ne == nk * nk
    nblk = ne // neb
    kern = functools.partial(_expert_kernel, nh=nh, nk=nk, rows=rows, sub=sub)
    once = pl.Buffered(1)
    score_blk = lambda e: jnp.minimum(e, nblk - 1)
    gate_blk = lambda e: jnp.maximum(e - 1, 0)
    return pl.pallas_call(
        kern,
        grid=(t // tb, nblk + 1),
        in_specs=[pl.BlockSpec((dm, tb), lambda i, e: (0, i), pipeline_mode=once),
                  pl.BlockSpec((neb, dm), lambda i, e: (score_blk(e), 0)),
                  pl.BlockSpec((dm, neb), lambda i, e: (0, gate_blk(e))),
                  pl.BlockSpec((nh, rows, tb), lambda i, e: (0, gate_blk(e), i)),
                  pl.BlockSpec((nh, rows, tb), lambda i, e: (0, gate_blk(e), i)),
                  pl.BlockSpec((nh, nk, tb), lambda i, e: (0, 0, i), pipeline_mode=once),
                  pl.BlockSpec((nh, nk, tb), lambda i, e: (0, 0, i), pipeline_mode=once)],
        out_specs=pl.BlockSpec((dm, tb), lambda i, e: (0, i), pipeline_mode=once),
        out_shape=jax.ShapeDtypeStruct((dm, t), F32),
        scratch_shapes=[pltpu.VMEM((neb, tb), F32), pltpu.VMEM((neb, tb), F32)],
        compiler_params=_cparams("parallel", "arbitrary"),
        name="peer_experts",
    )(xt, u_tab, vt_tab, cnt, e1, rank, e2)


def _final_kernel(h_ref, yt_ref, p_ref, lg_ref, lb_ref, wg_ref, wp_ref, o_ref, *, alpha):
    h = _layer_norm(alpha * h_ref[...] + yt_ref[...].T, lg_ref[...], lb_ref[...])
    gate = jax.nn.sigmoid(jnp.dot(h.astype(BF16), wg_ref[...], preferred_element_type=F32))
    proj = jnp.dot(p_ref[...].astype(BF16), wp_ref[...], preferred_element_type=F32)
    o_ref[...] = h + gate * proj


def _final(h, yt, p, ln_g, ln_b, w_gate, w_proj, alpha, tm=512):
    t, dm = h.shape
    pd = p.shape[1]
    assert t % tm == 0
    return pl.pallas_call(
        functools.partial(_final_kernel, alpha=alpha),
        grid=(t // tm,),
        in_specs=[pl.BlockSpec((tm, dm), lambda i: (i, 0)),
                  pl.BlockSpec((dm, tm), lambda i: (0, i)),
                  pl.BlockSpec((tm, pd), lambda i: (i, 0)),
                  _const_spec((1, dm)), _const_spec((1, dm)),
                  _const_spec((dm, dm)), _const_spec((pd, dm))],
        out_specs=pl.BlockSpec((tm, dm), lambda i: (i, 0)),
        out_shape=jax.ShapeDtypeStruct((t, dm), F32),
        compiler_params=_cparams("parallel"),
        name="ln2_ple",
    )(h, yt, p, ln_g, ln_b, w_gate, w_proj)


def _layer(h, p, w_in, b_igate, b_fgate, conv_qk, mh_norm_w, a_re, a_im, log_dt, b_re, b_im, c_re, c_im, d_skip,
           w_glu, b_glu, w_up_ssm, w_up_ml, w_out, ln1_g, ln1_b, peer_wq, peer_keys, peer_u, peer_v,
           ln2_g, ln2_b, ple_w_gate, ple_w_proj, alpha):
    bsz, s, dm = h.shape
    t = bsz * s
    nh = b_igate.shape[0]
    ssm_w = d_skip.shape[0]
    ml_w = mh_norm_w.shape[0]
    x2 = h.reshape(t, dm)
    xb = x2.astype(BF16)

    o0 = ssm_w
    o1 = o0 + 2 * ml_w
    o2 = o1 + ml_w
    o3 = o2 + ml_w
    o4 = o3 + 2 * nh
    u, qk, v, osig, gates = _in_proj(
        xb, [w_in[:, :o3].astype(BF16), w_in[:, o4:].astype(BF16)],
        [(0, ssm_w, F32, False), (0, 2 * ml_w, F32, False), (0, ml_w, BF16, False), (0, ml_w, BF16, True),
         (1, w_in.shape[1] - o4, BF16, True)])
    w_if = jnp.zeros((dm, LANES), F32).at[:, :2 * nh].set(w_in[:, o3:o4]).astype(BF16)
    gif = _matmul(xb, w_if, F32)

    bb, cb, lam = _s5_params(a_re, a_im, log_dt, b_re, b_im, c_re, c_im)
    y_pre = _s5_scan(u.reshape(bsz, s, ssm_w), bb, cb, lam)

    gcol = gif.reshape(bsz, s, LANES)
    grow = jnp.swapaxes(gcol[:, :, :2 * nh], 1, 2)
    bias = jnp.concatenate([b_igate, b_fgate])
    bias_c = jnp.zeros((1, LANES), F32).at[0, :2 * nh].set(bias)
    y_b = _mlstm(qk.reshape(bsz, s, 2 * ml_w), v.reshape(bsz, s, ml_w), osig.reshape(bsz, s, ml_w),
                 gcol, grow, bias_c, bias[:, None], conv_qk, mh_norm_w[None, :], nh)

    h1, h1t = _mix(y_pre.reshape(t, ssm_w), u, y_b.reshape(t, ml_w), gates, x2, d_skip[None, :],
                   w_glu.astype(BF16), b_glu[None, :], w_up_ssm.astype(BF16), w_up_ml.astype(BF16),
                   w_out.astype(BF16), ln1_g[None, :], ln1_b[None, :], alpha)

    ph, _, nk, half = peer_keys.shape
    keys2 = peer_keys.reshape(ph * 2, nk, half).astype(BF16)
    cnt, e1, rank, e2 = _route(h1, peer_wq.astype(BF16), keys2, ph)
    y2t = _experts(h1t, peer_u.astype(BF16), peer_v.T.astype(BF16), cnt, e1, rank, e2)

    out = _final(h1, y2t, p.reshape(t, -1), ln2_g[None, :], ln2_b[None, :],
                 ple_w_gate.astype(BF16), ple_w_proj.astype(BF16), alpha)
    return out.reshape(bsz, s, dm)


def kernel(x, p, w_in, b_igate, b_fgate, conv_qk, mh_norm_w, ssm_a_re, ssm_a_im, ssm_log_dt, ssm_b_re, ssm_b_im,
           ssm_c_re, ssm_c_im, ssm_d, w_glu, b_glu, w_up_ssm, w_up_ml, w_out, ln1_g, ln1_b, peer_wq, peer_keys,
           peer_u, peer_v, ln2_g, ln2_b, ple_w_gate, ple_w_proj):
    depth = w_in.shape[0]
    alpha = (2 * depth) ** 0.25
    h = x
    for i in range(depth):
        h = _layer(h, p[i], w_in[i], b_igate[i], b_fgate[i], conv_qk[i], mh_norm_w[i], ssm_a_re[i], ssm_a_im[i],
                   ssm_log_dt[i], ssm_b_re[i], ssm_b_im[i], ssm_c_re[i], ssm_c_im[i], ssm_d[i], w_glu[i], b_glu[i],
                   w_up_ssm[i], w_up_ml[i], w_out[i], ln1_g[i], ln1_b[i], peer_wq[i], peer_keys[i], peer_u[i],
                   peer_v[i], ln2_g[i], ln2_b[i], ple_w_gate[i], ple_w_proj[i], alpha)
    return h
```

```python
import functools
import math

import jax
import jax.numpy as jnp
from jax import lax
from jax.experimental import pallas as pl
from jax.experimental.pallas import tpu as pltpu

F32 = jnp.float32
BF16 = jnp.bfloat16

LN_EPS = 1e-5
PEER_TOPK = 16
LANES = 128
SUBLANES = 8
VMEM_LIMIT = 56 * 1024 * 1024

ML_CHUNK = 256
ML_BATCH = 4
S5_CHUNK = 1024
NEG_INF = float("-inf")


def _cparams(*sem):
    return pltpu.CompilerParams(dimension_semantics=sem, vmem_limit_bytes=VMEM_LIMIT)


def _const_spec(shape):
    nd = len(shape)
    return pl.BlockSpec(shape, lambda *_: (0,) * nd, pipeline_mode=pl.Buffered(1))


def _mm_kernel(a_ref, w_ref, o_ref):
    o_ref[...] = jnp.dot(a_ref[...], w_ref[...], preferred_element_type=F32).astype(o_ref.dtype)


def _matmul(a, w, out_dtype, tm=1024, tn=512):
    m, k = a.shape
    n = w.shape[1]
    tm, tn = min(tm, m), min(tn, n)
    assert m % tm == 0 and n % tn == 0
    return pl.pallas_call(
        _mm_kernel,
        grid=(m // tm, n // tn),
        in_specs=[pl.BlockSpec((tm, k), lambda i, j: (i, 0)),
                  pl.BlockSpec((k, tn), lambda i, j: (0, j))],
        out_specs=pl.BlockSpec((tm, tn), lambda i, j: (i, j)),
        out_shape=jax.ShapeDtypeStruct((m, n), out_dtype),
        compiler_params=_cparams("parallel", "parallel"),
        name="gate_proj",
    )(a, w)


def _in_proj_kernel(a_ref, *refs, segs, nw, sub):
    j = pl.program_id(1)
    w_refs, o_refs = refs[:nw], refs[nw:]
    for o_ref, (lo, hi, wi, sig) in zip(o_refs, segs):
        @pl.when((j >= lo) & (j < hi))
        def _(o_ref=o_ref, w_ref=w_refs[wi], sig=sig):
            for q in range(o_ref.shape[1] // sub):
                cs = slice(q * sub, (q + 1) * sub)
                acc = jnp.dot(a_ref[...], w_ref[:, cs], preferred_element_type=F32)
                o_ref[:, cs] = (jax.nn.sigmoid(acc) if sig else acc).astype(o_ref.dtype)


def _in_proj(a, weights, outs, tm=1024, tn=1024, sub=256):
    m, k = a.shape
    tm = min(tm, m)
    assert m % tm == 0 and all(o[1] % tn == 0 for o in outs)
    segs, ospecs, shapes, step = [], [], [], 0
    wlo = [None] * len(weights)
    wnb = [0] * len(weights)
    for wi, width, dtype, sig in outs:
        nblk = width // tn
        if wlo[wi] is None:
            wlo[wi] = step
        assert wlo[wi] + wnb[wi] == step, "outputs sharing a weight must be adjacent"
        wnb[wi] += nblk
        segs.append((step, step + nblk, wi, sig))
        ospecs.append(pl.BlockSpec((tm, tn), lambda i, j, lo=step, nblk=nblk: (i, jnp.clip(j - lo, 0, nblk - 1))))
        shapes.append(jax.ShapeDtypeStruct((m, width), dtype))
        step += nblk
    assert all(w.shape == (k, nb * tn) for w, nb in zip(weights, wnb))
    wspecs = [pl.BlockSpec((k, tn), lambda i, j, lo=lo, nb=nb: (0, jnp.clip(j - lo, 0, nb - 1)))
              for lo, nb in zip(wlo, wnb)]
    return pl.pallas_call(
        functools.partial(_in_proj_kernel, segs=tuple(segs), nw=len(weights), sub=sub),
        grid=(m // tm, step),
        in_specs=[pl.BlockSpec((tm, k), lambda i, j: (i, 0))] + wspecs,
        out_specs=ospecs,
        out_shape=shapes,
        compiler_params=_cparams("parallel", "arbitrary"),
        name="in_proj",
    )(a, *weights)


def _s5_kernel(u_ref, bb_ref, cb_ref, lam_ref, y_ref, bu_ref, st_ref, *, bsz, chunk, pitch, ntile):
    c = pl.program_id(1)

    @pl.when(c == 0)
    def _():
        st_ref[...] = jnp.zeros_like(st_ref)

    bb = bb_ref[0]
    for b in range(bsz):
        bu = jnp.dot(u_ref[b].astype(BF16), bb, preferred_element_type=F32)
        for k in range(2 * ntile):
            bu_ref[k, b * pitch:b * pitch + chunk, :] = bu[:, k * LANES:(k + 1) * LANES]

    lam = lam_ref[0]
    lr = [jnp.broadcast_to(lam[k:k + 1, :], (bsz, LANES)) for k in range(ntile)]
    li = [jnp.broadcast_to(lam[ntile + k:ntile + k + 1, :], (bsz, LANES)) for k in range(ntile)]

    def step(t, carry):
        new = []
        for k in range(ntile):
            sr, si = carry[2 * k], carry[2 * k + 1]
            rows = pl.ds(t, bsz, stride=pitch)
            nr = lr[k] * sr - li[k] * si + bu_ref[k, rows, :]
            ni = lr[k] * si + li[k] * sr + bu_ref[ntile + k, rows, :]
            bu_ref[k, rows, :] = nr
            bu_ref[ntile + k, rows, :] = ni
            new += [nr, ni]
        return tuple(new)

    init = []
    for k in range(ntile):
        init += [st_ref[k], st_ref[ntile + k]]
    fin = lax.fori_loop(0, chunk, step, tuple(init), unroll=2)
    for k in range(ntile):
        st_ref[k] = fin[2 * k]
        st_ref[ntile + k] = fin[2 * k + 1]

    cb = cb_ref[0]
    for b in range(bsz):
        st = jnp.concatenate([bu_ref[k, b * pitch:b * pitch + chunk, :] for k in range(2 * ntile)], axis=1)
        y_ref[b] = jnp.dot(st.astype(BF16), cb, preferred_element_type=F32)


def _s5_scan(u, bb, cb, lam):
    bsz, s, w = u.shape
    nslab = w // LANES
    ntile = lam.shape[1] // 2
    chunk = min(S5_CHUNK, s)
    pitch = chunk + SUBLANES
    assert s % chunk == 0 and bsz <= SUBLANES
    kern = functools.partial(_s5_kernel, bsz=bsz, chunk=chunk, pitch=pitch, ntile=ntile)
    return pl.pallas_call(
        kern,
        grid=(nslab, s // chunk),
        in_specs=[pl.BlockSpec((bsz, chunk, LANES), lambda j, c: (0, c, j)),
                  pl.BlockSpec((1, LANES, 2 * ntile * LANES), lambda j, c: (j, 0, 0)),
                  pl.BlockSpec((1, 2 * ntile * LANES, LANES), lambda j, c: (j, 0, 0)),
                  pl.BlockSpec((1, 2 * ntile, LANES), lambda j, c: (j, 0, 0))],
        out_specs=pl.BlockSpec((bsz, chunk, LANES), lambda j, c: (0, c, j)),
        out_shape=jax.ShapeDtypeStruct((bsz, s, w), F32),
        scratch_shapes=[pltpu.VMEM((2 * ntile, bsz * pitch, LANES), F32),
                        pltpu.VMEM((2 * ntile, bsz, LANES), F32)],
        compiler_params=_cparams("parallel", "arbitrary"),
        name="s5_scan",
    )(u, bb, cb, lam)


def _s5_params(a_re, a_im, log_dt, b_re, b_im, c_re, c_im):
    g, p = a_re.shape
    gc = b_re.shape[2]
    gps = LANES // gc
    nslab = g // gps
    dt = jnp.exp(log_dt)[:, None]
    mag = jnp.exp(dt * a_re)
    lam_r = mag * jnp.cos(dt * a_im)
    lam_i = mag * jnp.sin(dt * a_im)
    den = a_re * a_re + a_im * a_im
    zr = ((lam_r - 1.0) * a_re + lam_i * a_im) / den
    zi = (lam_i * a_re - (lam_r - 1.0) * a_im) / den
    bbar_r = zr[..., None] * b_re - zi[..., None] * b_im
    bbar_i = zr[..., None] * b_im + zi[..., None] * b_re
    eye = jnp.eye(gps, dtype=F32)

    def in_slab(bm):
        bm = bm.reshape(nslab, gps, p, gc)
        return jnp.einsum("jgpc,gh->jgchp", bm, eye).reshape(nslab, gps * gc, gps * p)

    def out_slab(cm):
        cm = cm.reshape(nslab, gps, gc, p)
        return jnp.einsum("jgcp,gh->jhpgc", cm, eye).reshape(nslab, gps * p, gps * gc)

    bb = jnp.concatenate([in_slab(bbar_r), in_slab(bbar_i)], axis=2).astype(BF16)
    cb = jnp.concatenate([out_slab(c_re), -out_slab(c_im)], axis=1).astype(BF16)
    ntile = gps * p // LANES
    lam = jnp.concatenate([lam_r.reshape(nslab, ntile, LANES), lam_i.reshape(nslab, ntile, LANES)], axis=1)
    return bb, cb, lam


def _log_sigmoid(x):
    return jnp.minimum(x, 0.0) - jnp.log1p(jnp.exp(-jnp.abs(x)))


def _mlstm_kernel(qk_ref, v_ref, o_ref, gc_ref, gr_ref, bc_ref, br_ref, cw_ref, nw_ref, y_ref,
                  buf_ref, qs_ref, c_ref, n_ref, m_ref, *, nb, nh, dh, chunk, kconv):
    ci = pl.program_id(1)
    width = nh * dh
    halo = SUBLANES

    @pl.when(ci == 0)
    def _():
        buf_ref[:, 0:halo, :] = jnp.zeros((nb, halo, 2 * width), F32)
        c_ref[...] = jnp.zeros_like(c_ref)
        n_ref[...] = jnp.zeros_like(n_ref)
        m_ref[...] = jnp.zeros_like(m_ref)

    tt = lax.broadcasted_iota(jnp.int32, (chunk, chunk), 0)
    ss = lax.broadcasted_iota(jnp.int32, (chunk, chunk), 1)
    causal = ss <= tt
    tril = causal.astype(F32)
    triu = (tt <= ss).astype(F32)

    gates = []
    for bi in range(nb):
        buf_ref[bi, halo:halo + chunk, :] = qk_ref[bi]
        base = halo - (kconv - 1)
        conv = cw_ref[0:1, :] * buf_ref[bi, base:base + chunk, :]
        for j in range(1, kconv):
            conv = conv + cw_ref[j:j + 1, :] * buf_ref[bi, base + j:base + j + chunk, :]
        buf_ref[bi, 0:halo, :] = buf_ref[bi, chunk:chunk + halo, :]
        qs_ref[bi] = conv * jax.nn.sigmoid(conv)

        gcol = gc_ref[bi] + bc_ref[...]
        col_id = lax.broadcasted_iota(jnp.int32, gcol.shape, 1)
        lcol = jnp.where(col_id >= nh, _log_sigmoid(gcol), gcol)
        grow = gr_ref[bi] + br_ref[...]
        row_id = lax.broadcasted_iota(jnp.int32, grow.shape, 0)
        lrow = jnp.where(row_id >= nh, _log_sigmoid(grow), grow)
        bcol = jnp.dot(tril, lcol, preferred_element_type=F32, precision=lax.Precision.HIGHEST)
        brow = jnp.dot(lrow, triu, preferred_element_type=F32, precision=lax.Precision.HIGHEST)
        gates.append((lcol, lrow, bcol, brow))

    inv_sqrt = 1.0 / math.sqrt(dh)
    for h in range(nh):
        hs = slice(h * dh, (h + 1) * dh)
        ks = slice(width + h * dh, width + (h + 1) * dh)
        bs = range(nb)
        st = [bi * nh + h for bi in bs]
        qf = [qs_ref[bi, :, hs] for bi in bs]
        q = [x.astype(BF16) for x in qf]
        kf = [qs_ref[bi, :, ks] * inv_sqrt for bi in bs]
        k = [x.astype(BF16) for x in kf]
        v = [v_ref[bi, :, hs] for bi in bs]
        b_c = [gates[bi][2][:, nh + h:nh + h + 1] for bi in bs]
        i_c = [gates[bi][0][:, h:h + 1] for bi in bs]
        b_r = [gates[bi][3][nh + h:nh + h + 1, :] for bi in bs]
        i_r = [gates[bi][1][h:h + 1, :] for bi in bs]
        g_tot = [x[chunk - 1:chunk, :] for x in b_c]
        m_prev = [m_ref[s_][:, 0:1] for s_ in st]
        c_prev = [c_ref[s_] for s_ in st]
        n_prev = [n_ref[s_] for s_ in st]

        qk_t = [lax.dot_general(q[bi], k[bi], (((1,), (1,)), ((), ())), preferred_element_type=F32) for bi in bs]
        q_c = [jnp.dot(q[bi], c_prev[bi].astype(BF16), preferred_element_type=F32) for bi in bs]
        dmat = [jnp.where(causal, b_c[bi] - b_r[bi] + i_r[bi], NEG_INF) for bi in bs]
        inter_log = [b_c[bi] + m_prev[bi] for bi in bs]
        m_row = [jnp.maximum(inter_log[bi], jnp.max(dmat[bi], axis=1, keepdims=True)) for bi in bs]
        sc = [qk_t[bi] * jnp.exp(dmat[bi] - m_row[bi]) for bi in bs]
        inter_scale = [jnp.exp(inter_log[bi] - m_row[bi]) for bi in bs]
        sc_v = [jnp.dot(sc[bi].astype(BF16), v[bi], preferred_element_type=F32) for bi in bs]

        wlog = [g_tot[bi] - b_c[bi] + i_c[bi] for bi in bs]
        m_loc = [jnp.max(x, axis=0, keepdims=True) for x in wlog]
        kw = [kf[bi] * jnp.exp(wlog[bi] - m_loc[bi]) for bi in bs]
        c_chunk = [lax.dot_general(kw[bi].astype(BF16), v[bi], (((0,), (0,)), ((), ())),
                                   preferred_element_type=F32) for bi in bs]

        num = [sc_v[bi] + inter_scale[bi] * q_c[bi] for bi in bs]
        qn = [jnp.sum(qf[bi] * n_prev[bi], axis=1, keepdims=True) for bi in bs]
        den = [jnp.sum(sc[bi], axis=1, keepdims=True) + inter_scale[bi] * qn[bi] for bi in bs]
        hh = [num[bi] / jnp.maximum(jnp.abs(den[bi]), jnp.exp(-m_row[bi])) for bi in bs]
        mu = [jnp.mean(x, axis=1, keepdims=True) for x in hh]
        var = [jnp.mean(jnp.square(hh[bi] - mu[bi]), axis=1, keepdims=True) for bi in bs]
        for bi in bs:
            hn = (hh[bi] - mu[bi]) * lax.rsqrt(var[bi] + LN_EPS) * nw_ref[:, hs]
            y_ref[bi, :, hs] = (o_ref[bi, :, hs].astype(F32) * hn).astype(y_ref.dtype)

        for bi in bs:
            n_chunk = jnp.sum(kw[bi], axis=0, keepdims=True)
            m_new = jnp.maximum(g_tot[bi] + m_prev[bi], m_loc[bi])
            a = jnp.exp(g_tot[bi] + m_prev[bi] - m_new)
            bb = jnp.exp(m_loc[bi] - m_new)
            c_ref[st[bi]] = a * c_prev[bi] + bb * c_chunk[bi]
            n_ref[st[bi]] = a * n_prev[bi] + bb * n_chunk
            m_ref[st[bi]] = jnp.broadcast_to(m_new, (1, LANES))


def _mlstm(qk, v, osig, gcol, grow, bias_c, bias_r, conv_w, norm_w, nh, nb=ML_BATCH):
    bsz, s, w2 = qk.shape
    width = w2 // 2
    dh = width // nh
    chunk = ML_CHUNK
    kconv = conv_w.shape[0]
    nb = math.gcd(nb, bsz)
    assert s % chunk == 0 and kconv - 1 <= SUBLANES
    kern = functools.partial(_mlstm_kernel, nb=nb, nh=nh, dh=dh, chunk=chunk, kconv=kconv)
    return pl.pallas_call(
        kern,
        grid=(bsz // nb, s // chunk),
        in_specs=[pl.BlockSpec((nb, chunk, w2), lambda b, c: (b, c, 0)),
                  pl.BlockSpec((nb, chunk, width), lambda b, c: (b, c, 0)),
                  pl.BlockSpec((nb, chunk, width), lambda b, c: (b, c, 0)),
                  pl.BlockSpec((nb, chunk, LANES), lambda b, c: (b, c, 0)),
                  pl.BlockSpec((nb, 2 * nh, chunk), lambda b, c: (b, 0, c)),
                  _const_spec((1, LANES)),
                  _const_spec((2 * nh, 1)),
                  _const_spec((kconv, w2)),
                  _const_spec((1, width))],
        out_specs=pl.BlockSpec((nb, chunk, width), lambda b, c: (b, c, 0)),
        out_shape=jax.ShapeDtypeStruct((bsz, s, width), BF16),
        scratch_shapes=[pltpu.VMEM((nb, SUBLANES + chunk, w2), F32),
                        pltpu.VMEM((nb, chunk, w2), F32),
                        pltpu.VMEM((nb * nh, dh, dh), F32),
                        pltpu.VMEM((nb * nh, 1, dh), F32),
                        pltpu.VMEM((nb * nh, 1, LANES), F32)],
        compiler_params=_cparams("parallel", "arbitrary"),
        name="mlstm",
    )(qk, v, osig, gcol, grow, bias_c, bias_r, conv_w, norm_w)


def _layer_norm(x, g, b):
    mu = jnp.mean(x, axis=-1, keepdims=True)
    var = jnp.mean(jnp.square(x - mu), axis=-1, keepdims=True)
    return (x - mu) * lax.rsqrt(var + LN_EPS) * g + b


def _mix_kernel(yp_ref, u_ref, yb_ref, ga_ref, gb_ref, x_ref, d_ref, wg_ref, bg_ref, wa_ref, wb_ref, wo_ref,
                lg_ref, lb_ref, h_ref, ht_ref, *, alpha):
    y = jax.nn.gelu(yp_ref[...] + d_ref[...] * u_ref[...])
    gate = jax.nn.sigmoid(jnp.dot(y.astype(BF16), wg_ref[...], preferred_element_type=F32) + bg_ref[...])
    ya = (y * gate).astype(BF16)
    merged = (ga_ref[...].astype(F32) * jnp.dot(ya, wa_ref[...], preferred_element_type=F32)
              + gb_ref[...].astype(F32) * jnp.dot(yb_ref[...], wb_ref[...], preferred_element_type=F32))
    mix = jnp.dot(merged.astype(BF16), wo_ref[...], preferred_element_type=F32)
    h = _layer_norm(alpha * x_ref[...] + mix, lg_ref[...], lb_ref[...])
    h_ref[...] = h
    ht_ref[...] = h.T.astype(BF16)


def _mix(y_pre, u, y_b, gates, x, d, w_glu, b_glu, w_a, w_b, w_o, ln_g, ln_b, alpha, tm=256):
    t, dm = x.shape
    w = u.shape[1]
    assert t % tm == 0
    row = lambda width: pl.BlockSpec((tm, width), lambda i: (i, 0))
    return pl.pallas_call(
        functools.partial(_mix_kernel, alpha=alpha),
        grid=(t // tm,),
        in_specs=[row(w), row(w), row(w),
                  pl.BlockSpec((tm, dm), lambda i: (i, 0)),
                  pl.BlockSpec((tm, dm), lambda i: (i, 1)),
                  row(dm),
                  _const_spec((1, w)), _const_spec((w, w)), _const_spec((1, w)),
                  _const_spec((w, dm)), _const_spec((w, dm)), _const_spec((dm, dm)),
                  _const_spec((1, dm)), _const_spec((1, dm))],
        out_specs=[pl.BlockSpec((tm, dm), lambda i: (i, 0)),
                   pl.BlockSpec((dm, tm), lambda i: (0, i))],
        out_shape=[jax.ShapeDtypeStruct((t, dm), F32), jax.ShapeDtypeStruct((dm, t), BF16)],
        compiler_params=_cparams("parallel"),
        name="mix_ln1",
    )(y_pre, u, y_b, gates, gates, x, d, w_glu, b_glu, w_a, w_b, w_o, ln_g, ln_b)


def _oddeven_merge(lo, hi, r):
    step = r * 2
    if step < hi - lo:
        yield from _oddeven_merge(lo, hi, step)
        yield from _oddeven_merge(lo + r, hi, step)
        yield from [(i, i + r) for i in range(lo + r, hi - r, step)]
    else:
        yield (lo, lo + r)


def _oddeven_merge_sort(lo, hi):
    if hi - lo >= 1:
        mid = lo + (hi - lo) // 2
        yield from _oddeven_merge_sort(lo, mid)
        yield from _oddeven_merge_sort(mid + 1, hi)
        yield from _oddeven_merge(lo, hi, 1)


def _sorted_topk_rows(arrays, dst_refs, k):
    g = arrays[0].shape[0] // SUBLANES
    ws = [[s[j * SUBLANES:(j + 1) * SUBLANES, :] for j in range(g)] for s in arrays]
    for i, j in _oddeven_merge_sort(0, g - 1):
        for w in ws:
            w[i], w[j] = jnp.maximum(w[i], w[j]), jnp.minimum(w[i], w[j])
    for r in range(k):
        left = k - 1 - r
        for w, dst_ref in zip(ws, dst_refs):
            mx = jnp.max(w[0], axis=0, keepdims=True)
            dst_ref[r:r + 1, :] = mx
            if left > 0:
                hit = w[0] == mx
                for lvl in range(min(g, left)):
                    below = w[lvl + 1] if lvl + 1 < g else NEG_INF
                    w[lvl] = jnp.where(hit, below, w[lvl])


def _route_kernel(h_ref, wq_ref, keys_ref, cnt_ref, e1_ref, rank_ref, e2_ref, a_ref, b_ref, cand_ref,
                  *, nh, nk, half, topk, ncand_rows):
    q = jnp.dot(h_ref[...].astype(BF16), wq_ref[...], preferred_element_type=F32).astype(BF16)
    tb = q.shape[0]
    nrank = topk + 1
    for h in range(nh):
        s1 = lax.dot_general(keys_ref[2 * h], q[:, (2 * h) * half:(2 * h + 1) * half],
                             (((1,), (1,)), ((), ())), preferred_element_type=F32)
        s2 = lax.dot_general(keys_ref[2 * h + 1], q[:, (2 * h + 1) * half:(2 * h + 2) * half],
                             (((1,), (1,)), ((), ())), preferred_element_type=F32)
        _sorted_topk_rows([s1, s2], [a_ref, b_ref], nrank)
        a = a_ref[0:nrank, :]
        b = b_ref[0:nrank, :]
        cand_ref[...] = jnp.full((ncand_rows, tb), NEG_INF, F32)
        off = 0
        for i in range(nrank):
            n_i = nrank // (i + 1)
            cand_ref[off:off + n_i, :] = a[i:i + 1, :] + b[0:n_i, :]
            off += n_i
        cur = cand_ref[...]
        top = a[0:1, :] + b[0:1, :]
        z = jnp.zeros((1, tb), F32)
        kth = top
        for r in range(topk):
            kth = jnp.max(cur, axis=0, keepdims=True)
            z = z + jnp.exp(kth - top)
            cur = jnp.where(cur == kth, NEG_INF, cur)
        nxt = jnp.max(cur, axis=0, keepdims=True)
        tau = 0.5 * (kth + nxt)
        thr = tau - s1
        cnt = jnp.zeros_like(s1)
        for r in range(topk):
            cnt = jnp.where(b[r:r + 1, :] >= thr, float(r + 1), cnt)
        rank = jnp.full_like(s2, float(nrank))
        for r in reversed(range(nrank)):
            rank = jnp.where(s2 >= b[r:r + 1, :], float(r), rank)
        cnt_ref[h] = cnt
        e1_ref[h] = jnp.exp(s1 - a[0:1, :])
        rank_ref[h] = rank.astype(rank_ref.dtype)
        e2_ref[h] = (jnp.exp(s2 - b[0:1, :]) / z).astype(e2_ref.dtype)


def _route(h, wq, keys2, nh, tb=512):
    t, dm = h.shape
    nk, half = keys2.shape[1], keys2.shape[2]
    topk = PEER_TOPK
    nrank = topk + 1
    ncand = sum(nrank // (r + 1) for r in range(nrank))
    ncand_rows = -(-ncand // SUBLANES) * SUBLANES
    rank_rows = -(-nrank // SUBLANES) * SUBLANES
    tb = min(tb, t)
    assert t % tb == 0 and nk > nrank
    kern = functools.partial(_route_kernel, nh=nh, nk=nk, half=half, topk=topk, ncand_rows=ncand_rows)
    ospec = pl.BlockSpec((nh, nk, tb), lambda i: (0, 0, i))
    return pl.pallas_call(
        kern,
        grid=(t // tb,),
        in_specs=[pl.BlockSpec((tb, dm), lambda i: (i, 0)),
                  _const_spec(wq.shape), _const_spec(keys2.shape)],
        out_specs=[ospec] * 4,
        out_shape=[jax.ShapeDtypeStruct((nh, nk, t), dt) for dt in (F32, F32, BF16, BF16)],
        scratch_shapes=[pltpu.VMEM((rank_rows, tb), F32), pltpu.VMEM((rank_rows, tb), F32),
                        pltpu.VMEM((ncand_rows, tb), F32)],
        compiler_params=_cparams("parallel"),
        name="peer_route",
    )(h, wq, keys2)


_GELU_K1 = -2.0 * math.sqrt(2.0 / math.pi) * math.log2(math.e)
_GELU_K2 = _GELU_K1 * 0.044715


def _gelu_tanh(x):
    return x / (1.0 + jnp.exp2(x * (_GELU_K1 + _GELU_K2 * (x * x))))


def _expert_kernel(xt_ref, u_ref, vt_ref, cnt_ref, e1_ref, rank_ref, e2_ref, o_ref, sa_ref, sb_ref,
                   *, nh, nk, rows, sub, last):
    e = pl.program_id(1)
    slots = (sa_ref, sb_ref)

    def step(rd_ref, wr_ref):
        for q in range(xt_ref.shape[1] // sub):
            cs = slice(q * sub, (q + 1) * sub)
            if wr_ref is not None:
                wr_ref[:, cs] = jnp.dot(u_ref[...], xt_ref[:, cs], preferred_element_type=F32)
            if rd_ref is None:
                continue
            parts = []
            for r in range(rows):
                g = None
                for h in range(nh):
                    sel = jnp.where(rank_ref[h, :, cs] < cnt_ref[h, r:r + 1, cs].astype(BF16),
                                    e2_ref[h, :, cs] * e1_ref[h, r:r + 1, cs].astype(BF16), 0.0)
                    g = sel if g is None else g + sel
                parts.append(_gelu_tanh(rd_ref[r * nk:(r + 1) * nk, cs]).astype(BF16) * g)
            act = jnp.concatenate(parts, axis=0)
            o_ref[:, cs] += jnp.dot(vt_ref[...], act, preferred_element_type=F32)

    @pl.when(e == 0)
    def _():
        o_ref[...] = jnp.zeros_like(o_ref)
        step(None, slots[1])

    for par in range(2):
        @pl.when((e > 0) & (e < last) & (lax.rem(e, 2) == par))
        def _(par=par):
            step(slots[par], slots[1 - par])

    @pl.when(e == last)
    def _():
        step(slots[last % 2], None)


def _experts(xt, u_tab, vt_tab, cnt, e1, rank, e2, tb=1024, rows=8, sub=256):
    dm, t = xt.shape
    ne = u_tab.shape[0]
    nh, nk, _ = cnt.shape
    neb = rows * nk
    tb = min(tb, t)
    assert t % tb == 0 and tb % sub == 0 and ne % neb == 0 and ne == nk * nk
    nblk = ne // neb
    kern = functools.partial(_expert_kernel, nh=nh, nk=nk, rows=rows, sub=sub, last=nblk)
    once = pl.Buffered(1)
    score_blk = lambda e: jnp.minimum(e, nblk - 1)
    gate_blk = lambda e: jnp.maximum(e - 1, 0)
    return pl.pallas_call(
        kern,
        grid=(t // tb, nblk + 1),
        in_specs=[pl.BlockSpec((dm, tb), lambda i, e: (0, i), pipeline_mode=once),
                  pl.BlockSpec((neb, dm), lambda i, e: (score_blk(e), 0)),
                  pl.BlockSpec((dm, neb), lambda i, e: (0, gate_blk(e))),
                  pl.BlockSpec((nh, rows, tb), lambda i, e: (0, gate_blk(e), i)),
                  pl.BlockSpec((nh, rows, tb), lambda i, e: (0, gate_blk(e), i)),
                  pl.BlockSpec((nh, nk, tb), lambda i, e: (0, 0, i), pipeline_mode=once),
                  pl.BlockSpec((nh, nk, tb), lambda i, e: (0, 0, i), pipeline_mode=once)],
        out_specs=pl.BlockSpec((dm, tb), lambda i, e: (0, i), pipeline_mode=once),
        out_shape=jax.ShapeDtypeStruct((dm, t), F32),
        scratch_shapes=[pltpu.VMEM((neb, tb), F32), pltpu.VMEM((neb, tb), F32)],
        compiler_params=_cparams("parallel", "arbitrary"),
        name="peer_experts",
    )(xt, u_tab, vt_tab, cnt, e1, rank, e2)


def _final_kernel(h_ref, yt_ref, p_ref, lg_ref, lb_ref, wg_ref, wp_ref, o_ref, *, alpha):
    h = _layer_norm(alpha * h_ref[...] + yt_ref[...].T, lg_ref[...], lb_ref[...])
    gate = jax.nn.sigmoid(jnp.dot(h.astype(BF16), wg_ref[...], preferred_element_type=F32))
    proj = jnp.dot(p_ref[...].astype(BF16), wp_ref[...], preferred_element_type=F32)
    o_ref[...] = h + gate * proj


def _final(h, yt, p, ln_g, ln_b, w_gate, w_proj, alpha, tm=512):
    t, dm = h.shape
    pd = p.shape[1]
    assert t % tm == 0
    return pl.pallas_call(
        functools.partial(_final_kernel, alpha=alpha),
        grid=(t // tm,),
        in_specs=[pl.BlockSpec((tm, dm), lambda i: (i, 0)),
                  pl.BlockSpec((dm, tm), lambda i: (0, i)),
                  pl.BlockSpec((tm, pd), lambda i: (i, 0)),
                  _const_spec((1, dm)), _const_spec((1, dm)),
                  _const_spec((dm, dm)), _const_spec((pd, dm))],
        out_specs=pl.BlockSpec((tm, dm), lambda i: (i, 0)),
        out_shape=jax.ShapeDtypeStruct((t, dm), F32),
        compiler_params=_cparams("parallel"),
        name="ln2_ple",
    )(h, yt, p, ln_g, ln_b, w_gate, w_proj)


def _layer(h, p, w_in, b_igate, b_fgate, conv_qk, mh_norm_w, a_re, a_im, log_dt, b_re, b_im, c_re, c_im, d_skip,
           w_glu, b_glu, w_up_ssm, w_up_ml, w_out, ln1_g, ln1_b, peer_wq, peer_keys, peer_u, peer_v,
           ln2_g, ln2_b, ple_w_gate, ple_w_proj, alpha):
    bsz, s, dm = h.shape
    t = bsz * s
    nh = b_igate.shape[0]
    ssm_w = d_skip.shape[0]
    ml_w = mh_norm_w.shape[0]
    x2 = h.reshape(t, dm)
    xb = x2.astype(BF16)

    o0 = ssm_w
    o1 = o0 + 2 * ml_w
    o2 = o1 + ml_w
    o3 = o2 + ml_w
    o4 = o3 + 2 * nh
    u, qk, v, osig, gates = _in_proj(
        xb, [w_in[:, :o3].astype(BF16), w_in[:, o4:].astype(BF16)],
        [(0, ssm_w, F32, False), (0, 2 * ml_w, F32, False), (0, ml_w, BF16, False), (0, ml_w, BF16, True),
         (1, w_in.shape[1] - o4, BF16, True)])
    w_if = jnp.zeros((dm, LANES), F32).at[:, :2 * nh].set(w_in[:, o3:o4]).astype(BF16)
    gif = _matmul(xb, w_if, F32)

    bb, cb, lam = _s5_params(a_re, a_im, log_dt, b_re, b_im, c_re, c_im)
    y_pre = _s5_scan(u.reshape(bsz, s, ssm_w), bb, cb, lam)

    gcol = gif.reshape(bsz, s, LANES)
    grow = jnp.swapaxes(gcol[:, :, :2 * nh], 1, 2)
    bias = jnp.concatenate([b_igate, b_fgate])
    bias_c = jnp.zeros((1, LANES), F32).at[0, :2 * nh].set(bias)
    y_b = _mlstm(qk.reshape(bsz, s, 2 * ml_w), v.reshape(bsz, s, ml_w), osig.reshape(bsz, s, ml_w),
                 gcol, grow, bias_c, bias[:, None], conv_qk, mh_norm_w[None, :], nh)

    h1, h1t = _mix(y_pre.reshape(t, ssm_w), u, y_b.reshape(t, ml_w), gates, x2, d_skip[None, :],
                   w_glu.astype(BF16), b_glu[None, :], w_up_ssm.astype(BF16), w_up_ml.astype(BF16),
                   w_out.astype(BF16), ln1_g[None, :], ln1_b[None, :], alpha)

    ph, _, nk, half = peer_keys.shape
    keys2 = peer_keys.reshape(ph * 2, nk, half).astype(BF16)
    cnt, e1, rank, e2 = _route(h1, peer_wq.astype(BF16), keys2, ph)
    y2t = _experts(h1t, peer_u.astype(BF16), peer_v.T.astype(BF16), cnt, e1, rank, e2)

    out = _final(h1, y2t, p.reshape(t, -1), ln2_g[None, :], ln2_b[None, :],
                 ple_w_gate.astype(BF16), ple_w_proj.astype(BF16), alpha)
    return out.reshape(bsz, s, dm)


def kernel(x, p, w_in, b_igate, b_fgate, conv_qk, mh_norm_w, ssm_a_re, ssm_a_im, ssm_log_dt, ssm_b_re, ssm_b_im,
           ssm_c_re, ssm_c_im, ssm_d, w_glu, b_glu, w_up_ssm, w_up_ml, w_out, ln1_g, ln1_b, peer_wq, peer_keys,
           peer_u, peer_v, ln2_g, ln2_b, ple_w_gate, ple_w_proj):
    depth = w_in.shape[0]
    alpha = (2 * depth) ** 0.25
    h = x
    for i in range(depth):
        h = _layer(h, p[i], w_in[i], b_igate[i], b_fgate[i], conv_qk[i], mh_norm_w[i], ssm_a_re[i], ssm_a_im[i],
                   ssm_log_dt[i], ssm_b_re[i], ssm_b_im[i], ssm_c_re[i], ssm_c_im[i], ssm_d[i], w_glu[i], b_glu[i],
                   w_up_ssm[i], w_up_ml[i], w_out[i], ln1_g[i], ln1_b[i], peer_wq[i], peer_keys[i], peer_u[i],
                   peer_v[i], ln2_g[i], ln2_b[i], ple_w_gate[i], ple_w_proj[i], alpha)
    return h
```

```python
import functools
import math

import jax
import jax.numpy as jnp
from jax import lax
from jax.experimental import pallas as pl
from jax.experimental.pallas import tpu as pltpu

F32 = jnp.float32
BF16 = jnp.bfloat16

LN_EPS = 1e-5
PEER_TOPK = 16
LANES = 128
SUBLANES = 8
VMEM_LIMIT = 56 * 1024 * 1024

ML_CHUNK = 256
ML_BATCH = 4
S5_CHUNK = 1024
NEG_INF = float("-inf")


def _cparams(*sem):
    return pltpu.CompilerParams(dimension_semantics=sem, vmem_limit_bytes=VMEM_LIMIT)


def _const_spec(shape):
    nd = len(shape)
    return pl.BlockSpec(shape, lambda *_: (0,) * nd, pipeline_mode=pl.Buffered(1))


def _mm_kernel(a_ref, w_ref, o_ref):
    o_ref[...] = jnp.dot(a_ref[...], w_ref[...], preferred_element_type=F32).astype(o_ref.dtype)


def _matmul(a, w, out_dtype, tm=1024, tn=512):
    m, k = a.shape
    n = w.shape[1]
    tm, tn = min(tm, m), min(tn, n)
    assert m % tm == 0 and n % tn == 0
    return pl.pallas_call(
        _mm_kernel,
        grid=(m // tm, n // tn),
        in_specs=[pl.BlockSpec((tm, k), lambda i, j: (i, 0)),
                  pl.BlockSpec((k, tn), lambda i, j: (0, j))],
        out_specs=pl.BlockSpec((tm, tn), lambda i, j: (i, j)),
        out_shape=jax.ShapeDtypeStruct((m, n), out_dtype),
        compiler_params=_cparams("parallel", "parallel"),
        name="gate_proj",
    )(a, w)


def _in_proj_kernel(a_ref, *refs, segs, nw, sub):
    j = pl.program_id(1)
    w_refs, o_refs = refs[:nw], refs[nw:]
    for o_ref, (lo, hi, wi, sig) in zip(o_refs, segs):
        @pl.when((j >= lo) & (j < hi))
        def _(o_ref=o_ref, w_ref=w_refs[wi], sig=sig):
            for q in range(o_ref.shape[1] // sub):
                cs = slice(q * sub, (q + 1) * sub)
                acc = jnp.dot(a_ref[...], w_ref[:, cs], preferred_element_type=F32)
                o_ref[:, cs] = (jax.nn.sigmoid(acc) if sig else acc).astype(o_ref.dtype)


def _in_proj(a, weights, outs, tm=1024, tn=1024, sub=256):
    m, k = a.shape
    tm = min(tm, m)
    assert m % tm == 0 and all(o[1] % tn == 0 for o in outs)
    segs, ospecs, shapes, step = [], [], [], 0
    wlo = [None] * len(weights)
    wnb = [0] * len(weights)
    for wi, width, dtype, sig in outs:
        nblk = width // tn
        if wlo[wi] is None:
            wlo[wi] = step
        assert wlo[wi] + wnb[wi] == step, "outputs sharing a weight must be adjacent"
        wnb[wi] += nblk
        segs.append((step, step + nblk, wi, sig))
        ospecs.append(pl.BlockSpec((tm, tn), lambda i, j, lo=step, nblk=nblk: (i, jnp.clip(j - lo, 0, nblk - 1))))
        shapes.append(jax.ShapeDtypeStruct((m, width), dtype))
        step += nblk
    assert all(w.shape == (k, nb * tn) for w, nb in zip(weights, wnb))
    wspecs = [pl.BlockSpec((k, tn), lambda i, j, lo=lo, nb=nb: (0, jnp.clip(j - lo, 0, nb - 1)))
              for lo, nb in zip(wlo, wnb)]
    return pl.pallas_call(
        functools.partial(_in_proj_kernel, segs=tuple(segs), nw=len(weights), sub=sub),
        grid=(m // tm, step),
        in_specs=[pl.BlockSpec((tm, k), lambda i, j: (i, 0))] + wspecs,
        out_specs=ospecs,
        out_shape=shapes,
        compiler_params=_cparams("parallel", "arbitrary"),
        name="in_proj",
    )(a, *weights)


def _s5_kernel(u_ref, bb_ref, cb_ref, lam_ref, y_ref, bu_ref, st_ref, *, bsz, chunk, pitch, ntile):
    c = pl.program_id(1)

    @pl.when(c == 0)
    def _():
        st_ref[...] = jnp.zeros_like(st_ref)

    bb = bb_ref[0]
    for b in range(bsz):
        bu = jnp.dot(u_ref[b].astype(BF16), bb, preferred_element_type=F32)
        for k in range(2 * ntile):
            bu_ref[k, b * pitch:b * pitch + chunk, :] = bu[:, k * LANES:(k + 1) * LANES]

    lam = lam_ref[0]
    lr = [jnp.broadcast_to(lam[k:k + 1, :], (bsz, LANES)) for k in range(ntile)]
    li = [jnp.broadcast_to(lam[ntile + k:ntile + k + 1, :], (bsz, LANES)) for k in range(ntile)]

    def step(t, carry):
        new = []
        for k in range(ntile):
            sr, si = carry[2 * k], carry[2 * k + 1]
            rows = pl.ds(t, bsz, stride=pitch)
            nr = lr[k] * sr - li[k] * si + bu_ref[k, rows, :]
            ni = lr[k] * si + li[k] * sr + bu_ref[ntile + k, rows, :]
            bu_ref[k, rows, :] = nr
            bu_ref[ntile + k, rows, :] = ni
            new += [nr, ni]
        return tuple(new)

    init = []
    for k in range(ntile):
        init += [st_ref[k], st_ref[ntile + k]]
    fin = lax.fori_loop(0, chunk, step, tuple(init), unroll=2)
    for k in range(ntile):
        st_ref[k] = fin[2 * k]
        st_ref[ntile + k] = fin[2 * k + 1]

    cb = cb_ref[0]
    for b in range(bsz):
        st = jnp.concatenate([bu_ref[k, b * pitch:b * pitch + chunk, :] for k in range(2 * ntile)], axis=1)
        y_ref[b] = jnp.dot(st.astype(BF16), cb, preferred_element_type=F32)


def _s5_scan(u, bb, cb, lam):
    bsz, s, w = u.shape
    nslab = w // LANES
    ntile = lam.shape[1] // 2
    chunk = min(S5_CHUNK, s)
    pitch = chunk + SUBLANES
    assert s % chunk == 0 and bsz <= SUBLANES
    kern = functools.partial(_s5_kernel, bsz=bsz, chunk=chunk, pitch=pitch, ntile=ntile)
    return pl.pallas_call(
        kern,
        grid=(nslab, s // chunk),
        in_specs=[pl.BlockSpec((bsz, chunk, LANES), lambda j, c: (0, c, j)),
                  pl.BlockSpec((1, LANES, 2 * ntile * LANES), lambda j, c: (j, 0, 0)),
                  pl.BlockSpec((1, 2 * ntile * LANES, LANES), lambda j, c: (j, 0, 0)),
                  pl.BlockSpec((1, 2 * ntile, LANES), lambda j, c: (j, 0, 0))],
        out_specs=pl.BlockSpec((bsz, chunk, LANES), lambda j, c: (0, c, j)),
        out_shape=jax.ShapeDtypeStruct((bsz, s, w), F32),
        scratch_shapes=[pltpu.VMEM((2 * ntile, bsz * pitch, LANES), F32),
                        pltpu.VMEM((2 * ntile, bsz, LANES), F32)],
        compiler_params=_cparams("parallel", "arbitrary"),
        name="s5_scan",
    )(u, bb, cb, lam)


def _s5_params(a_re, a_im, log_dt, b_re, b_im, c_re, c_im):
    g, p = a_re.shape
    gc = b_re.shape[2]
    gps = LANES // gc
    nslab = g // gps
    dt = jnp.exp(log_dt)[:, None]
    mag = jnp.exp(dt * a_re)
    lam_r = mag * jnp.cos(dt * a_im)
    lam_i = mag * jnp.sin(dt * a_im)
    den = a_re * a_re + a_im * a_im
    zr = ((lam_r - 1.0) * a_re + lam_i * a_im) / den
    zi = (lam_i * a_re - (lam_r - 1.0) * a_im) / den
    bbar_r = zr[..., None] * b_re - zi[..., None] * b_im
    bbar_i = zr[..., None] * b_im + zi[..., None] * b_re
    eye = jnp.eye(gps, dtype=F32)

    def in_slab(bm):
        bm = bm.reshape(nslab, gps, p, gc)
        return jnp.einsum("jgpc,gh->jgchp", bm, eye).reshape(nslab, gps * gc, gps * p)

    def out_slab(cm):
        cm = cm.reshape(nslab, gps, gc, p)
        return jnp.einsum("jgcp,gh->jhpgc", cm, eye).reshape(nslab, gps * p, gps * gc)

    bb = jnp.concatenate([in_slab(bbar_r), in_slab(bbar_i)], axis=2).astype(BF16)
    cb = jnp.concatenate([out_slab(c_re), -out_slab(c_im)], axis=1).astype(BF16)
    ntile = gps * p // LANES
    lam = jnp.concatenate([lam_r.reshape(nslab, ntile, LANES), lam_i.reshape(nslab, ntile, LANES)], axis=1)
    return bb, cb, lam


def _log_sigmoid(x):
    return jnp.minimum(x, 0.0) - jnp.log1p(jnp.exp(-jnp.abs(x)))


def _mlstm_kernel(qk_ref, v_ref, o_ref, gc_ref, gr_ref, bc_ref, br_ref, cw_ref, nw_ref, y_ref,
                  buf_ref, qs_ref, c_ref, n_ref, m_ref, *, nb, nh, dh, chunk, kconv):
    ci = pl.program_id(1)
    width = nh * dh
    halo = SUBLANES

    @pl.when(ci == 0)
    def _():
        buf_ref[:, 0:halo, :] = jnp.zeros((nb, halo, 2 * width), F32)
        c_ref[...] = jnp.zeros_like(c_ref)
        n_ref[...] = jnp.zeros_like(n_ref)
        m_ref[...] = jnp.zeros_like(m_ref)

    tt = lax.broadcasted_iota(jnp.int32, (chunk, chunk), 0)
    ss = lax.broadcasted_iota(jnp.int32, (chunk, chunk), 1)
    causal = ss <= tt
    tril = causal.astype(F32)
    triu = (tt <= ss).astype(F32)

    gates = []
    for bi in range(nb):
        buf_ref[bi, halo:halo + chunk, :] = qk_ref[bi]
        base = halo - (kconv - 1)
        conv = cw_ref[0:1, :] * buf_ref[bi, base:base + chunk, :]
        for j in range(1, kconv):
            conv = conv + cw_ref[j:j + 1, :] * buf_ref[bi, base + j:base + j + chunk, :]
        buf_ref[bi, 0:halo, :] = buf_ref[bi, chunk:chunk + halo, :]
        qs_ref[bi] = conv * jax.nn.sigmoid(conv)

        gcol = gc_ref[bi] + bc_ref[...]
        col_id = lax.broadcasted_iota(jnp.int32, gcol.shape, 1)
        lcol = jnp.where(col_id >= nh, _log_sigmoid(gcol), gcol)
        grow = gr_ref[bi] + br_ref[...]
        row_id = lax.broadcasted_iota(jnp.int32, grow.shape, 0)
        lrow = jnp.where(row_id >= nh, _log_sigmoid(grow), grow)
        bcol = jnp.dot(tril, lcol, preferred_element_type=F32, precision=lax.Precision.HIGHEST)
        brow = jnp.dot(lrow, triu, preferred_element_type=F32, precision=lax.Precision.HIGHEST)
        gates.append((lcol, lrow, bcol, brow))

    inv_sqrt = 1.0 / math.sqrt(dh)
    for h in range(nh):
        hs = slice(h * dh, (h + 1) * dh)
        ks = slice(width + h * dh, width + (h + 1) * dh)
        bs = range(nb)
        st = [bi * nh + h for bi in bs]
        qf = [qs_ref[bi, :, hs] for bi in bs]
        q = [x.astype(BF16) for x in qf]
        kf = [qs_ref[bi, :, ks] * inv_sqrt for bi in bs]
        k = [x.astype(BF16) for x in kf]
        v = [v_ref[bi, :, hs] for bi in bs]
        b_c = [gates[bi][2][:, nh + h:nh + h + 1] for bi in bs]
        i_c = [gates[bi][0][:, h:h + 1] for bi in bs]
        b_r = [gates[bi][3][nh + h:nh + h + 1, :] for bi in bs]
        i_r = [gates[bi][1][h:h + 1, :] for bi in bs]
        g_tot = [x[chunk - 1:chunk, :] for x in b_c]
        m_prev = [m_ref[s_][:, 0:1] for s_ in st]
        c_prev = [c_ref[s_] for s_ in st]
        n_prev = [n_ref[s_] for s_ in st]

        qk_t = [lax.dot_general(q[bi], k[bi], (((1,), (1,)), ((), ())), preferred_element_type=F32) for bi in bs]
        q_c = [jnp.dot(q[bi], c_prev[bi].astype(BF16), preferred_element_type=F32) for bi in bs]
        dmat = [jnp.where(causal, b_c[bi] - b_r[bi] + i_r[bi], NEG_INF) for bi in bs]
        inter_log = [b_c[bi] + m_prev[bi] for bi in bs]
        m_row = [jnp.maximum(inter_log[bi], jnp.max(dmat[bi], axis=1, keepdims=True)) for bi in bs]
        sc = [qk_t[bi] * jnp.exp(dmat[bi] - m_row[bi]) for bi in bs]
        inter_scale = [jnp.exp(inter_log[bi] - m_row[bi]) for bi in bs]
        sc_v = [jnp.dot(sc[bi].astype(BF16), v[bi], preferred_element_type=F32) for bi in bs]

        wlog = [g_tot[bi] - b_c[bi] + i_c[bi] for bi in bs]
        m_loc = [jnp.max(x, axis=0, keepdims=True) for x in wlog]
        kw = [kf[bi] * jnp.exp(wlog[bi] - m_loc[bi]) for bi in bs]
        c_chunk = [lax.dot_general(kw[bi].astype(BF16), v[bi], (((0,), (0,)), ((), ())),
                                   preferred_element_type=F32) for bi in bs]

        num = [sc_v[bi] + inter_scale[bi] * q_c[bi] for bi in bs]
        qn = [jnp.sum(qf[bi] * n_prev[bi], axis=1, keepdims=True) for bi in bs]
        den = [jnp.sum(sc[bi], axis=1, keepdims=True) + inter_scale[bi] * qn[bi] for bi in bs]
        hh = [num[bi] / jnp.maximum(jnp.abs(den[bi]), jnp.exp(-m_row[bi])) for bi in bs]
        mu = [jnp.mean(x, axis=1, keepdims=True) for x in hh]
        var = [jnp.mean(jnp.square(hh[bi] - mu[bi]), axis=1, keepdims=True) for bi in bs]
        for bi in bs:
            hn = (hh[bi] - mu[bi]) * lax.rsqrt(var[bi] + LN_EPS) * nw_ref[:, hs]
            y_ref[bi, :, hs] = (o_ref[bi, :, hs].astype(F32) * hn).astype(y_ref.dtype)

        for bi in bs:
            n_chunk = jnp.sum(kw[bi], axis=0, keepdims=True)
            m_new = jnp.maximum(g_tot[bi] + m_prev[bi], m_loc[bi])
            a = jnp.exp(g_tot[bi] + m_prev[bi] - m_new)
            bb = jnp.exp(m_loc[bi] - m_new)
            c_ref[st[bi]] = a * c_prev[bi] + bb * c_chunk[bi]
            n_ref[st[bi]] = a * n_prev[bi] + bb * n_chunk
            m_ref[st[bi]] = jnp.broadcast_to(m_new, (1, LANES))


def _mlstm(qk, v, osig, gcol, grow, bias_c, bias_r, conv_w, norm_w, nh, nb=ML_BATCH):
    bsz, s, w2 = qk.shape
    width = w2 // 2
    dh = width // nh
    chunk = ML_CHUNK
    kconv = conv_w.shape[0]
    nb = math.gcd(nb, bsz)
    assert s % chunk == 0 and kconv - 1 <= SUBLANES
    kern = functools.partial(_mlstm_kernel, nb=nb, nh=nh, dh=dh, chunk=chunk, kconv=kconv)
    return pl.pallas_call(
        kern,
        grid=(bsz // nb, s // chunk),
        in_specs=[pl.BlockSpec((nb, chunk, w2), lambda b, c: (b, c, 0)),
                  pl.BlockSpec((nb, chunk, width), lambda b, c: (b, c, 0)),
                  pl.BlockSpec((nb, chunk, width), lambda b, c: (b, c, 0)),
                  pl.BlockSpec((nb, chunk, LANES), lambda b, c: (b, c, 0)),
                  pl.BlockSpec((nb, 2 * nh, chunk), lambda b, c: (b, 0, c)),
                  _const_spec((1, LANES)),
                  _const_spec((2 * nh, 1)),
                  _const_spec((kconv, w2)),
                  _const_spec((1, width))],
        out_specs=pl.BlockSpec((nb, chunk, width), lambda b, c: (b, c, 0)),
        out_shape=jax.ShapeDtypeStruct((bsz, s, width), BF16),
        scratch_shapes=[pltpu.VMEM((nb, SUBLANES + chunk, w2), F32),
                        pltpu.VMEM((nb, chunk, w2), F32),
                        pltpu.VMEM((nb * nh, dh, dh), F32),
                        pltpu.VMEM((nb * nh, 1, dh), F32),
                        pltpu.VMEM((nb * nh, 1, LANES), F32)],
        compiler_params=_cparams("parallel", "arbitrary"),
        name="mlstm",
    )(qk, v, osig, gcol, grow, bias_c, bias_r, conv_w, norm_w)


def _layer_norm(x, g, b):
    mu = jnp.mean(x, axis=-1, keepdims=True)
    var = jnp.mean(jnp.square(x - mu), axis=-1, keepdims=True)
    return (x - mu) * lax.rsqrt(var + LN_EPS) * g + b


def _mix_kernel(yp_ref, u_ref, yb_ref, ga_ref, gb_ref, x_ref, d_ref, wg_ref, bg_ref, wa_ref, wb_ref, wo_ref,
                lg_ref, lb_ref, h_ref, ht_ref, *, alpha):
    y = jax.nn.gelu(yp_ref[...] + d_ref[...] * u_ref[...])
    gate = jax.nn.sigmoid(jnp.dot(y.astype(BF16), wg_ref[...], preferred_element_type=F32) + bg_ref[...])
    ya = (y * gate).astype(BF16)
    merged = (ga_ref[...].astype(F32) * jnp.dot(ya, wa_ref[...], preferred_element_type=F32)
              + gb_ref[...].astype(F32) * jnp.dot(yb_ref[...], wb_ref[...], preferred_element_type=F32))
    mix = jnp.dot(merged.astype(BF16), wo_ref[...], preferred_element_type=F32)
    h = _layer_norm(alpha * x_ref[...] + mix, lg_ref[...], lb_ref[...])
    h_ref[...] = h
    ht_ref[...] = h.T.astype(BF16)


def _mix(y_pre, u, y_b, gates, x, d, w_glu, b_glu, w_a, w_b, w_o, ln_g, ln_b, alpha, tm=256):
    t, dm = x.shape
    w = u.shape[1]
    assert t % tm == 0
    row = lambda width: pl.BlockSpec((tm, width), lambda i: (i, 0))
    return pl.pallas_call(
        functools.partial(_mix_kernel, alpha=alpha),
        grid=(t // tm,),
        in_specs=[row(w), row(w), row(w),
                  pl.BlockSpec((tm, dm), lambda i: (i, 0)),
                  pl.BlockSpec((tm, dm), lambda i: (i, 1)),
                  row(dm),
                  _const_spec((1, w)), _const_spec((w, w)), _const_spec((1, w)),
                  _const_spec((w, dm)), _const_spec((w, dm)), _const_spec((dm, dm)),
                  _const_spec((1, dm)), _const_spec((1, dm))],
        out_specs=[pl.BlockSpec((tm, dm), lambda i: (i, 0)),
                   pl.BlockSpec((dm, tm), lambda i: (0, i))],
        out_shape=[jax.ShapeDtypeStruct((t, dm), F32), jax.ShapeDtypeStruct((dm, t), BF16)],
        compiler_params=_cparams("parallel"),
        name="mix_ln1",
    )(y_pre, u, y_b, gates, gates, x, d, w_glu, b_glu, w_a, w_b, w_o, ln_g, ln_b)


def _oddeven_merge(lo, hi, r):
    step = r * 2
    if step < hi - lo:
        yield from _oddeven_merge(lo, hi, step)
        yield from _oddeven_merge(lo + r, hi, step)
        yield from [(i, i + r) for i in range(lo + r, hi - r, step)]
    else:
        yield (lo, lo + r)


def _oddeven_merge_sort(lo, hi):
    if hi - lo >= 1:
        mid = lo + (hi - lo) // 2
        yield from _oddeven_merge_sort(lo, mid)
        yield from _oddeven_merge_sort(mid + 1, hi)
        yield from _oddeven_merge(lo, hi, 1)


def _sorted_topk_rows(arrays, dst_refs, k):
    g = arrays[0].shape[0] // SUBLANES
    ws = [[s[j * SUBLANES:(j + 1) * SUBLANES, :] for j in range(g)] for s in arrays]
    for i, j in _oddeven_merge_sort(0, g - 1):
        for w in ws:
            w[i], w[j] = jnp.maximum(w[i], w[j]), jnp.minimum(w[i], w[j])
    for r in range(k):
        left = k - 1 - r
        for w, dst_ref in zip(ws, dst_refs):
            mx = jnp.max(w[0], axis=0, keepdims=True)
            dst_ref[r:r + 1, :] = mx
            if left > 0:
                hit = w[0] == mx
                for lvl in range(min(g, left)):
                    below = w[lvl + 1] if lvl + 1 < g else NEG_INF
                    w[lvl] = jnp.where(hit, below, w[lvl])


def _route_kernel(h_ref, wq_ref, keys_ref, cnt_ref, e1_ref, rank_ref, e2_ref, a_ref, b_ref, cand_ref,
                  *, nh, nk, half, topk, ncand_rows):
    q = jnp.dot(h_ref[...].astype(BF16), wq_ref[...], preferred_element_type=F32).astype(BF16)
    tb = q.shape[0]
    nrank = topk + 1
    for h in range(nh):
        s1 = lax.dot_general(keys_ref[2 * h], q[:, (2 * h) * half:(2 * h + 1) * half],
                             (((1,), (1,)), ((), ())), preferred_element_type=F32)
        s2 = lax.dot_general(keys_ref[2 * h + 1], q[:, (2 * h + 1) * half:(2 * h + 2) * half],
                             (((1,), (1,)), ((), ())), preferred_element_type=F32)
        _sorted_topk_rows([s1, s2], [a_ref, b_ref], nrank)
        a = a_ref[0:nrank, :]
        b = b_ref[0:nrank, :]
        cand_ref[...] = jnp.full((ncand_rows, tb), NEG_INF, F32)
        off = 0
        for i in range(nrank):
            n_i = nrank // (i + 1)
            cand_ref[off:off + n_i, :] = a[i:i + 1, :] + b[0:n_i, :]
            off += n_i
        cur = cand_ref[...]
        top = a[0:1, :] + b[0:1, :]
        z = jnp.zeros((1, tb), F32)
        kth = top
        for r in range(topk):
            kth = jnp.max(cur, axis=0, keepdims=True)
            z = z + jnp.exp(kth - top)
            cur = jnp.where(cur == kth, NEG_INF, cur)
        nxt = jnp.max(cur, axis=0, keepdims=True)
        tau = 0.5 * (kth + nxt)
        thr = tau - s1
        cnt = jnp.zeros_like(s1)
        for r in range(topk):
            cnt = jnp.where(b[r:r + 1, :] >= thr, float(r + 1), cnt)
        rank = jnp.full_like(s2, float(nrank))
        for r in reversed(range(nrank)):
            rank = jnp.where(s2 >= b[r:r + 1, :], float(r), rank)
        cnt_ref[h] = cnt
        e1_ref[h] = jnp.exp(s1 - a[0:1, :])
        rank_ref[h] = rank.astype(rank_ref.dtype)
        e2_ref[h] = (jnp.exp(s2 - b[0:1, :]) / z).astype(e2_ref.dtype)


def _route(h, wq, keys2, nh, tb=256):
    t, dm = h.shape
    nk, half = keys2.shape[1], keys2.shape[2]
    topk = PEER_TOPK
    nrank = topk + 1
    ncand = sum(nrank // (r + 1) for r in range(nrank))
    ncand_rows = -(-ncand // SUBLANES) * SUBLANES
    rank_rows = -(-nrank // SUBLANES) * SUBLANES
    tb = min(tb, t)
    assert t % tb == 0 and nk > nrank
    kern = functools.partial(_route_kernel, nh=nh, nk=nk, half=half, topk=topk, ncand_rows=ncand_rows)
    ospec = pl.BlockSpec((nh, nk, tb), lambda i: (0, 0, i))
    return pl.pallas_call(
        kern,
        grid=(t // tb,),
        in_specs=[pl.BlockSpec((tb, dm), lambda i: (i, 0)),
                  _const_spec(wq.shape), _const_spec(keys2.shape)],
        out_specs=[ospec] * 4,
        out_shape=[jax.ShapeDtypeStruct((nh, nk, t), dt) for dt in (F32, F32, BF16, BF16)],
        scratch_shapes=[pltpu.VMEM((rank_rows, tb), F32), pltpu.VMEM((rank_rows, tb), F32),
                        pltpu.VMEM((ncand_rows, tb), F32)],
        compiler_params=_cparams("parallel"),
        name="peer_route",
    )(h, wq, keys2)


_GELU_K1 = -2.0 * math.sqrt(2.0 / math.pi) * math.log2(math.e)
_GELU_K2 = _GELU_K1 * 0.044715


def _gelu_tanh(x):
    return x / (1.0 + jnp.exp2(x * (_GELU_K1 + _GELU_K2 * (x * x))))


def _expert_kernel(xt_ref, u_ref, vt_ref, cnt_ref, e1_ref, rank_ref, e2_ref, o_ref, sa_ref, sb_ref,
                   *, nh, nk, rows, sub, last):
    e = pl.program_id(1)
    slots = (sa_ref, sb_ref)

    def step(rd_ref, wr_ref):
        for q in range(xt_ref.shape[1] // sub):
            cs = slice(q * sub, (q + 1) * sub)
            if wr_ref is not None:
                wr_ref[:, cs] = jnp.dot(u_ref[...], xt_ref[:, cs], preferred_element_type=F32)
            if rd_ref is None:
                continue
            parts = []
            for r in range(rows):
                g = None
                for h in range(nh):
                    sel = jnp.where(rank_ref[h, :, cs] < cnt_ref[h, r:r + 1, cs].astype(BF16),
                                    e2_ref[h, :, cs] * e1_ref[h, r:r + 1, cs].astype(BF16), 0.0)
                    g = sel if g is None else g + sel
                parts.append(_gelu_tanh(rd_ref[r * nk:(r + 1) * nk, cs]).astype(BF16) * g)
            act = jnp.concatenate(parts, axis=0)
            o_ref[:, cs] += jnp.dot(vt_ref[...], act, preferred_element_type=F32)

    @pl.when(e == 0)
    def _():
        o_ref[...] = jnp.zeros_like(o_ref)
        step(None, slots[1])

    for par in range(2):
        @pl.when((e > 0) & (e < last) & (lax.rem(e, 2) == par))
        def _(par=par):
            step(slots[par], slots[1 - par])

    @pl.when(e == last)
    def _():
        step(slots[last % 2], None)


def _experts(xt, u_tab, vt_tab, cnt, e1, rank, e2, tb=1024, rows=8, sub=256):
    dm, t = xt.shape
    ne = u_tab.shape[0]
    nh, nk, _ = cnt.shape
    neb = rows * nk
    tb = min(tb, t)
    assert t % tb == 0 and tb % sub == 0 and ne % neb == 0 and ne == nk * nk
    nblk = ne // neb
    kern = functools.partial(_expert_kernel, nh=nh, nk=nk, rows=rows, sub=sub, last=nblk)
    once = pl.Buffered(1)
    score_blk = lambda e: jnp.minimum(e, nblk - 1)
    gate_blk = lambda e: jnp.maximum(e - 1, 0)
    return pl.pallas_call(
        kern,
        grid=(t // tb, nblk + 1),
        in_specs=[pl.BlockSpec((dm, tb), lambda i, e: (0, i), pipeline_mode=once),
                  pl.BlockSpec((neb, dm), lambda i, e: (score_blk(e), 0)),
                  pl.BlockSpec((dm, neb), lambda i, e: (0, gate_blk(e))),
                  pl.BlockSpec((nh, rows, tb), lambda i, e: (0, gate_blk(e), i)),
                  pl.BlockSpec((nh, rows, tb), lambda i, e: (0, gate_blk(e), i)),
                  pl.BlockSpec((nh, nk, tb), lambda i, e: (0, 0, i), pipeline_mode=once),
                  pl.BlockSpec((nh, nk, tb), lambda i, e: (0, 0, i), pipeline_mode=once)],
        out_specs=pl.BlockSpec((dm, tb), lambda i, e: (0, i), pipeline_mode=once),
        out_shape=jax.ShapeDtypeStruct((dm, t), F32),
        scratch_shapes=[pltpu.VMEM((neb, tb), F32), pltpu.VMEM((neb, tb), F32)],
        compiler_params=_cparams("parallel", "arbitrary"),
        name="peer_experts",
    )(xt, u_tab, vt_tab, cnt, e1, rank, e2)


def _final_kernel(h_ref, yt_ref, p_ref, lg_ref, lb_ref, wg_ref, wp_ref, o_ref, *, alpha, sub):
    for q in range(h_ref.shape[0] // sub):
        rs = slice(q * sub, (q + 1) * sub)
        h = _layer_norm(alpha * h_ref[rs, :] + yt_ref[:, rs].T, lg_ref[...], lb_ref[...])
        gate = jax.nn.sigmoid(jnp.dot(h.astype(BF16), wg_ref[...], preferred_element_type=F32))
        proj = jnp.dot(p_ref[rs, :].astype(BF16), wp_ref[...], preferred_element_type=F32)
        o_ref[rs, :] = h + gate * proj


def _final(h, yt, p, ln_g, ln_b, w_gate, w_proj, alpha, tm=512, sub=256):
    t, dm = h.shape
    pd = p.shape[1]
    assert t % tm == 0 and tm % sub == 0
    return pl.pallas_call(
        functools.partial(_final_kernel, alpha=alpha, sub=sub),
        grid=(t // tm,),
        in_specs=[pl.BlockSpec((tm, dm), lambda i: (i, 0)),
                  pl.BlockSpec((dm, tm), lambda i: (0, i)),
                  pl.BlockSpec((tm, pd), lambda i: (i, 0)),
                  _const_spec((1, dm)), _const_spec((1, dm)),
                  _const_spec((dm, dm)), _const_spec((pd, dm))],
        out_specs=pl.BlockSpec((tm, dm), lambda i: (i, 0)),
        out_shape=jax.ShapeDtypeStruct((t, dm), F32),
        compiler_params=_cparams("parallel"),
        name="ln2_ple",
    )(h, yt, p, ln_g, ln_b, w_gate, w_proj)


def _layer(h, p, w_in, b_igate, b_fgate, conv_qk, mh_norm_w, a_re, a_im, log_dt, b_re, b_im, c_re, c_im, d_skip,
           w_glu, b_glu, w_up_ssm, w_up_ml, w_out, ln1_g, ln1_b, peer_wq, peer_keys, peer_u, peer_v,
           ln2_g, ln2_b, ple_w_gate, ple_w_proj, alpha):
    bsz, s, dm = h.shape
    t = bsz * s
    nh = b_igate.shape[0]
    ssm_w = d_skip.shape[0]
    ml_w = mh_norm_w.shape[0]
    x2 = h.reshape(t, dm)
    xb = x2.astype(BF16)

    o0 = ssm_w
    o1 = o0 + 2 * ml_w
    o2 = o1 + ml_w
    o3 = o2 + ml_w
    o4 = o3 + 2 * nh
    u, qk, v, osig, gates = _in_proj(
        xb, [w_in[:, :o3].astype(BF16), w_in[:, o4:].astype(BF16)],
        [(0, ssm_w, F32, False), (0, 2 * ml_w, F32, False), (0, ml_w, BF16, False), (0, ml_w, BF16, True),
         (1, w_in.shape[1] - o4, BF16, True)])
    w_if = jnp.zeros((dm, LANES), F32).at[:, :2 * nh].set(w_in[:, o3:o4]).astype(BF16)
    gif = _matmul(xb, w_if, F32)

    bb, cb, lam = _s5_params(a_re, a_im, log_dt, b_re, b_im, c_re, c_im)
    y_pre = _s5_scan(u.reshape(bsz, s, ssm_w), bb, cb, lam)

    gcol = gif.reshape(bsz, s, LANES)
    grow = jnp.swapaxes(gcol[:, :, :2 * nh], 1, 2)
    bias = jnp.concatenate([b_igate, b_fgate])
    bias_c = jnp.zeros((1, LANES), F32).at[0, :2 * nh].set(bias)
    y_b = _mlstm(qk.reshape(bsz, s, 2 * ml_w), v.reshape(bsz, s, ml_w), osig.reshape(bsz, s, ml_w),
                 gcol, grow, bias_c, bias[:, None], conv_qk, mh_norm_w[None, :], nh)

    h1, h1t = _mix(y_pre.reshape(t, ssm_w), u, y_b.reshape(t, ml_w), gates, x2, d_skip[None, :],
                   w_glu.astype(BF16), b_glu[None, :], w_up_ssm.astype(BF16), w_up_ml.astype(BF16),
                   w_out.astype(BF16), ln1_g[None, :], ln1_b[None, :], alpha)

    ph, _, nk, half = peer_keys.shape
    keys2 = peer_keys.reshape(ph * 2, nk, half).astype(BF16)
    cnt, e1, rank, e2 = _route(h1, peer_wq.astype(BF16), keys2, ph)
    y2t = _experts(h1t, peer_u.astype(BF16), peer_v.T.astype(BF16), cnt, e1, rank, e2)

    out = _final(h1, y2t, p.reshape(t, -1), ln2_g[None, :], ln2_b[None, :],
                 ple_w_gate.astype(BF16), ple_w_proj.astype(BF16), alpha)
    return out.reshape(bsz, s, dm)


def kernel(x, p, w_in, b_igate, b_fgate, conv_qk, mh_norm_w, ssm_a_re, ssm_a_im, ssm_log_dt, ssm_b_re, ssm_b_im,
           ssm_c_re, ssm_c_im, ssm_d, w_glu, b_glu, w_up_ssm, w_up_ml, w_out, ln1_g, ln1_b, peer_wq, peer_keys,
           peer_u, peer_v, ln2_g, ln2_b, ple_w_gate, ple_w_proj):
    depth = w_in.shape[0]
    alpha = (2 * depth) ** 0.25
    h = x
    for i in range(depth):
        h = _layer(h, p[i], w_in[i], b_igate[i], b_fgate[i], conv_qk[i], mh_norm_w[i], ssm_a_re[i], ssm_a_im[i],
                   ssm_log_dt[i], ssm_b_re[i], ssm_b_im[i], ssm_c_re[i], ssm_c_im[i], ssm_d[i], w_glu[i], b_glu[i],
                   w_up_ssm[i], w_up_ml[i], w_out[i], ln1_g[i], ln1_b[i], peer_wq[i], peer_keys[i], peer_u[i],
                   peer_v[i], ln2_g[i], ln2_b[i], ple_w_gate[i], ple_w_proj[i], alpha)
    return h
```

```python
import functools
import math

import jax
import jax.numpy as jnp
from jax import lax
from jax.experimental import pallas as pl
from jax.experimental.pallas import tpu as pltpu

F32 = jnp.float32
BF16 = jnp.bfloat16

LN_EPS = 1e-5
PEER_TOPK = 16
LANES = 128
SUBLANES = 8
VMEM_LIMIT = 56 * 1024 * 1024

ML_CHUNK = 256
ML_BATCH = 4
S5_CHUNK = 1024
NEG_INF = float("-inf")


def _cparams(*sem):
    return pltpu.CompilerParams(dimension_semantics=sem, vmem_limit_bytes=VMEM_LIMIT)


def _const_spec(shape):
    nd = len(shape)
    return pl.BlockSpec(shape, lambda *_: (0,) * nd, pipeline_mode=pl.Buffered(1))


def _mm_kernel(a_ref, w_ref, o_ref):
    o_ref[...] = jnp.dot(a_ref[...], w_ref[...], preferred_element_type=F32).astype(o_ref.dtype)


def _matmul(a, w, out_dtype, tm=1024, tn=512):
    m, k = a.shape
    n = w.shape[1]
    tm, tn = min(tm, m), min(tn, n)
    assert m % tm == 0 and n % tn == 0
    return pl.pallas_call(
        _mm_kernel,
        grid=(m // tm, n // tn),
        in_specs=[pl.BlockSpec((tm, k), lambda i, j: (i, 0)),
                  pl.BlockSpec((k, tn), lambda i, j: (0, j))],
        out_specs=pl.BlockSpec((tm, tn), lambda i, j: (i, j)),
        out_shape=jax.ShapeDtypeStruct((m, n), out_dtype),
        compiler_params=_cparams("parallel", "parallel"),
        name="gate_proj",
    )(a, w)


def _in_proj_kernel(a_ref, *refs, segs, nw, sub):
    j = pl.program_id(1)
    w_refs, o_refs = refs[:nw], refs[nw:]
    for o_ref, (lo, hi, wi, sig) in zip(o_refs, segs):
        @pl.when((j >= lo) & (j < hi))
        def _(o_ref=o_ref, w_ref=w_refs[wi], sig=sig):
            for q in range(o_ref.shape[1] // sub):
                cs = slice(q * sub, (q + 1) * sub)
                acc = jnp.dot(a_ref[...], w_ref[:, cs], preferred_element_type=F32)
                o_ref[:, cs] = (jax.nn.sigmoid(acc) if sig else acc).astype(o_ref.dtype)


def _in_proj(a, weights, outs, tm=1024, tn=1024, sub=256):
    m, k = a.shape
    tm = min(tm, m)
    assert m % tm == 0 and all(o[1] % tn == 0 for o in outs)
    segs, ospecs, shapes, step = [], [], [], 0
    wlo = [None] * len(weights)
    wnb = [0] * len(weights)
    for wi, width, dtype, sig in outs:
        nblk = width // tn
        if wlo[wi] is None:
            wlo[wi] = step
        assert wlo[wi] + wnb[wi] == step, "outputs sharing a weight must be adjacent"
        wnb[wi] += nblk
        segs.append((step, step + nblk, wi, sig))
        ospecs.append(pl.BlockSpec((tm, tn), lambda i, j, lo=step, nblk=nblk: (i, jnp.clip(j - lo, 0, nblk - 1))))
        shapes.append(jax.ShapeDtypeStruct((m, width), dtype))
        step += nblk
    assert all(w.shape == (k, nb * tn) for w, nb in zip(weights, wnb))
    wspecs = [pl.BlockSpec((k, tn), lambda i, j, lo=lo, nb=nb: (0, jnp.clip(j - lo, 0, nb - 1)))
              for lo, nb in zip(wlo, wnb)]
    return pl.pallas_call(
        functools.partial(_in_proj_kernel, segs=tuple(segs), nw=len(weights), sub=sub),
        grid=(m // tm, step),
        in_specs=[pl.BlockSpec((tm, k), lambda i, j: (i, 0))] + wspecs,
        out_specs=ospecs,
        out_shape=shapes,
        compiler_params=_cparams("parallel", "arbitrary"),
        name="in_proj",
    )(a, *weights)


def _s5_kernel(u_ref, bb_ref, cb_ref, lam_ref, y_ref, bu_ref, st_ref, *, bsz, chunk, pitch, ntile):
    c = pl.program_id(1)

    @pl.when(c == 0)
    def _():
        st_ref[...] = jnp.zeros_like(st_ref)

    bb = bb_ref[0]
    for b in range(bsz):
        bu = jnp.dot(u_ref[b].astype(BF16), bb, preferred_element_type=F32)
        for k in range(2 * ntile):
            bu_ref[k, b * pitch:b * pitch + chunk, :] = bu[:, k * LANES:(k + 1) * LANES]

    lam = lam_ref[0]
    lr = [jnp.broadcast_to(lam[k:k + 1, :], (bsz, LANES)) for k in range(ntile)]
    li = [jnp.broadcast_to(lam[ntile + k:ntile + k + 1, :], (bsz, LANES)) for k in range(ntile)]

    def step(t, carry):
        new = []
        for k in range(ntile):
            sr, si = carry[2 * k], carry[2 * k + 1]
            rows = pl.ds(t, bsz, stride=pitch)
            nr = lr[k] * sr - li[k] * si + bu_ref[k, rows, :]
            ni = lr[k] * si + li[k] * sr + bu_ref[ntile + k, rows, :]
            bu_ref[k, rows, :] = nr
            bu_ref[ntile + k, rows, :] = ni
            new += [nr, ni]
        return tuple(new)

    init = []
    for k in range(ntile):
        init += [st_ref[k], st_ref[ntile + k]]
    fin = lax.fori_loop(0, chunk, step, tuple(init), unroll=2)
    for k in range(ntile):
        st_ref[k] = fin[2 * k]
        st_ref[ntile + k] = fin[2 * k + 1]

    cb = cb_ref[0]
    for b in range(bsz):
        st = jnp.concatenate([bu_ref[k, b * pitch:b * pitch + chunk, :] for k in range(2 * ntile)], axis=1)
        y_ref[b] = jnp.dot(st.astype(BF16), cb, preferred_element_type=F32)


def _s5_scan(u, bb, cb, lam):
    bsz, s, w = u.shape
    nslab = w // LANES
    ntile = lam.shape[1] // 2
    chunk = min(S5_CHUNK, s)
    pitch = chunk + SUBLANES
    assert s % chunk == 0 and bsz <= SUBLANES
    kern = functools.partial(_s5_kernel, bsz=bsz, chunk=chunk, pitch=pitch, ntile=ntile)
    return pl.pallas_call(
        kern,
        grid=(nslab, s // chunk),
        in_specs=[pl.BlockSpec((bsz, chunk, LANES), lambda j, c: (0, c, j)),
                  pl.BlockSpec((1, LANES, 2 * ntile * LANES), lambda j, c: (j, 0, 0)),
                  pl.BlockSpec((1, 2 * ntile * LANES, LANES), lambda j, c: (j, 0, 0)),
                  pl.BlockSpec((1, 2 * ntile, LANES), lambda j, c: (j, 0, 0))],
        out_specs=pl.BlockSpec((bsz, chunk, LANES), lambda j, c: (0, c, j)),
        out_shape=jax.ShapeDtypeStruct((bsz, s, w), F32),
        scratch_shapes=[pltpu.VMEM((2 * ntile, bsz * pitch, LANES), F32),
                        pltpu.VMEM((2 * ntile, bsz, LANES), F32)],
        compiler_params=_cparams("parallel", "arbitrary"),
        name="s5_scan",
    )(u, bb, cb, lam)


def _s5_params(a_re, a_im, log_dt, b_re, b_im, c_re, c_im):
    g, p = a_re.shape
    gc = b_re.shape[2]
    gps = LANES // gc
    nslab = g // gps
    dt = jnp.exp(log_dt)[:, None]
    mag = jnp.exp(dt * a_re)
    lam_r = mag * jnp.cos(dt * a_im)
    lam_i = mag * jnp.sin(dt * a_im)
    den = a_re * a_re + a_im * a_im
    zr = ((lam_r - 1.0) * a_re + lam_i * a_im) / den
    zi = (lam_i * a_re - (lam_r - 1.0) * a_im) / den
    bbar_r = zr[..., None] * b_re - zi[..., None] * b_im
    bbar_i = zr[..., None] * b_im + zi[..., None] * b_re
    eye = jnp.eye(gps, dtype=F32)

    def in_slab(bm):
        bm = bm.reshape(nslab, gps, p, gc)
        return jnp.einsum("jgpc,gh->jgchp", bm, eye).reshape(nslab, gps * gc, gps * p)

    def out_slab(cm):
        cm = cm.reshape(nslab, gps, gc, p)
        return jnp.einsum("jgcp,gh->jhpgc", cm, eye).reshape(nslab, gps * p, gps * gc)

    bb = jnp.concatenate([in_slab(bbar_r), in_slab(bbar_i)], axis=2).astype(BF16)
    cb = jnp.concatenate([out_slab(c_re), -out_slab(c_im)], axis=1).astype(BF16)
    ntile = gps * p // LANES
    lam = jnp.concatenate([lam_r.reshape(nslab, ntile, LANES), lam_i.reshape(nslab, ntile, LANES)], axis=1)
    return bb, cb, lam


def _log_sigmoid(x):
    return jnp.minimum(x, 0.0) - jnp.log1p(jnp.exp(-jnp.abs(x)))


def _mlstm_kernel(qk_ref, v_ref, o_ref, gc_ref, gr_ref, bc_ref, br_ref, cw_ref, nw_ref, y_ref,
                  buf_ref, qs_ref, c_ref, n_ref, m_ref, *, nb, nh, dh, chunk, kconv):
    ci = pl.program_id(1)
    width = nh * dh
    halo = SUBLANES

    @pl.when(ci == 0)
    def _():
        buf_ref[:, 0:halo, :] = jnp.zeros((nb, halo, 2 * width), F32)
        c_ref[...] = jnp.zeros_like(c_ref)
        n_ref[...] = jnp.zeros_like(n_ref)
        m_ref[...] = jnp.zeros_like(m_ref)

    tt = lax.broadcasted_iota(jnp.int32, (chunk, chunk), 0)
    ss = lax.broadcasted_iota(jnp.int32, (chunk, chunk), 1)
    causal = ss <= tt
    tril = causal.astype(F32)
    triu = (tt <= ss).astype(F32)

    gates = []
    for bi in range(nb):
        buf_ref[bi, halo:halo + chunk, :] = qk_ref[bi]
        base = halo - (kconv - 1)
        conv = cw_ref[0:1, :] * buf_ref[bi, base:base + chunk, :]
        for j in range(1, kconv):
            conv = conv + cw_ref[j:j + 1, :] * buf_ref[bi, base + j:base + j + chunk, :]
        buf_ref[bi, 0:halo, :] = buf_ref[bi, chunk:chunk + halo, :]
        qs_ref[bi] = conv * jax.nn.sigmoid(conv)

        gcol = gc_ref[bi] + bc_ref[...]
        col_id = lax.broadcasted_iota(jnp.int32, gcol.shape, 1)
        lcol = jnp.where(col_id >= nh, _log_sigmoid(gcol), gcol)
        grow = gr_ref[bi] + br_ref[...]
        row_id = lax.broadcasted_iota(jnp.int32, grow.shape, 0)
        lrow = jnp.where(row_id >= nh, _log_sigmoid(grow), grow)
        bcol = jnp.dot(tril, lcol, preferred_element_type=F32, precision=lax.Precision.HIGHEST)
        brow = jnp.dot(lrow, triu, preferred_element_type=F32, precision=lax.Precision.HIGHEST)
        gates.append((lcol, lrow, bcol, brow))

    inv_sqrt = 1.0 / math.sqrt(dh)
    for h in range(nh):
        hs = slice(h * dh, (h + 1) * dh)
        ks = slice(width + h * dh, width + (h + 1) * dh)
        bs = range(nb)
        st = [bi * nh + h for bi in bs]
        qf = [qs_ref[bi, :, hs] for bi in bs]
        q = [x.astype(BF16) for x in qf]
        kf = [qs_ref[bi, :, ks] * inv_sqrt for bi in bs]
        k = [x.astype(BF16) for x in kf]
        v = [v_ref[bi, :, hs] for bi in bs]
        b_c = [gates[bi][2][:, nh + h:nh + h + 1] for bi in bs]
        i_c = [gates[bi][0][:, h:h + 1] for bi in bs]
        b_r = [gates[bi][3][nh + h:nh + h + 1, :] for bi in bs]
        i_r = [gates[bi][1][h:h + 1, :] for bi in bs]
        g_tot = [x[chunk - 1:chunk, :] for x in b_c]
        m_prev = [m_ref[s_][:, 0:1] for s_ in st]
        c_prev = [c_ref[s_] for s_ in st]
        n_prev = [n_ref[s_] for s_ in st]

        qk_t = [lax.dot_general(q[bi], k[bi], (((1,), (1,)), ((), ())), preferred_element_type=F32) for bi in bs]
        q_c = [jnp.dot(q[bi], c_prev[bi].astype(BF16), preferred_element_type=F32) for bi in bs]
        dmat = [jnp.where(causal, b_c[bi] - b_r[bi] + i_r[bi], NEG_INF) for bi in bs]
        inter_log = [b_c[bi] + m_prev[bi] for bi in bs]
        m_row = [jnp.maximum(inter_log[bi], jnp.max(dmat[bi], axis=1, keepdims=True)) for bi in bs]
        sc = [qk_t[bi] * jnp.exp(dmat[bi] - m_row[bi]) for bi in bs]
        inter_scale = [jnp.exp(inter_log[bi] - m_row[bi]) for bi in bs]
        sc_v = [jnp.dot(sc[bi].astype(BF16), v[bi], preferred_element_type=F32) for bi in bs]

        wlog = [g_tot[bi] - b_c[bi] + i_c[bi] for bi in bs]
        m_loc = [jnp.max(x, axis=0, keepdims=True) for x in wlog]
        kw = [kf[bi] * jnp.exp(wlog[bi] - m_loc[bi]) for bi in bs]
        c_chunk = [lax.dot_general(kw[bi].astype(BF16), v[bi], (((0,), (0,)), ((), ())),
                                   preferred_element_type=F32) for bi in bs]

        num = [sc_v[bi] + inter_scale[bi] * q_c[bi] for bi in bs]
        qn = [jnp.sum(qf[bi] * n_prev[bi], axis=1, keepdims=True) for bi in bs]
        den = [jnp.sum(sc[bi], axis=1, keepdims=True) + inter_scale[bi] * qn[bi] for bi in bs]
        hh = [num[bi] / jnp.maximum(jnp.abs(den[bi]), jnp.exp(-m_row[bi])) for bi in bs]
        mu = [jnp.mean(x, axis=1, keepdims=True) for x in hh]
        var = [jnp.mean(jnp.square(hh[bi] - mu[bi]), axis=1, keepdims=True) for bi in bs]
        for bi in bs:
            hn = (hh[bi] - mu[bi]) * lax.rsqrt(var[bi] + LN_EPS) * nw_ref[:, hs]
            y_ref[bi, :, hs] = (o_ref[bi, :, hs].astype(F32) * hn).astype(y_ref.dtype)

        for bi in bs:
            n_chunk = jnp.sum(kw[bi], axis=0, keepdims=True)
            m_new = jnp.maximum(g_tot[bi] + m_prev[bi], m_loc[bi])
            a = jnp.exp(g_tot[bi] + m_prev[bi] - m_new)
            bb = jnp.exp(m_loc[bi] - m_new)
            c_ref[st[bi]] = a * c_prev[bi] + bb * c_chunk[bi]
            n_ref[st[bi]] = a * n_prev[bi] + bb * n_chunk
            m_ref[st[bi]] = jnp.broadcast_to(m_new, (1, LANES))


def _mlstm(qk, v, osig, gcol, grow, bias_c, bias_r, conv_w, norm_w, nh, nb=ML_BATCH):
    bsz, s, w2 = qk.shape
    width = w2 // 2
    dh = width // nh
    chunk = ML_CHUNK
    kconv = conv_w.shape[0]
    nb = math.gcd(nb, bsz)
    assert s % chunk == 0 and kconv - 1 <= SUBLANES
    kern = functools.partial(_mlstm_kernel, nb=nb, nh=nh, dh=dh, chunk=chunk, kconv=kconv)
    return pl.pallas_call(
        kern,
        grid=(bsz // nb, s // chunk),
        in_specs=[pl.BlockSpec((nb, chunk, w2), lambda b, c: (b, c, 0)),
                  pl.BlockSpec((nb, chunk, width), lambda b, c: (b, c, 0)),
                  pl.BlockSpec((nb, chunk, width), lambda b, c: (b, c, 0)),
                  pl.BlockSpec((nb, chunk, LANES), lambda b, c: (b, c, 0)),
                  pl.BlockSpec((nb, 2 * nh, chunk), lambda b, c: (b, 0, c)),
                  _const_spec((1, LANES)),
                  _const_spec((2 * nh, 1)),
                  _const_spec((kconv, w2)),
                  _const_spec((1, width))],
        out_specs=pl.BlockSpec((nb, chunk, width), lambda b, c: (b, c, 0)),
        out_shape=jax.ShapeDtypeStruct((bsz, s, width), BF16),
        scratch_shapes=[pltpu.VMEM((nb, SUBLANES + chunk, w2), F32),
                        pltpu.VMEM((nb, chunk, w2), F32),
                        pltpu.VMEM((nb * nh, dh, dh), F32),
                        pltpu.VMEM((nb * nh, 1, dh), F32),
                        pltpu.VMEM((nb * nh, 1, LANES), F32)],
        compiler_params=_cparams("parallel", "arbitrary"),
        name="mlstm",
    )(qk, v, osig, gcol, grow, bias_c, bias_r, conv_w, norm_w)


def _layer_norm(x, g, b):
    mu = jnp.mean(x, axis=-1, keepdims=True)
    var = jnp.mean(jnp.square(x - mu), axis=-1, keepdims=True)
    return (x - mu) * lax.rsqrt(var + LN_EPS) * g + b


def _mix_kernel(yp_ref, u_ref, yb_ref, ga_ref, gb_ref, x_ref, d_ref, wg_ref, bg_ref, wa_ref, wb_ref, wo_ref,
                lg_ref, lb_ref, h_ref, ht_ref, *, alpha, sub):
    for q in range(x_ref.shape[0] // sub):
        rs = slice(q * sub, (q + 1) * sub)
        y = jax.nn.gelu(yp_ref[rs, :] + d_ref[...] * u_ref[rs, :])
        gate = jax.nn.sigmoid(jnp.dot(y.astype(BF16), wg_ref[...], preferred_element_type=F32) + bg_ref[...])
        ya = (y * gate).astype(BF16)
        merged = (ga_ref[rs, :].astype(F32) * jnp.dot(ya, wa_ref[...], preferred_element_type=F32)
                  + gb_ref[rs, :].astype(F32) * jnp.dot(yb_ref[rs, :], wb_ref[...], preferred_element_type=F32))
        mix = jnp.dot(merged.astype(BF16), wo_ref[...], preferred_element_type=F32)
        h = _layer_norm(alpha * x_ref[rs, :] + mix, lg_ref[...], lb_ref[...])
        h_ref[rs, :] = h
        ht_ref[:, rs] = h.T.astype(BF16)


def _mix(y_pre, u, y_b, gates, x, d, w_glu, b_glu, w_a, w_b, w_o, ln_g, ln_b, alpha, tm=256):
    t, dm = x.shape
    w = u.shape[1]
    assert t % tm == 0
    row = lambda width: pl.BlockSpec((tm, width), lambda i: (i, 0))
    return pl.pallas_call(
        functools.partial(_mix_kernel, alpha=alpha, sub=tm // 2),
        grid=(t // tm,),
        in_specs=[row(w), row(w), row(w),
                  pl.BlockSpec((tm, dm), lambda i: (i, 0)),
                  pl.BlockSpec((tm, dm), lambda i: (i, 1)),
                  row(dm),
                  _const_spec((1, w)), _const_spec((w, w)), _const_spec((1, w)),
                  _const_spec((w, dm)), _const_spec((w, dm)), _const_spec((dm, dm)),
                  _const_spec((1, dm)), _const_spec((1, dm))],
        out_specs=[pl.BlockSpec((tm, dm), lambda i: (i, 0)),
                   pl.BlockSpec((dm, tm), lambda i: (0, i))],
        out_shape=[jax.ShapeDtypeStruct((t, dm), F32), jax.ShapeDtypeStruct((dm, t), BF16)],
        compiler_params=_cparams("parallel"),
        name="mix_ln1",
    )(y_pre, u, y_b, gates, gates, x, d, w_glu, b_glu, w_a, w_b, w_o, ln_g, ln_b)


def _oddeven_merge(lo, hi, r):
    step = r * 2
    if step < hi - lo:
        yield from _oddeven_merge(lo, hi, step)
        yield from _oddeven_merge(lo + r, hi, step)
        yield from [(i, i + r) for i in range(lo + r, hi - r, step)]
    else:
        yield (lo, lo + r)


def _oddeven_merge_sort(lo, hi):
    if hi - lo >= 1:
        mid = lo + (hi - lo) // 2
        yield from _oddeven_merge_sort(lo, mid)
        yield from _oddeven_merge_sort(mid + 1, hi)
        yield from _oddeven_merge(lo, hi, 1)


def _sorted_topk_rows(arrays, dst_refs, k):
    g = arrays[0].shape[0] // SUBLANES
    ws = [[s[j * SUBLANES:(j + 1) * SUBLANES, :] for j in range(g)] for s in arrays]
    for i, j in _oddeven_merge_sort(0, g - 1):
        for w in ws:
            w[i], w[j] = jnp.maximum(w[i], w[j]), jnp.minimum(w[i], w[j])
    for r in range(k):
        left = k - 1 - r
        for w, dst_ref in zip(ws, dst_refs):
            mx = jnp.max(w[0], axis=0, keepdims=True)
            dst_ref[r:r + 1, :] = mx
            if left > 0:
                hit = w[0] == mx
                for lvl in range(min(g, left)):
                    below = w[lvl + 1] if lvl + 1 < g else NEG_INF
                    w[lvl] = jnp.where(hit, below, w[lvl])


def _route_kernel(h_ref, wq_ref, keys_ref, cnt_ref, e1_ref, rank_ref, e2_ref, a_ref, b_ref, cand_ref,
                  *, nh, nk, half, topk, ncand_rows):
    q = jnp.dot(h_ref[...].astype(BF16), wq_ref[...], preferred_element_type=F32).astype(BF16)
    tb = q.shape[0]
    nrank = topk + 1
    for h in range(nh):
        s1 = lax.dot_general(keys_ref[2 * h], q[:, (2 * h) * half:(2 * h + 1) * half],
                             (((1,), (1,)), ((), ())), preferred_element_type=F32)
        s2 = lax.dot_general(keys_ref[2 * h + 1], q[:, (2 * h + 1) * half:(2 * h + 2) * half],
                             (((1,), (1,)), ((), ())), preferred_element_type=F32)
        _sorted_topk_rows([s1, s2], [a_ref, b_ref], nrank)
        a = a_ref[0:nrank, :]
        b = b_ref[0:nrank, :]
        cand_ref[...] = jnp.full((ncand_rows, tb), NEG_INF, F32)
        off = 0
        for i in range(nrank):
            n_i = nrank // (i + 1)
            cand_ref[off:off + n_i, :] = a[i:i + 1, :] + b[0:n_i, :]
            off += n_i
        cur = cand_ref[...]
        top = a[0:1, :] + b[0:1, :]
        z = jnp.zeros((1, tb), F32)
        kth = top
        for r in range(topk):
            kth = jnp.max(cur, axis=0, keepdims=True)
            z = z + jnp.exp(kth - top)
            cur = jnp.where(cur == kth, NEG_INF, cur)
        nxt = jnp.max(cur, axis=0, keepdims=True)
        tau = 0.5 * (kth + nxt)
        thr = tau - s1
        cnt = jnp.zeros_like(s1)
        for r in range(topk):
            cnt = jnp.where(b[r:r + 1, :] >= thr, float(r + 1), cnt)
        rank = jnp.full_like(s2, float(nrank))
        for r in reversed(range(nrank)):
            rank = jnp.where(s2 >= b[r:r + 1, :], float(r), rank)
        cnt_ref[h] = cnt
        e1_ref[h] = jnp.exp(s1 - a[0:1, :])
        rank_ref[h] = rank.astype(rank_ref.dtype)
        e2_ref[h] = (jnp.exp(s2 - b[0:1, :]) / z).astype(e2_ref.dtype)


def _route(h, wq, keys2, nh, tb=256):
    t, dm = h.shape
    nk, half = keys2.shape[1], keys2.shape[2]
    topk = PEER_TOPK
    nrank = topk + 1
    ncand = sum(nrank // (r + 1) for r in range(nrank))
    ncand_rows = -(-ncand // SUBLANES) * SUBLANES
    rank_rows = -(-nrank // SUBLANES) * SUBLANES
    tb = min(tb, t)
    assert t % tb == 0 and nk > nrank
    kern = functools.partial(_route_kernel, nh=nh, nk=nk, half=half, topk=topk, ncand_rows=ncand_rows)
    ospec = pl.BlockSpec((nh, nk, tb), lambda i: (0, 0, i))
    return pl.pallas_call(
        kern,
        grid=(t // tb,),
        in_specs=[pl.BlockSpec((tb, dm), lambda i: (i, 0)),
                  _const_spec(wq.shape), _const_spec(keys2.shape)],
        out_specs=[ospec] * 4,
        out_shape=[jax.ShapeDtypeStruct((nh, nk, t), dt) for dt in (F32, F32, BF16, BF16)],
        scratch_shapes=[pltpu.VMEM((rank_rows, tb), F32), pltpu.VMEM((rank_rows, tb), F32),
                        pltpu.VMEM((ncand_rows, tb), F32)],
        compiler_params=_cparams("parallel"),
        name="peer_route",
    )(h, wq, keys2)


_GELU_K1 = -2.0 * math.sqrt(2.0 / math.pi) * math.log2(math.e)
_GELU_K2 = _GELU_K1 * 0.044715


def _gelu_tanh(x):
    return x / (1.0 + jnp.exp2(x * (_GELU_K1 + _GELU_K2 * (x * x))))


def _expert_kernel(xt_ref, u_ref, vt_ref, cnt_ref, e1_ref, rank_ref, e2_ref, o_ref, sa_ref, sb_ref,
                   *, nh, nk, rows, sub, last):
    e = pl.program_id(1)
    slots = (sa_ref, sb_ref)

    def step(rd_ref, wr_ref):
        for q in range(xt_ref.shape[1] // sub):
            cs = slice(q * sub, (q + 1) * sub)
            if wr_ref is not None:
                wr_ref[:, cs] = jnp.dot(u_ref[...], xt_ref[:, cs], preferred_element_type=F32)
            if rd_ref is None:
                continue
            parts = []
            for r in range(rows):
                g = None
                for h in range(nh):
                    sel = jnp.where(rank_ref[h, :, cs] < cnt_ref[h, r:r + 1, cs].astype(BF16),
                                    e2_ref[h, :, cs] * e1_ref[h, r:r + 1, cs].astype(BF16), 0.0)
                    g = sel if g is None else g + sel
                parts.append(_gelu_tanh(rd_ref[r * nk:(r + 1) * nk, cs]).astype(BF16) * g)
            act = jnp.concatenate(parts, axis=0)
            o_ref[:, cs] += jnp.dot(vt_ref[...], act, preferred_element_type=F32)

    @pl.when(e == 0)
    def _():
        o_ref[...] = jnp.zeros_like(o_ref)
        step(None, slots[1])

    for par in range(2):
        @pl.when((e > 0) & (e < last) & (lax.rem(e, 2) == par))
        def _(par=par):
            step(slots[par], slots[1 - par])

    @pl.when(e == last)
    def _():
        step(slots[last % 2], None)


def _experts(xt, u_tab, vt_tab, cnt, e1, rank, e2, tb=1024, rows=8, sub=256):
    dm, t = xt.shape
    ne = u_tab.shape[0]
    nh, nk, _ = cnt.shape
    neb = rows * nk
    tb = min(tb, t)
    assert t % tb == 0 and tb % sub == 0 and ne % neb == 0 and ne == nk * nk
    nblk = ne // neb
    kern = functools.partial(_expert_kernel, nh=nh, nk=nk, rows=rows, sub=sub, last=nblk)
    once = pl.Buffered(1)
    score_blk = lambda e: jnp.minimum(e, nblk - 1)
    gate_blk = lambda e: jnp.maximum(e - 1, 0)
    return pl.pallas_call(
        kern,
        grid=(t // tb, nblk + 1),
        in_specs=[pl.BlockSpec((dm, tb), lambda i, e: (0, i), pipeline_mode=once),
                  pl.BlockSpec((neb, dm), lambda i, e: (score_blk(e), 0)),
                  pl.BlockSpec((dm, neb), lambda i, e: (0, gate_blk(e))),
                  pl.BlockSpec((nh, rows, tb), lambda i, e: (0, gate_blk(e), i)),
                  pl.BlockSpec((nh, rows, tb), lambda i, e: (0, gate_blk(e), i)),
                  pl.BlockSpec((nh, nk, tb), lambda i, e: (0, 0, i), pipeline_mode=once),
                  pl.BlockSpec((nh, nk, tb), lambda i, e: (0, 0, i), pipeline_mode=once)],
        out_specs=pl.BlockSpec((dm, tb), lambda i, e: (0, i), pipeline_mode=once),
        out_shape=jax.ShapeDtypeStruct((dm, t), F32),
        scratch_shapes=[pltpu.VMEM((neb, tb), F32), pltpu.VMEM((neb, tb), F32)],
        compiler_params=_cparams("parallel", "arbitrary"),
        name="peer_experts",
    )(xt, u_tab, vt_tab, cnt, e1, rank, e2)


def _final_kernel(h_ref, yt_ref, p_ref, lg_ref, lb_ref, wg_ref, wp_ref, o_ref, *, alpha, sub):
    for q in range(h_ref.shape[0] // sub):
        rs = slice(q * sub, (q + 1) * sub)
        h = _layer_norm(alpha * h_ref[rs, :] + yt_ref[:, rs].T, lg_ref[...], lb_ref[...])
        gate = jax.nn.sigmoid(jnp.dot(h.astype(BF16), wg_ref[...], preferred_element_type=F32))
        proj = jnp.dot(p_ref[rs, :].astype(BF16), wp_ref[...], preferred_element_type=F32)
        o_ref[rs, :] = h + gate * proj


def _final(h, yt, p, ln_g, ln_b, w_gate, w_proj, alpha, tm=512, sub=256):
    t, dm = h.shape
    pd = p.shape[1]
    assert t % tm == 0 and tm % sub == 0
    return pl.pallas_call(
        functools.partial(_final_kernel, alpha=alpha, sub=sub),
        grid=(t // tm,),
        in_specs=[pl.BlockSpec((tm, dm), lambda i: (i, 0)),
                  pl.BlockSpec((dm, tm), lambda i: (0, i)),
                  pl.BlockSpec((tm, pd), lambda i: (i, 0)),
                  _const_spec((1, dm)), _const_spec((1, dm)),
                  _const_spec((dm, dm)), _const_spec((pd, dm))],
        out_specs=pl.BlockSpec((tm, dm), lambda i: (i, 0)),
        out_shape=jax.ShapeDtypeStruct((t, dm), F32),
        compiler_params=_cparams("parallel"),
        name="ln2_ple",
    )(h, yt, p, ln_g, ln_b, w_gate, w_proj)


def _layer(h, p, w_in, b_igate, b_fgate, conv_qk, mh_norm_w, a_re, a_im, log_dt, b_re, b_im, c_re, c_im, d_skip,
           w_glu, b_glu, w_up_ssm, w_up_ml, w_out, ln1_g, ln1_b, peer_wq, peer_keys, peer_u, peer_v,
           ln2_g, ln2_b, ple_w_gate, ple_w_proj, alpha):
    bsz, s, dm = h.shape
    t = bsz * s
    nh = b_igate.shape[0]
    ssm_w = d_skip.shape[0]
    ml_w = mh_norm_w.shape[0]
    x2 = h.reshape(t, dm)
    xb = x2.astype(BF16)

    o0 = ssm_w
    o1 = o0 + 2 * ml_w
    o2 = o1 + ml_w
    o3 = o2 + ml_w
    o4 = o3 + 2 * nh
    u, qk, v, osig, gates = _in_proj(
        xb, [w_in[:, :o3].astype(BF16), w_in[:, o4:].astype(BF16)],
        [(0, ssm_w, F32, False), (0, 2 * ml_w, F32, False), (0, ml_w, BF16, False), (0, ml_w, BF16, True),
         (1, w_in.shape[1] - o4, BF16, True)])
    w_if = jnp.zeros((dm, LANES), F32).at[:, :2 * nh].set(w_in[:, o3:o4]).astype(BF16)
    gif = _matmul(xb, w_if, F32)

    bb, cb, lam = _s5_params(a_re, a_im, log_dt, b_re, b_im, c_re, c_im)
    y_pre = _s5_scan(u.reshape(bsz, s, ssm_w), bb, cb, lam)

    gcol = gif.reshape(bsz, s, LANES)
    grow = jnp.swapaxes(gcol[:, :, :2 * nh], 1, 2)
    bias = jnp.concatenate([b_igate, b_fgate])
    bias_c = jnp.zeros((1, LANES), F32).at[0, :2 * nh].set(bias)
    y_b = _mlstm(qk.reshape(bsz, s, 2 * ml_w), v.reshape(bsz, s, ml_w), osig.reshape(bsz, s, ml_w),
                 gcol, grow, bias_c, bias[:, None], conv_qk, mh_norm_w[None, :], nh)

    h1, h1t = _mix(y_pre.reshape(t, ssm_w), u, y_b.reshape(t, ml_w), gates, x2, d_skip[None, :],
                   w_glu.astype(BF16), b_glu[None, :], w_up_ssm.astype(BF16), w_up_ml.astype(BF16),
                   w_out.astype(BF16), ln1_g[None, :], ln1_b[None, :], alpha)

    ph, _, nk, half = peer_keys.shape
    keys2 = peer_keys.reshape(ph * 2, nk, half).astype(BF16)
    cnt, e1, rank, e2 = _route(h1, peer_wq.astype(BF16), keys2, ph)
    y2t = _experts(h1t, peer_u.astype(BF16), peer_v.T.astype(BF16), cnt, e1, rank, e2)

    out = _final(h1, y2t, p.reshape(t, -1), ln2_g[None, :], ln2_b[None, :],
                 ple_w_gate.astype(BF16), ple_w_proj.astype(BF16), alpha)
    return out.reshape(bsz, s, dm)


def kernel(x, p, w_in, b_igate, b_fgate, conv_qk, mh_norm_w, ssm_a_re, ssm_a_im, ssm_log_dt, ssm_b_re, ssm_b_im,
           ssm_c_re, ssm_c_im, ssm_d, w_glu, b_glu, w_up_ssm, w_up_ml, w_out, ln1_g, ln1_b, peer_wq, peer_keys,
           peer_u, peer_v, ln2_g, ln2_b, ple_w_gate, ple_w_proj):
    depth = w_in.shape[0]
    alpha = (2 * depth) ** 0.25
    h = x
    for i in range(depth):
        h = _layer(h, p[i], w_in[i], b_igate[i], b_fgate[i], conv_qk[i], mh_norm_w[i], ssm_a_re[i], ssm_a_im[i],
                   ssm_log_dt[i], ssm_b_re[i], ssm_b_im[i], ssm_c_re[i], ssm_c_im[i], ssm_d[i], w_glu[i], b_glu[i],
                   w_up_ssm[i], w_up_ml[i], w_out[i], ln1_g[i], ln1_b[i], peer_wq[i], peer_keys[i], peer_u[i],
                   peer_v[i], ln2_g[i], ln2_b[i], ple_w_gate[i], ple_w_proj[i], alpha)
    return h
```
